```python
import math
import jax
import jax.numpy as jnp
from jax import lax
import numpy as np

D_MODEL = 1024
BATCH = 16
SEQ = 4096
DEPTH = 1

GRID_W = 64
CTX_LEN = 256
LRU_WIDTH = 1024
LRU_BLOCKS = 8
LRU_BLOCK = LRU_WIDTH // LRU_BLOCKS
LRU_C = 8.0
CONV_W = 4
CONV_PAD = (1, 2)
GDN_HEADS = 8
GDN_DK = 128
GDN_DV = 128
GDN_QK = GDN_HEADS * GDN_DK
GDN_VW = GDN_HEADS * GDN_DV
GDN_CHUNK = 64
QKV_COLS = 2 * GDN_QK + GDN_VW
GDN_SCAN_COLS = QKV_COLS + 4 * GDN_HEADS
OFF_XA = 0
OFF_GA = OFF_XA + LRU_WIDTH
OFF_GDN = OFF_GA + LRU_WIDTH
OFF_Z = OFF_GDN + GDN_SCAN_COLS
OFF_MG = OFF_Z + GDN_VW
IN_COLS = OFF_MG + 2 * D_MODEL
N_GROUPS = 4
EXP_PER_GROUP = 8
N_EXPERTS = N_GROUPS * EXP_PER_GROUP
TOP_K = 2
D_EXPERT = 1024
MOE_BLOCK = 128
LN_EPS = 1e-6
L2_EPS = 1e-6
DN_ALPHA = (2.0 * DEPTH) ** 0.25
DN_BETA = (8.0 * DEPTH) ** -0.25

kernel_name = 'hybrid_rglru_gdn_hmoe_diffusion_block'


def _ln_f32(x):
    xf = x.astype(jnp.float32)
    xc = xf - jnp.mean(xf, axis=-1, keepdims=True)
    return xc * lax.rsqrt(jnp.mean(xc * xc, axis=-1, keepdims=True) + LN_EPS)


def modulate(x, shift, scale):
    return (_ln_f32(x) * (1.0 + scale) + shift).astype(x.dtype)


def post_norm(x, branch, g, b):
    return (_ln_f32(DN_ALPHA * x + branch) * g + b).astype(x.dtype)


def rms_norm(t, w):
    t = t.astype(jnp.float32)
    return t * lax.rsqrt(jnp.mean(t * t, axis=-1, keepdims=True) + LN_EPS) * w


def l2norm(t):
    return t * lax.rsqrt(jnp.sum(t * t, axis=-1, keepdims=True) + L2_EPS)


def dwconv(u, w):
    return lax.conv_general_dilated(
        u, w[:, None, :].astype(u.dtype), window_strides=(1,), padding=(CONV_PAD,),
        dimension_numbers=('NWC', 'WIO', 'NWC'), feature_group_count=u.shape[-1])


def grid_to_colmajor(u, rows):
    b, n, ch = u.shape
    return u.reshape(b, rows, GRID_W, ch).transpose(0, 2, 1, 3).reshape(b, n, ch)


def colmajor_to_grid(u, rows):
    b, n, ch = u.shape
    return u.reshape(b, GRID_W, rows, ch).transpose(0, 2, 1, 3).reshape(b, n, ch)


def rglru_direction(u, wa, ba, wx, bx, lam, h0, reverse):
    b, n, wd = u.shape
    ub = u.reshape(b, n, LRU_BLOCKS, LRU_BLOCK)
    r = jax.nn.sigmoid(jnp.einsum('blni,nij->blnj', ub, wa).reshape(b, n, wd) + ba)
    i = jax.nn.sigmoid(jnp.einsum('blni,nij->blnj', ub, wx).reshape(b, n, wd) + bx)
    log_a = -LRU_C * r * jax.nn.softplus(-lam)
    a = jnp.exp(log_a)
    inp = jnp.sqrt(-jnp.expm1(2.0 * log_a)) * (i * u)

    def step(h, a_b):
        a_t, b_t = a_b
        h = a_t * h + b_t
        return h, h

    h_last, hs = lax.scan(step, h0, (a.swapaxes(0, 1), inp.swapaxes(0, 1)), reverse=reverse)
    return hs.swapaxes(0, 1), h_last


def lru_branch(xa, lp, s0_f, s0_b):
    u = (dwconv(xa, lp['conv_a_w']) + lp['conv_a_b']).astype(jnp.float32)
    h_f, s_f = rglru_direction(u, lp['lru_wa'][0], lp['lru_ba'][0], lp['lru_wx'][0], lp['lru_bx'][0],
                               lp['lru_lambda'][0], s0_f, False)
    h_b, s_b = rglru_direction(u, lp['lru_wa'][1], lp['lru_ba'][1], lp['lru_wx'][1], lp['lru_bx'][1],
                               lp['lru_lambda'][1], s0_b, True)
    return h_f + h_b, s_f, s_b


def unit_lower_inverse(nil):
    eye = jnp.eye(nil.shape[-1], dtype=nil.dtype)
    inv = eye + nil
    power = nil
    for _ in range(int(math.log2(GDN_CHUNK)) - 1):
        power = power @ power
        inv = inv + inv @ power
    return inv


def gated_delta_chunked(q, k, v, g, beta, s0):
    b, n, h, dk = q.shape
    dv = v.shape[-1]
    nc = n // GDN_CHUNK
    c4 = lambda t: t.reshape(b, nc, GDN_CHUNK, h, t.shape[-1]).transpose(0, 3, 1, 2, 4)
    qc, kc, vc = c4(q), c4(k), c4(v)
    gc = g.reshape(b, nc, GDN_CHUNK, h).transpose(0, 3, 1, 2)
    bc = beta.reshape(b, nc, GDN_CHUNK, h).transpose(0, 3, 1, 2)
    G = jnp.cumsum(gc, axis=-1)
    idx = jnp.arange(GDN_CHUNK)
    incl = idx[:, None] >= idx[None, :]
    strict = idx[:, None] > idx[None, :]
    diff = G[..., :, None] - G[..., None, :]
    decay = jnp.where(incl, jnp.exp(jnp.where(incl, diff, 0.0)), 0.0)
    kb = kc * bc[..., None]
    vb = vc * bc[..., None]
    lmat = jnp.where(strict, jnp.einsum('bhncd,bhned->bhnce', kb, kc) * decay, 0.0)
    tmat = unit_lower_inverse(-lmat)
    w_c = tmat @ (kb * jnp.exp(G)[..., None])
    u_c = tmat @ vb
    a_c = jnp.einsum('bhncd,bhned->bhnce', qc, kc) * decay
    qg = qc * jnp.exp(G)[..., None]
    kg = kc * jnp.exp(G[..., -1:] - G)[..., None]
    gl = jnp.exp(G[..., -1])

    def step(s, inp):
        qg_t, kg_t, u_t, w_t, a_t, gl_t = inp
        v_new = u_t - w_t @ s
        o = qg_t @ s + a_t @ v_new
        s = s * gl_t[..., None, None] + jnp.einsum('bhck,bhcv->bhkv', kg_t, v_new)
        return s, o

    xs = tuple(jnp.moveaxis(t, 2, 0) for t in (qg, kg, u_c, w_c, a_c, gl))
    s_last, o = lax.scan(step, s0, xs)
    o = jnp.transpose(o, (1, 0, 3, 2, 4)).reshape(b, n, h, dv)
    return o, s_last


def gdn_branch(u, lp, s0_f, s0_b):
    b, n, _ = u.shape
    qkv = jax.nn.silu(dwconv(u[..., :QKV_COLS], lp['conv_qkv_w'])).astype(jnp.float32)
    q = l2norm(qkv[..., :GDN_QK].reshape(b, n, GDN_HEADS, GDN_DK)) * (GDN_DK ** -0.5)
    k = l2norm(qkv[..., GDN_QK:2 * GDN_QK].reshape(b, n, GDN_HEADS, GDN_DK))
    v = qkv[..., 2 * GDN_QK:].reshape(b, n, GDN_HEADS, GDN_DV)
    lg = u[..., QKV_COLS:].astype(jnp.float32).reshape(b, n, 4, GDN_HEADS)
    a_log, dt_bias = lp['gdn_a_log'], lp['gdn_dt_bias']
    g_f = -jnp.exp(a_log[0]) * jax.nn.softplus(lg[:, :, 0] + dt_bias[0])
    g_b = -jnp.exp(a_log[1]) * jax.nn.softplus(lg[:, :, 1] + dt_bias[1])
    beta_f = jax.nn.sigmoid(lg[:, :, 2])
    beta_b = jax.nn.sigmoid(lg[:, :, 3])
    o_f, s_f = gated_delta_chunked(q, k, v, g_f, beta_f, s0_f)
    rev = lambda t: jnp.flip(t, axis=1)
    o_b, s_b = gated_delta_chunked(rev(q), rev(k), rev(v), rev(g_b), rev(beta_b), s0_b)
    return (o_f + rev(o_b)).reshape(b, n, GDN_VW), s_f, s_b


def mixer_output(p, h_lru, o_gdn, lp):
    b, n, _ = p.shape
    ga = p[..., OFF_GA:OFF_GDN]
    z = p[..., OFF_Z:OFF_MG]
    mg_a = p[..., OFF_MG:OFF_MG + D_MODEL]
    mg_b = p[..., OFF_MG + D_MODEL:]
    ya = (jax.nn.gelu(ga.astype(jnp.float32)) * h_lru).astype(p.dtype)
    zg = jax.nn.silu(z.astype(jnp.float32)).reshape(b, n, GDN_HEADS, GDN_DV)
    yb = (rms_norm(o_gdn.reshape(b, n, GDN_HEADS, GDN_DV), lp['gdn_norm_w']) * zg)
    yb = yb.reshape(b, n, GDN_VW).astype(p.dtype)
    merged = jax.nn.sigmoid(mg_a) * (ya @ lp['w_pa']) + jax.nn.sigmoid(mg_b) * (yb @ lp['w_pb'])
    return merged @ lp['w_out']


def grouped_experts(hf, expert_id, weights, lp):
    t, d = hf.shape
    n_assign = t * TOP_K
    e_flat = expert_id.reshape(n_assign)
    tok_flat = jnp.arange(n_assign) // TOP_K
    w_flat = weights.reshape(n_assign)
    order = jnp.argsort(e_flat)
    e_s, tok_s, w_s = e_flat[order], tok_flat[order], w_flat[order]
    counts = jnp.zeros((N_EXPERTS,), jnp.int32).at[e_flat].add(1)
    padded = (counts + MOE_BLOCK - 1) // MOE_BLOCK * MOE_BLOCK
    ends = jnp.cumsum(counts)
    pends = jnp.cumsum(padded)
    rank = jnp.arange(n_assign) - (ends - counts)[e_s]
    dest = (pends - padded)[e_s] + rank
    n_blocks = (n_assign + N_EXPERTS * (MOE_BLOCK - 1) + MOE_BLOCK - 1) // MOE_BLOCK
    x_pad = jnp.zeros((n_blocks * MOE_BLOCK, d), hf.dtype).at[dest].set(hf[tok_s])
    block_expert = jnp.minimum(
        jnp.searchsorted(pends, jnp.arange(n_blocks) * MOE_BLOCK, side='right'), N_EXPERTS - 1)
    w_gate, w_up, w_down = lp['w_e_gate'], lp['w_e_up'], lp['w_e_down']

    def expert_block(args):
        xb, e = args
        return (jax.nn.silu(xb @ w_gate[e]) * (xb @ w_up[e])) @ w_down[e]

    y_pad = lax.map(expert_block, (x_pad.reshape(n_blocks, MOE_BLOCK, d), block_expert))
    y_pad = y_pad.reshape(n_blocks * MOE_BLOCK, d)
    return jnp.zeros_like(hf).at[tok_s].add(w_s[:, None].astype(hf.dtype) * y_pad[dest])


def hier_moe(hf, lp):
    t = hf.shape[0]
    group_prob = jax.nn.softmax((hf @ lp['w_router_g'] + lp['b_router_g']).astype(jnp.float32), axis=-1)
    p_group, g_idx = lax.top_k(group_prob, 1)
    e_logits = (hf @ lp['w_router_e'] + lp['b_router_e']).astype(jnp.float32)
    e_logits = e_logits.reshape(t, N_GROUPS, EXP_PER_GROUP)
    sel = jnp.broadcast_to(g_idx[:, :, None], (t, 1, EXP_PER_GROUP))
    in_group = jnp.take_along_axis(e_logits, sel, axis=1)[:, 0]
    top_p, top_i = lax.top_k(jax.nn.softmax(in_group, axis=-1), TOP_K)
    weights = p_group * top_p / jnp.sum(top_p, axis=-1, keepdims=True)
    expert_id = g_idx * EXP_PER_GROUP + top_i
    return grouped_experts(hf, expert_id, weights, lp)


def moe_sublayer(xs, shift, scale, gate, lp, ln_g, ln_b):
    h = modulate(xs, shift, scale)
    b, n, d = h.shape
    y = hier_moe(h.reshape(b * n, d), lp).reshape(b, n, d)
    return post_norm(xs, gate * y, ln_g, ln_b)


def setup_inputs(seed: int = 0) -> dict:
    key = jax.random.key(seed)
    ks = jax.random.split(key, 34)
    f32 = jnp.float32
    D = D_MODEL

    def nrm(k, shape, scale):
        return jax.random.normal(k, shape, f32) * scale

    a0 = jax.random.uniform(ks[14], (DEPTH, 2, LRU_WIDTH), f32, 0.9, 0.999)
    s = a0 ** (1.0 / LRU_C)
    dt = jnp.exp(jax.random.uniform(ks[17], (DEPTH, 2, GDN_HEADS), f32, math.log(1e-3), math.log(1e-1)))
    return {
        'x': nrm(ks[0], (BATCH, SEQ, D), 1.0),
        'c': nrm(ks[1], (BATCH, D), 1.0),
        'ctx': nrm(ks[2], (BATCH, CTX_LEN, D), 1.0),
        'c_ctx': nrm(ks[3], (D,), 1.0),
        'w_mod': nrm(ks[4], (DEPTH, D, 6 * D), 0.5 * D ** -0.5),
        'b_mod': nrm(ks[5], (DEPTH, 6 * D), 0.01),
        'w_in': nrm(ks[6], (DEPTH, D, IN_COLS), D ** -0.5),
        'b_in': nrm(ks[7], (DEPTH, IN_COLS), 0.01),
        'conv_a_w': nrm(ks[8], (DEPTH, CONV_W, LRU_WIDTH), CONV_W ** -0.5),
        'conv_a_b': nrm(ks[9], (DEPTH, LRU_WIDTH), 0.01),
        'lru_wa': nrm(ks[10], (DEPTH, 2, LRU_BLOCKS, LRU_BLOCK, LRU_BLOCK), LRU_BLOCK ** -0.5),
        'lru_ba': nrm(ks[11], (DEPTH, 2, LRU_WIDTH), 0.01),
        'lru_wx': nrm(ks[12], (DEPTH, 2, LRU_BLOCKS, LRU_BLOCK, LRU_BLOCK), LRU_BLOCK ** -0.5),
        'lru_bx': nrm(ks[13], (DEPTH, 2, LRU_WIDTH), 0.01),
        'lru_lambda': jnp.log(s) - jnp.log1p(-s),
        'conv_qkv_w': nrm(ks[15], (DEPTH, CONV_W, QKV_COLS), CONV_W ** -0.5),
        'gdn_a_log': jnp.log(jax.random.uniform(ks[16], (DEPTH, 2, GDN_HEADS), f32, 1.0, 16.0)),
        'gdn_dt_bias': dt + jnp.log(-jnp.expm1(-dt)),
        'gdn_norm_w': 1.0 + nrm(ks[18], (DEPTH, GDN_DV), 0.01),
        'w_pa': nrm(ks[19], (DEPTH, LRU_WIDTH, D), DN_BETA * LRU_WIDTH ** -0.5),
        'w_pb': nrm(ks[20], (DEPTH, GDN_VW, D), DN_BETA * GDN_VW ** -0.5),
        'w_out': nrm(ks[21], (DEPTH, D, D), DN_BETA * D ** -0.5),
        'ln1_g': 1.0 + nrm(ks[22], (DEPTH, D), 0.01),
        'ln1_b': nrm(ks[23], (DEPTH, D), 0.01),
        'w_router_g': nrm(ks[24], (DEPTH, D, N_GROUPS), D ** -0.5),
        'b_router_g': nrm(ks[25], (DEPTH, N_GROUPS), 0.01),
        'w_router_e': nrm(ks[26], (DEPTH, D, N_EXPERTS), D ** -0.5),
        'b_router_e': nrm(ks[27], (DEPTH, N_EXPERTS), 0.01),
        'w_e_gate': nrm(ks[28], (DEPTH, N_EXPERTS, D, D_EXPERT), DN_BETA * D ** -0.5),
        'w_e_up': nrm(ks[29], (DEPTH, N_EXPERTS, D, D_EXPERT), DN_BETA * D ** -0.5),
        'w_e_down': nrm(ks[30], (DEPTH, N_EXPERTS, D_EXPERT, D), DN_BETA * D_EXPERT ** -0.5),
        'ln2_g': 1.0 + nrm(ks[31], (DEPTH, D), 0.01),
        'ln2_b': nrm(ks[32], (DEPTH, D), 0.01),
    }


def reference(x, c, ctx, c_ctx, w_mod, b_mod, w_in, b_in, conv_a_w, conv_a_b, lru_wa, lru_ba, lru_wx,
              lru_bx, lru_lambda, conv_qkv_w, gdn_a_log, gdn_dt_bias, gdn_norm_w, w_pa, w_pb, w_out,
              ln1_g, ln1_b, w_router_g, b_router_g, w_router_e, b_router_e, w_e_gate, w_e_up, w_e_down,
              ln2_g, ln2_b):
    bsz, n_lat, _ = x.shape
    rows = n_lat // GRID_W
    x_lat, x_ctx = x, ctx
    for layer in range(DEPTH):
        lp = {
            'conv_a_w': conv_a_w[layer], 'conv_a_b': conv_a_b[layer],
            'lru_wa': lru_wa[layer], 'lru_ba': lru_ba[layer], 'lru_wx': lru_wx[layer],
            'lru_bx': lru_bx[layer], 'lru_lambda': lru_lambda[layer],
            'conv_qkv_w': conv_qkv_w[layer], 'gdn_a_log': gdn_a_log[layer],
            'gdn_dt_bias': gdn_dt_bias[layer], 'gdn_norm_w': gdn_norm_w[layer],
            'w_pa': w_pa[layer], 'w_pb': w_pb[layer], 'w_out': w_out[layer],
            'w_router_g': w_router_g[layer], 'b_router_g': b_router_g[layer],
            'w_router_e': w_router_e[layer], 'b_router_e': b_router_e[layer],
            'w_e_gate': w_e_gate[layer], 'w_e_up': w_e_up[layer], 'w_e_down': w_e_down[layer],
        }
        mod_lat = jnp.split(jax.nn.silu(c) @ w_mod[layer] + b_mod[layer], 6, axis=-1)
        sh1, sc1, g1, sh2, sc2, g2 = [m[:, None, :] for m in mod_lat]
        csh1, csc1, cg1, csh2, csc2, cg2 = jnp.split(
            jax.nn.silu(c_ctx) @ w_mod[layer] + b_mod[layer], 6, axis=-1)
        zero_lru = jnp.zeros((bsz, LRU_WIDTH), jnp.float32)
        zero_gdn = jnp.zeros((bsz, GDN_HEADS, GDN_DK, GDN_DV), jnp.float32)

        p_ctx = modulate(x_ctx, csh1, csc1) @ w_in[layer] + b_in[layer]
        lru_ctx, sa_f, sa_b = lru_branch(p_ctx[..., OFF_XA:OFF_GA], lp, zero_lru, zero_lru)
        gdn_ctx, sb_f, sb_b = gdn_branch(p_ctx[..., OFF_GDN:OFF_Z], lp, zero_gdn, zero_gdn)

        p_lat = modulate(x_lat, sh1, sc1) @ w_in[layer] + b_in[layer]
        lru_lat, _, _ = lru_branch(p_lat[..., OFF_XA:OFF_GA], lp, sa_f, sa_b)
        gdn_lat, _, _ = gdn_branch(grid_to_colmajor(p_lat[..., OFF_GDN:OFF_Z], rows), lp, sb_f, sb_b)
        gdn_lat = colmajor_to_grid(gdn_lat, rows)
        mix_lat = mixer_output(p_lat, lru_lat, gdn_lat, lp)

        if layer < DEPTH - 1:
            mix_ctx = mixer_output(p_ctx, lru_ctx, gdn_ctx, lp)
            x_ctx = post_norm(x_ctx, cg1 * mix_ctx, ln1_g[layer], ln1_b[layer])
            x_ctx = moe_sublayer(x_ctx, csh2, csc2, cg2, lp, ln2_g[layer], ln2_b[layer])

        x_lat = post_norm(x_lat, g1 * mix_lat, ln1_g[layer], ln1_b[layer])
        x_lat = moe_sublayer(x_lat, sh2, sc2, g2, lp, ln2_g[layer], ln2_b[layer])
    return x_lat
```

```python
import functools
import math

import jax
import jax.numpy as jnp
from jax import lax
from jax.experimental import pallas as pl
from jax.experimental.pallas import tpu as pltpu

F32 = jnp.float32
BF16 = jnp.bfloat16

D_MODEL = 1024
GRID_W = 64
LRU_WIDTH = 1024
LRU_BLOCKS = 8
LRU_BLOCK = LRU_WIDTH // LRU_BLOCKS
LRU_C = 8.0
CONV_W = 4
GDN_HEADS = 8
GDN_DK = 128
GDN_DV = 128
GDN_QK = GDN_HEADS * GDN_DK
GDN_VW = GDN_HEADS * GDN_DV
GDN_CHUNK = 64
QKV_COLS = 2 * GDN_QK + GDN_VW
N_GATES = 4 * GDN_HEADS
OFF_XA = 0
OFF_GA = OFF_XA + LRU_WIDTH
OFF_GDN = OFF_GA + LRU_WIDTH
OFF_Z = OFF_GDN + QKV_COLS + N_GATES
OFF_MG = OFF_Z + GDN_VW
N_GROUPS = 4
EXP_PER_GROUP = 8
N_EXPERTS = N_GROUPS * EXP_PER_GROUP
TOP_K = 2
LN_EPS = 1e-6
L2_EPS = 1e-6
DEPTH = 1
DN_ALPHA = (2.0 * DEPTH) ** 0.25

LANES = 128
VMEM_LIMIT = 56 * 1024 * 1024

HI = lax.Precision.HIGHEST


def _cparams(*sem):
    return pltpu.CompilerParams(dimension_semantics=sem, vmem_limit_bytes=VMEM_LIMIT)


def _bdot(a, b):
    return jnp.dot(a.astype(BF16), b.astype(BF16), preferred_element_type=F32)


def _bdot_nt(a, b):
    return lax.dot_general(a.astype(BF16), b.astype(BF16), (((1,), (1,)), ((), ())),
                           preferred_element_type=F32)


def _ln(x):
    mu = jnp.mean(x, axis=-1, keepdims=True)
    xc = x - mu
    var = jnp.mean(xc * xc, axis=-1, keepdims=True)
    return xc * lax.rsqrt(var + LN_EPS)


def _sigmoid(x):
    return 1.0 / (1.0 + jnp.exp(-x))


def _silu(x):
    return x * _sigmoid(x)


def _softplus(x):
    return jnp.maximum(x, 0.0) + jnp.log(1.0 + jnp.exp(-jnp.abs(x)))


def _gelu_tanh(x):
    return 0.5 * x * (1.0 + jnp.tanh(math.sqrt(2.0 / math.pi) * (x + 0.044715 * (x * x * x))))


def _mod_body(c_ref, w_ref, b_ref, o_ref):
    o_ref[...] = jnp.dot(_silu(c_ref[...]), w_ref[...], preferred_element_type=F32,
                         precision=HI) + b_ref[...]


def _mod_vectors(cc, w_mod, b_mod):
    rows, d = cc.shape
    n = w_mod.shape[1]
    tn = 1536
    return pl.pallas_call(
        _mod_body,
        grid=(n // tn,),
        in_specs=[pl.BlockSpec((rows, d), lambda j: (0, 0)),
                  pl.BlockSpec((d, tn), lambda j: (0, j)),
                  pl.BlockSpec((1, tn), lambda j: (0, j))],
        out_specs=pl.BlockSpec((rows, tn), lambda j: (0, j)),
        out_shape=jax.ShapeDtypeStruct((rows, n), F32),
        compiler_params=_cparams("arbitrary"),
        name="mod_vectors",
    )(cc, w_mod, b_mod.reshape(1, n))


def _inproj_r_body(x_ref, sc_ref, sh_ref, w_ref, b_ref, *o_refs, splits, nb):
    tm, d = x_ref.shape
    xn = _ln(x_ref[...]).reshape(tm // nb, nb, d)
    xm = (xn * (1.0 + sc_ref[...])[None] + sh_ref[...][None]).reshape(tm, d).astype(BF16)
    for o_ref, (lo, hi) in zip(o_refs, splits):
        for n0 in range(lo, hi, 512):
            o_ref[:, n0 - lo:n0 - lo + 512] = (
                jnp.dot(xm, w_ref[:, n0:n0 + 512], preferred_element_type=F32) + b_ref[:, n0:n0 + 512])


def _inproj_raster(x_tb, sc, sh, w, b, splits, tm):
    t, d = x_tb.shape
    nb = sc.shape[0]
    n = w.shape[1]
    body = functools.partial(_inproj_r_body, splits=splits, nb=nb)
    return pl.pallas_call(
        body,
        grid=(t // tm,),
        in_specs=[pl.BlockSpec((tm, d), lambda i: (i, 0)),
                  pl.BlockSpec((nb, d), lambda i: (0, 0)),
                  pl.BlockSpec((nb, d), lambda i: (0, 0)),
                  pl.BlockSpec((d, n), lambda i: (0, 0)),
                  pl.BlockSpec((1, n), lambda i: (0, 0))],
        out_specs=[pl.BlockSpec((tm, hi - lo), lambda i: (i, 0)) for lo, hi in splits],
        out_shape=[jax.ShapeDtypeStruct((t, hi - lo), F32) for lo, hi in splits],
        compiler_params=_cparams("parallel"),
        name="inproj_raster",
    )(x_tb, sc, sh, w, b)


def _inproj_g_body(x_ref, sc_ref, sh_ref, w_ref, b_ref, *o_refs, splits):
    gb, ch, d = x_ref.shape
    parts = []
    for i in range(gb):
        xn = _ln(x_ref[i])
        parts.append((xn * (1.0 + sc_ref[i:i + 1, :]) + sh_ref[i:i + 1, :]).astype(BF16))
    xm = jnp.concatenate(parts, axis=0)
    for o_ref, (lo, hi) in zip(o_refs, splits):
        step = 512 if (hi - lo) % 512 == 0 else hi - lo
        for n0 in range(lo, hi, step):
            res = jnp.dot(xm, w_ref[:, n0:n0 + step], preferred_element_type=F32) + b_ref[:, n0:n0 + step]
            for i in range(gb):
                o_ref[0, i, :, n0 - lo:n0 - lo + step] = res[i * ch:(i + 1) * ch]


def _inproj_column(xv, x_index_map, n_chunks, sc, sh, w, b, splits):
    nb, d = sc.shape
    gb = 8
    n = w.shape[1]
    body = functools.partial(_inproj_g_body, splits=splits)
    return pl.pallas_call(
        body,
        grid=(n_chunks, nb // gb),
        in_specs=[pl.BlockSpec((gb, GDN_CHUNK, d), x_index_map),
                  pl.BlockSpec((gb, d), lambda c, g: (g, 0)),
                  pl.BlockSpec((gb, d), lambda c, g: (g, 0)),
                  pl.BlockSpec((d, n), lambda c, g: (0, 0)),
                  pl.BlockSpec((1, n), lambda c, g: (0, 0))],
        out_specs=[pl.BlockSpec((1, gb, GDN_CHUNK, hi - lo), lambda c, g: (c, g, 0, 0)) for lo, hi in splits],
        out_shape=[jax.ShapeDtypeStruct((n_chunks, nb, GDN_CHUNK, hi - lo), F32) for lo, hi in splits],
        compiler_params=_cparams("parallel", "parallel"),
        name="inproj_column",
    )(xv, sc, sh, w, b)


def _lru_body(*refs, tt, nb, reverse, emit):
    if emit == "ya":
        (xa_ref, prev_ref, next_ref, cw_ref, cb_ref, wa_ref, ba_ref, wx_ref, bx_ref, lam_ref, h0_ref,
         hf_ref, ga_ref, out_ref, hl_ref, a_s, b_s, h_s) = refs
    elif emit == "h":
        (xa_ref, prev_ref, next_ref, cw_ref, cb_ref, wa_ref, ba_ref, wx_ref, bx_ref, lam_ref, h0_ref,
         out_ref, hl_ref, a_s, b_s, h_s) = refs
    else:
        (xa_ref, prev_ref, next_ref, cw_ref, cb_ref, wa_ref, ba_ref, wx_ref, bx_ref, lam_ref, h0_ref,
         hl_ref, a_s, b_s, h_s) = refs
    step = pl.program_id(0)
    nsteps = pl.num_programs(0)
    blk = (nsteps - 1 - step) if reverse else step
    rows = tt * nb

    @pl.when(step == 0)
    def _():
        h_s[...] = h0_ref[...]

    prev = jnp.where(blk == 0, 0.0, prev_ref[...])
    nxt = jnp.where(blk == nsteps - 1, 0.0, next_ref[...])
    ext = jnp.concatenate([prev, xa_ref[...], nxt], axis=0)
    u = (cw_ref[0:1, :] * ext[0:rows] + cw_ref[1:2, :] * ext[nb:rows + nb]
         + cw_ref[2:3, :] * ext[2 * nb:rows + 2 * nb] + cw_ref[3:4, :] * ext[3 * nb:rows + 3 * nb]
         + cb_ref[...])
    for n in range(LRU_BLOCKS):
        sl = slice(n * LRU_BLOCK, (n + 1) * LRU_BLOCK)
        un = u[:, sl]
        ub = un.astype(BF16)
        r = _sigmoid(jnp.dot(ub, wa_ref[n], preferred_element_type=F32) + ba_ref[:, sl])
        ig = _sigmoid(jnp.dot(ub, wx_ref[n], preferred_element_type=F32) + bx_ref[:, sl])
        log_a = (-LRU_C) * r * _softplus(-lam_ref[:, sl])
        a = jnp.exp(log_a)
        a_s[:, sl] = a
        b_s[:, sl] = jnp.sqrt(1.0 - a * a) * (ig * un)

    def scan_step(j, h):
        t = (tt - 1 - j) if reverse else j
        r0 = pl.multiple_of(t * nb, nb)
        h = a_s[pl.ds(r0, nb), :] * h + b_s[pl.ds(r0, nb), :]
        if emit == "h":
            out_ref[pl.ds(r0, nb), :] = h
        elif emit == "ya":
            b_s[pl.ds(r0, nb), :] = h
        return h

    h = lax.fori_loop(0, tt, scan_step, h_s[...], unroll=4)
    h_s[...] = h
    hl_ref[...] = h
    if emit == "ya":
        out_ref[...] = (_gelu_tanh(ga_ref[...]) * (hf_ref[...] + b_s[...])).astype(BF16)


def _lru_direction(xa, lp, di, h0, reverse, emit, tt, hf=None, ga=None):
    t, w = xa.shape
    nb = h0.shape[0]
    rows = tt * nb
    nblk = t // rows
    assert rows % (2 * nb) == 0
    prev_per = rows // nb
    next_per = rows // (2 * nb)
    n_prev = t // nb
    n_next = t // (2 * nb)

    def bi(i):
        return (nblk - 1 - i) if reverse else i

    const2 = lambda i: (0, 0)
    in_specs = [
        pl.BlockSpec((rows, w), lambda i: (bi(i), 0)),
        pl.BlockSpec((nb, w), lambda i: (jnp.maximum(bi(i) * prev_per - 1, 0), 0)),
        pl.BlockSpec((2 * nb, w), lambda i: (jnp.minimum((bi(i) + 1) * next_per, n_next - 1), 0)),
        pl.BlockSpec((CONV_W, w), const2),
        pl.BlockSpec((1, w), const2),
        pl.BlockSpec((LRU_BLOCKS, LRU_BLOCK, LRU_BLOCK), lambda i: (0, 0, 0)),
        pl.BlockSpec((1, w), const2),
        pl.BlockSpec((LRU_BLOCKS, LRU_BLOCK, LRU_BLOCK), lambda i: (0, 0, 0)),
        pl.BlockSpec((1, w), const2),
        pl.BlockSpec((1, w), const2),
        pl.BlockSpec((nb, w), const2),
    ]
    args = [xa, xa, xa, lp["conv_a_w"], lp["conv_a_b"].reshape(1, w),
            lp["lru_wa"][di].astype(BF16), lp["lru_ba"][di].reshape(1, w),
            lp["lru_wx"][di].astype(BF16), lp["lru_bx"][di].reshape(1, w),
            lp["lru_lambda"][di].reshape(1, w), h0]
    out_specs = []
    out_shape = []
    if emit == "ya":
        in_specs += [pl.BlockSpec((rows, w), lambda i: (bi(i), 0)),
                     pl.BlockSpec((rows, w), lambda i: (bi(i), 0))]
        args += [hf, ga]
        out_specs.append(pl.BlockSpec((rows, w), lambda i: (bi(i), 0)))
        out_shape.append(jax.ShapeDtypeStruct((t, w), BF16))
    elif emit == "h":
        out_specs.append(pl.BlockSpec((rows, w), lambda i: (bi(i), 0)))
        out_shape.append(jax.ShapeDtypeStruct((t, w), F32))
    out_specs.append(pl.BlockSpec((nb, w), const2))
    out_shape.append(jax.ShapeDtypeStruct((nb, w), F32))
    body = functools.partial(_lru_body, tt=tt, nb=nb, reverse=reverse, emit=emit)
    return pl.pallas_call(
        body,
        grid=(nblk,),
        in_specs=in_specs,
        out_specs=out_specs,
        out_shape=out_shape,
        scratch_shapes=[pltpu.VMEM((rows, w), F32), pltpu.VMEM((rows, w), F32), pltpu.VMEM((nb, w), F32)],
        compiler_params=_cparams("arbitrary"),
        name="lru_" + ("bwd" if reverse else "fwd") + "_" + emit,
    )(*args)


def _gdn_prep_body(qkv_ref, prev_ref, next_ref, lg_ref, cw_ref, ga_ref, gd_ref, qkv_o, g_o):
    c = pl.program_id(0)
    nc = pl.num_programs(0)
    ch = GDN_CHUNK
    prev = jnp.where(c == 0, 0.0, prev_ref[0, 0])
    nxt = jnp.where(c == nc - 1, 0.0, next_ref[0, 0])
    ext = jnp.concatenate([prev, qkv_ref[0, 0], nxt], axis=0)
    u = (cw_ref[0:1, :] * ext[7:7 + ch] + cw_ref[1:2, :] * ext[8:8 + ch]
         + cw_ref[2:3, :] * ext[9:9 + ch] + cw_ref[3:4, :] * ext[10:10 + ch])
    act = _silu(u)
    for h in range(GDN_HEADS):
        sq = slice(h * GDN_DK, (h + 1) * GDN_DK)
        qh = act[:, sq]
        qkv_o[0, 0, :, sq] = qh * lax.rsqrt(jnp.sum(qh * qh, axis=-1, keepdims=True) + L2_EPS) * (GDN_DK ** -0.5)
        sk = slice(GDN_QK + h * GDN_DK, GDN_QK + (h + 1) * GDN_DK)
        kh = act[:, sk]
        qkv_o[0, 0, :, sk] = kh * lax.rsqrt(jnp.sum(kh * kh, axis=-1, keepdims=True) + L2_EPS)
    qkv_o[0, 0, :, 2 * GDN_QK:] = act[:, 2 * GDN_QK:]
    lg = lg_ref[0, 0]
    lane = lax.broadcasted_iota(jnp.int32, lg.shape, 1)
    decay = ga_ref[...] * _softplus(lg + gd_ref[...])
    g_o[0, 0] = jnp.where(lane < 2 * GDN_HEADS, decay, jnp.where(lane < N_GATES, _sigmoid(lg), 0.0))


def _gdn_prep(qkv, lg, conv_w, a_row, dt_row):
    nc, nb, ch, n = qkv.shape
    hb = ch // 8
    return pl.pallas_call(
        _gdn_prep_body,
        grid=(nc, nb),
        in_specs=[pl.BlockSpec((1, 1, ch, n), lambda c, b: (c, b, 0, 0)),
                  pl.BlockSpec((1, 1, 8, n), lambda c, b: (jnp.maximum(c - 1, 0), b, hb - 1, 0)),
                  pl.BlockSpec((1, 1, 8, n), lambda c, b: (jnp.minimum(c + 1, nc - 1), b, 0, 0)),
                  pl.BlockSpec((1, 1, ch, LANES), lambda c, b: (c, b, 0, 0)),
                  pl.BlockSpec((CONV_W, n), lambda c, b: (0, 0)),
                  pl.BlockSpec((1, LANES), lambda c, b: (0, 0)),
                  pl.BlockSpec((1, LANES), lambda c, b: (0, 0))],
        out_specs=[pl.BlockSpec((1, 1, ch, n), lambda c, b: (c, b, 0, 0)),
                   pl.BlockSpec((1, 1, ch, LANES), lambda c, b: (c, b, 0, 0))],
        out_shape=[jax.ShapeDtypeStruct(qkv.shape, F32), jax.ShapeDtypeStruct(lg.shape, F32)],
        compiler_params=_cparams("parallel", "parallel"),
        name="gdn_prep",
    )(qkv, qkv, qkv, lg, conv_w, a_row, dt_row)


def _gdn_scan_body(qkv_ref, g_ref, s0_ref, *rest, reverse, emit_o):
    if emit_o:
        o_ref, sl_ref, s_s = rest
    else:
        sl_ref, s_s = rest
    step = pl.program_id(1)
    ch = GDN_CHUNK

    @pl.when(step == 0)
    def _():
        s_s[...] = s0_ref[0]

    row = lax.broadcasted_iota(jnp.int32, (ch, ch), 0)
    col = lax.broadcasted_iota(jnp.int32, (ch, ch), 1)
    incl = (row <= col) if reverse else (row >= col)
    strict = (row < col) if reverse else (row > col)
    eye = (row == col).astype(F32)
    gates = g_ref[0, 0]
    gcum = jnp.dot(incl.astype(F32), gates, preferred_element_type=F32, precision=HI)
    gcum_t = gcum.T
    gtot = jnp.sum(gates, axis=0, keepdims=True)
    goff = GDN_HEADS if reverse else 0
    boff = 2 * GDN_HEADS + goff
    for h in range(GDN_HEADS):
        q = qkv_ref[0, 0, :, h * GDN_DK:(h + 1) * GDN_DK]
        k = qkv_ref[0, 0, :, GDN_QK + h * GDN_DK:GDN_QK + (h + 1) * GDN_DK]
        v = qkv_ref[0, 0, :, 2 * GDN_QK + h * GDN_DV:2 * GDN_QK + (h + 1) * GDN_DV]
        gc = gcum[:, goff + h:goff + h + 1]
        gr = gcum_t[goff + h:goff + h + 1, :]
        gt = gtot[:, goff + h:goff + h + 1]
        beta = gates[:, boff + h:boff + h + 1]
        decay = jnp.where(incl, jnp.exp(jnp.where(incl, gc - gr, 0.0)), 0.0)
        eg = jnp.exp(gc)
        kb = k * beta
        vb = v * beta
        nil = -jnp.where(strict, _bdot_nt(kb, k) * decay, 0.0)
        inv = eye + nil
        power = nil
        for _ in range(int(math.log2(ch)) - 1):
            power = _bdot(power, power)
            inv = inv + _bdot(inv, power)
        w = _bdot(inv, kb * eg)
        u = _bdot(inv, vb)
        a = _bdot_nt(q, k) * decay
        qg = q * eg
        kg = k * jnp.exp(gt - gc)
        s = s_s[h]
        v_new = u - _bdot(w, s)
        if emit_o:
            o_ref[0, 0, :, h * GDN_DV:(h + 1) * GDN_DV] = _bdot(qg, s) + _bdot(a, v_new)
        s_new = s * jnp.exp(gt) + _bdot(kg.T, v_new)
        s_s[h] = s_new
        sl_ref[0, h] = s_new


def _gdn_direction(qkv, gates, s0, reverse, emit_o):
    nc, nb, ch, n = qkv.shape

    def ci(c):
        return (nc - 1 - c) if reverse else c

    out_specs = []
    out_shape = []
    if emit_o:
        out_specs.append(pl.BlockSpec((1, 1, ch, GDN_VW), lambda b, c: (ci(c), b, 0, 0)))
        out_shape.append(jax.ShapeDtypeStruct((nc, nb, ch, GDN_VW), F32))
    out_specs.append(pl.BlockSpec((1, GDN_HEADS, GDN_DK, GDN_DV), lambda b, c: (b, 0, 0, 0)))
    out_shape.append(jax.ShapeDtypeStruct((nb, GDN_HEADS, GDN_DK, GDN_DV), F32))
    body = functools.partial(_gdn_scan_body, reverse=reverse, emit_o=emit_o)
    return pl.pallas_call(
        body,
        grid=(nb, nc),
        in_specs=[pl.BlockSpec((1, 1, ch, n), lambda b, c: (ci(c), b, 0, 0)),
                  pl.BlockSpec((1, 1, ch, LANES), lambda b, c: (ci(c), b, 0, 0)),
                  pl.BlockSpec((1, GDN_HEADS, GDN_DK, GDN_DV), lambda b, c: (b, 0, 0, 0))],
        out_specs=out_specs,
        out_shape=out_shape,
        scratch_shapes=[pltpu.VMEM((GDN_HEADS, GDN_DK, GDN_DV), F32)],
        compiler_params=_cparams("parallel", "arbitrary"),
        name="gdn_scan_" + ("bwd" if reverse else "fwd"),
    )(qkv, gates, s0)


def _gdn_out_body(of_ref, ob_ref, z_ref, nw_ref, y_ref):
    gb = of_ref.shape[1]
    for i in range(gb):
        o = of_ref[0, i] + ob_ref[0, i]
        zg = _silu(z_ref[0, i])
        for h in range(GDN_HEADS):
            sl = slice(h * GDN_DV, (h + 1) * GDN_DV)
            oh = o[:, sl]
            nh = oh * lax.rsqrt(jnp.mean(oh * oh, axis=-1, keepdims=True) + LN_EPS) * nw_ref[...]
            y_ref[:, i * GDN_VW + h * GDN_DV:i * GDN_VW + (h + 1) * GDN_DV] = (nh * zg[:, sl]).astype(BF16)


def _gdn_out(o_f, o_b, z, norm_w):
    nc, nb, ch, vw = o_f.shape
    gb = 8
    rows = ch
    spec = pl.BlockSpec((1, gb, ch, vw), lambda c, g: (c, g, 0, 0))
    y = pl.pallas_call(
        _gdn_out_body,
        grid=(nc, nb // gb),
        in_specs=[spec, spec, spec, pl.BlockSpec((1, GDN_DV), lambda c, g: (0, 0))],
        out_specs=pl.BlockSpec((rows, gb * vw), lambda c, g: (0, c * (nb // gb) + g)),
        out_shape=jax.ShapeDtypeStruct((rows, nc * nb * vw), BF16),
        compiler_params=_cparams("parallel", "parallel"),
        name="gdn_out",
    )(o_f, o_b, z, norm_w.reshape(1, GDN_DV))
    return y.reshape(rows * nc * nb, vw)


def _mixer_body(ya_ref, yb_ref, mg_ref, x_ref, g1_ref, sh2_ref, sc2_ref, wpa_ref, wpb_ref, wo_ref,
                lg_ref, lb_ref, wr_ref, br_ref, x1_ref, h2_ref, rt_ref, *, nb):
    tm, d = x_ref.shape
    pa = jnp.dot(ya_ref[...], wpa_ref[...], preferred_element_type=F32)
    pb = jnp.dot(yb_ref[...], wpb_ref[...], preferred_element_type=F32)
    merged = _sigmoid(mg_ref[:, :d]) * pa + _sigmoid(mg_ref[:, d:]) * pb
    mix = _bdot(merged, wo_ref[...])
    mix3 = mix.reshape(tm // nb, nb, d) * g1_ref[...][None]
    x1 = _ln(DN_ALPHA * x_ref[...] + mix3.reshape(tm, d)) * lg_ref[...] + lb_ref[...]
    x1_ref[...] = x1
    h3 = _ln(x1).reshape(tm // nb, nb, d) * (1.0 + sc2_ref[...])[None] + sh2_ref[...][None]
    h2 = h3.reshape(tm, d)
    h2_ref[...] = h2.astype(BF16)

    logits = jnp.dot(h2, wr_ref[...], preferred_element_type=F32, precision=HI) + br_ref[...]
    lane = lax.broadcasted_iota(jnp.int32, logits.shape, 1).astype(F32)
    neg = jnp.float32(-jnp.inf)
    big = jnp.float32(1 << 20)
    is_g = lane < N_GROUPS
    gl = jnp.where(is_g, logits, neg)
    gmax = jnp.max(gl, axis=-1, keepdims=True)
    gsum = jnp.sum(jnp.where(is_g, jnp.exp(gl - gmax), 0.0), axis=-1, keepdims=True)
    p_group = 1.0 / gsum
    g_idx = jnp.min(jnp.where(gl == gmax, lane, big), axis=-1, keepdims=True)
    lo = N_GROUPS + g_idx * EXP_PER_GROUP
    in_g = (lane >= lo) & (lane < lo + EXP_PER_GROUP)
    el = jnp.where(in_g, logits, neg)
    m1 = jnp.max(el, axis=-1, keepdims=True)
    i1 = jnp.min(jnp.where(el == m1, lane, big), axis=-1, keepdims=True)
    el2 = jnp.where(lane == i1, neg, el)
    m2 = jnp.max(el2, axis=-1, keepdims=True)
    i2 = jnp.min(jnp.where(el2 == m2, lane, big), axis=-1, keepdims=True)
    esum = jnp.sum(jnp.where(in_g, jnp.exp(el - m1), 0.0), axis=-1, keepdims=True)
    p1 = 1.0 / esum
    p2 = jnp.exp(m2 - m1) / esum
    w1 = p_group * p1 / (p1 + p2)
    w2 = p_group * p2 / (p1 + p2)
    e1 = i1 - N_GROUPS
    e2 = i2 - N_GROUPS
    rt_ref[...] = jnp.where(lane == 0, e1, jnp.where(lane == 1, e2, jnp.where(lane == 2, w1,
                            jnp.where(lane == 3, w2, 0.0))))


def _mixer(ya, yb, mg, x_tb, g1, sh2, sc2, w_pa, w_pb, w_out, ln_g, ln_b, w_r, b_r, tm):
    t, d = x_tb.shape
    nb = g1.shape[0]
    row = lambda n: pl.BlockSpec((tm, n), lambda i: (i, 0))
    full = lambda a: pl.BlockSpec(a.shape, lambda i: (0,) * a.ndim)
    args = [ya, yb, mg, x_tb, g1, sh2, sc2, w_pa, w_pb, w_out, ln_g, ln_b, w_r, b_r]
    in_specs = [row(d), row(d), row(2 * d), row(d)] + [full(a) for a in args[4:]]
    body = functools.partial(_mixer_body, nb=nb)
    return pl.pallas_call(
        body,
        grid=(t // tm,),
        in_specs=in_specs,
        out_specs=[row(d), row(d), row(LANES)],
        out_shape=[jax.ShapeDtypeStruct((t, d), F32), jax.ShapeDtypeStruct((t, d), BF16),
                   jax.ShapeDtypeStruct((t, LANES), F32)],
        compiler_params=_cparams("parallel"),
        name="mixer_router",
    )(*args)


def _experts_body(be_ref, nu_ref, x_ref, wg_ref, wu_ref, wd_ref, y_ref):
    i = pl.program_id(0)

    @pl.when(i < nu_ref[0])
    def _():
        x = x_ref[...]
        hg = jnp.dot(x, wg_ref[0], preferred_element_type=F32)
        hu = jnp.dot(x, wu_ref[0], preferred_element_type=F32)
        y_ref[...] = _bdot(_silu(hg) * hu, wd_ref[0])

    @pl.when(i >= nu_ref[0])
    def _():
        y_ref[...] = jnp.zeros_like(y_ref)


def _experts(x_pad, block_expert, n_used, w_gate, w_up, w_down, tm):
    rows, d = x_pad.shape
    de = w_gate.shape[2]
    nblk = rows // tm
    wspec = lambda k, n: pl.BlockSpec((1, k, n), lambda i, be, nu: (be[i], 0, 0))
    return pl.pallas_call(
        _experts_body,
        grid_spec=pltpu.PrefetchScalarGridSpec(
            num_scalar_prefetch=2,
            grid=(nblk,),
            in_specs=[pl.BlockSpec((tm, d), lambda i, be, nu: (i, 0)),
                      wspec(d, de), wspec(d, de), wspec(de, d)],
            out_specs=pl.BlockSpec((tm, d), lambda i, be, nu: (i, 0)),
        ),
        out_shape=jax.ShapeDtypeStruct((rows, d), F32),
        compiler_params=_cparams("arbitrary"),
        name="experts",
    )(block_expert, n_used, x_pad, w_gate, w_up, w_down)


def _final_body(x1_ref, y1_ref, y2_ref, w_ref, g2_ref, lg_ref, lb_ref, o_ref):
    b = pl.program_id(0)
    y = w_ref[0, :, 0:1] * y1_ref[0] + w_ref[0, :, 1:2] * y2_ref[0]
    g2 = g2_ref[pl.ds(b, 1), :]
    o_ref[0] = _ln(DN_ALPHA * x1_ref[...] + g2 * y) * lg_ref[...] + lb_ref[...]


def _final(x1_tb, y1, y2, wts, g2, ln_g, ln_b, tq):
    nb, length, d = y1.shape
    x1v = x1_tb.reshape(length, nb * d)
    return pl.pallas_call(
        _final_body,
        grid=(nb, length // tq),
        in_specs=[pl.BlockSpec((tq, d), lambda b, i: (i, b)),
                  pl.BlockSpec((1, tq, d), lambda b, i: (b, i, 0)),
                  pl.BlockSpec((1, tq, d), lambda b, i: (b, i, 0)),
                  pl.BlockSpec((1, tq, LANES), lambda b, i: (b, i, 0)),
                  pl.BlockSpec((nb, d), lambda b, i: (0, 0)),
                  pl.BlockSpec((1, d), lambda b, i: (0, 0)),
                  pl.BlockSpec((1, d), lambda b, i: (0, 0))],
        out_specs=pl.BlockSpec((1, tq, d), lambda b, i: (b, i, 0)),
        out_shape=jax.ShapeDtypeStruct((nb, length, d), F32),
        compiler_params=_cparams("parallel", "parallel"),
        name="moe_combine_postnorm",
    )(x1v, y1, y2, wts, g2, ln_g, ln_b)


MOE_TM = 512


def _route_plan(e1, e2, tm):
    t = e1.shape[0]
    e_flat = jnp.stack([e1, e2], axis=1).reshape(-1)
    n_assign = e_flat.shape[0]
    onehot = (e_flat[:, None] == jnp.arange(N_EXPERTS, dtype=jnp.int32)[None, :]).astype(jnp.int32)
    csum = jnp.cumsum(onehot, axis=0)
    rank = jnp.sum(jnp.where(onehot > 0, csum - 1, 0), axis=1)
    counts = csum[-1]
    padded = (counts + tm - 1) // tm * tm
    pends = jnp.cumsum(padded)
    dest = (pends - padded)[e_flat] + rank
    n_blocks = (n_assign + N_EXPERTS * (tm - 1) + tm - 1) // tm
    n_used = (pends[-1] // tm).astype(jnp.int32).reshape(1)
    block_expert = jnp.minimum(
        jnp.searchsorted(pends, jnp.arange(n_blocks, dtype=jnp.int32) * tm, side="right"),
        N_EXPERTS - 1).astype(jnp.int32)
    tok = jnp.arange(n_assign, dtype=jnp.int32) // TOP_K
    src = jnp.zeros((n_blocks * tm,), jnp.int32).at[dest].set(tok)
    return src, dest.reshape(t, TOP_K), block_expert, n_used


def kernel(x, c, ctx, c_ctx, w_mod, b_mod, w_in, b_in, conv_a_w, conv_a_b, lru_wa, lru_ba, lru_wx, lru_bx,
           lru_lambda, conv_qkv_w, gdn_a_log, gdn_dt_bias, gdn_norm_w, w_pa, w_pb, w_out, ln1_g, ln1_b,
           w_router_g, b_router_g, w_router_e, b_router_e, w_e_gate, w_e_up, w_e_down, ln2_g, ln2_b):
    nb, n_lat, d = x.shape
    n_ctx = ctx.shape[1]
    rows = n_lat // GRID_W
    assert rows == GDN_CHUNK and n_ctx % GDN_CHUNK == 0 and w_mod.shape[0] == 1
    layer = 0
    lp = {"conv_a_w": conv_a_w[layer], "conv_a_b": conv_a_b[layer], "lru_wa": lru_wa[layer],
          "lru_ba": lru_ba[layer], "lru_wx": lru_wx[layer], "lru_bx": lru_bx[layer],
          "lru_lambda": lru_lambda[layer]}

    pad_rows = (-(nb + 1)) % 8
    cc = jnp.concatenate([c, c_ctx[None, :], jnp.zeros((pad_rows, d), F32)], axis=0)
    mod = _mod_vectors(cc, w_mod[layer], b_mod[layer])
    sh1, sc1, g1, sh2, sc2, g2 = [mod[:nb, j * d:(j + 1) * d] for j in range(6)]
    csh1 = jnp.broadcast_to(mod[nb:nb + 1, 0:d], (nb, d))
    csc1 = jnp.broadcast_to(mod[nb:nb + 1, d:2 * d], (nb, d))

    w_l = w_in[layer]
    b_l = b_in[layer]
    w_r = jnp.concatenate([w_l[:, OFF_XA:OFF_GDN], w_l[:, OFF_MG:]], axis=1).astype(BF16)
    b_r = jnp.concatenate([b_l[OFF_XA:OFF_GDN], b_l[OFF_MG:]])[None, :]
    gate_pad = LANES - N_GATES
    w_g = jnp.concatenate([w_l[:, OFF_GDN:OFF_GDN + QKV_COLS],
                           jnp.pad(w_l[:, OFF_GDN + QKV_COLS:OFF_Z], ((0, 0), (0, gate_pad))),
                           w_l[:, OFF_Z:OFF_MG]], axis=1).astype(BF16)
    b_g = jnp.concatenate([b_l[OFF_GDN:OFF_GDN + QKV_COLS],
                           jnp.pad(b_l[OFF_GDN + QKV_COLS:OFF_Z], (0, gate_pad)),
                           b_l[OFF_Z:OFF_MG]])[None, :]
    n_qg = QKV_COLS + LANES

    ctx_tb = ctx.transpose(1, 0, 2).reshape(n_ctx * nb, d)
    (xa_ctx,) = _inproj_raster(ctx_tb, csc1, csh1, w_r[:, :LRU_WIDTH], b_r[:, :LRU_WIDTH],
                               [(0, LRU_WIDTH)], tm=512)
    zero_lru = jnp.zeros((nb, LRU_WIDTH), F32)
    (sa_f,) = _lru_direction(xa_ctx, lp, 0, zero_lru, False, "none", tt=32)
    (sa_b,) = _lru_direction(xa_ctx, lp, 1, zero_lru, True, "none", tt=32)

    nc_ctx = n_ctx // GDN_CHUNK
    qkv_c, lg_c = _inproj_column(ctx, lambda ci, g: (g, ci, 0), nc_ctx, csc1, csh1,
                                 w_g[:, :n_qg], b_g[:, :n_qg], [(0, QKV_COLS), (QKV_COLS, n_qg)])
    a_row = jnp.pad(-jnp.exp(gdn_a_log[layer].reshape(-1)), (0, LANES - 2 * GDN_HEADS))[None, :]
    dt_row = jnp.pad(gdn_dt_bias[layer].reshape(-1), (0, LANES - 2 * GDN_HEADS))[None, :]
    cw_qkv = conv_qkv_w[layer]
    qkv_c, gates_c = _gdn_prep(qkv_c, lg_c, cw_qkv, a_row, dt_row)
    zero_gdn = jnp.zeros((nb, GDN_HEADS, GDN_DK, GDN_DV), F32)
    (sb_f,) = _gdn_direction(qkv_c, gates_c, zero_gdn, False, False)
    (sb_b,) = _gdn_direction(qkv_c, gates_c, zero_gdn, True, False)

    x_tb = x.transpose(1, 0, 2).reshape(n_lat * nb, d)
    xa, ga, mg = _inproj_raster(x_tb, sc1, sh1, w_r, b_r,
                                [(0, LRU_WIDTH), (LRU_WIDTH, 2 * LRU_WIDTH), (2 * LRU_WIDTH, 2 * LRU_WIDTH + 2 * d)],
                                tm=512)
    h_f, _ = _lru_direction(xa, lp, 0, sa_f, False, "h", tt=32)
    ya, _ = _lru_direction(xa, lp, 1, sa_b, True, "ya", tt=32, hf=h_f, ga=ga)

    xv = x.reshape(nb, rows, GRID_W * d)
    qkv_l, lg_l, z_l = _inproj_column(xv, lambda ci, g: (g, 0, ci), GRID_W, sc1, sh1, w_g, b_g,
                                      [(0, QKV_COLS), (QKV_COLS, n_qg), (n_qg, n_qg + GDN_VW)])
    qkv_l, gates_l = _gdn_prep(qkv_l, lg_l, cw_qkv, a_row, dt_row)
    o_f, _ = _gdn_direction(qkv_l, gates_l, sb_f, False, True)
    o_b, _ = _gdn_direction(qkv_l, gates_l, sb_b, True, True)
    yb = _gdn_out(o_f, o_b, z_l, gdn_norm_w[layer])

    w_rt = jnp.pad(jnp.concatenate([w_router_g[layer], w_router_e[layer]], axis=1),
                   ((0, 0), (0, LANES - N_GROUPS - N_EXPERTS)))
    b_rt = jnp.pad(jnp.concatenate([b_router_g[layer], b_router_e[layer]]),
                   (0, LANES - N_GROUPS - N_EXPERTS))[None, :]
    x1, h2, route = _mixer(ya, yb, mg, x_tb, g1, sh2, sc2, w_pa[layer].astype(BF16), w_pb[layer].astype(BF16),
                           w_out[layer].astype(BF16), ln1_g[layer][None, :], ln1_b[layer][None, :], w_rt, b_rt,
                           tm=256)

    e1 = route[:, 0].astype(jnp.int32)
    e2 = route[:, 1].astype(jnp.int32)
    src, dest, block_expert, n_used = _route_plan(e1, e2, MOE_TM)
    x_pad = jnp.take(h2, src, axis=0)
    y_pad = _experts(x_pad, block_expert, n_used, w_e_gate[layer].astype(BF16), w_e_up[layer].astype(BF16),
                     w_e_down[layer].astype(BF16), MOE_TM)
    dest_bt = dest.reshape(n_lat, nb, TOP_K).transpose(1, 0, 2)
    y1 = jnp.take(y_pad, dest_bt[..., 0], axis=0)
    y2 = jnp.take(y_pad, dest_bt[..., 1], axis=0)
    wts = jnp.pad(route[:, 2:4], ((0, 0), (0, LANES - 2))).reshape(n_lat, nb, LANES).transpose(1, 0, 2)
    return _final(x1, y1, y2, wts, g2, ln2_g[layer][None, :], ln2_b[layer][None, :], tq=512)
```

```python
import functools
import math

import jax
import jax.numpy as jnp
from jax import lax
from jax.experimental import pallas as pl
from jax.experimental.pallas import tpu as pltpu

F32 = jnp.float32
BF16 = jnp.bfloat16

D_MODEL = 1024
GRID_W = 64
LRU_WIDTH = 1024
LRU_BLOCKS = 8
LRU_BLOCK = LRU_WIDTH // LRU_BLOCKS
LRU_C = 8.0
CONV_W = 4
GDN_HEADS = 8
GDN_DK = 128
GDN_DV = 128
GDN_QK = GDN_HEADS * GDN_DK
GDN_VW = GDN_HEADS * GDN_DV
GDN_CHUNK = 64
QKV_COLS = 2 * GDN_QK + GDN_VW
N_GATES = 4 * GDN_HEADS
OFF_XA = 0
OFF_GA = OFF_XA + LRU_WIDTH
OFF_GDN = OFF_GA + LRU_WIDTH
OFF_Z = OFF_GDN + QKV_COLS + N_GATES
OFF_MG = OFF_Z + GDN_VW
N_GROUPS = 4
EXP_PER_GROUP = 8
N_EXPERTS = N_GROUPS * EXP_PER_GROUP
TOP_K = 2
LN_EPS = 1e-6
L2_EPS = 1e-6
DEPTH = 1
DN_ALPHA = (2.0 * DEPTH) ** 0.25

LANES = 128
VMEM_LIMIT = 56 * 1024 * 1024

HI = lax.Precision.HIGHEST


def _cparams(*sem):
    return pltpu.CompilerParams(dimension_semantics=sem, vmem_limit_bytes=VMEM_LIMIT)


def _bdot(a, b):
    return jnp.dot(a.astype(BF16), b.astype(BF16), preferred_element_type=F32)


def _bdot_nt(a, b):
    return lax.dot_general(a.astype(BF16), b.astype(BF16), (((1,), (1,)), ((), ())),
                           preferred_element_type=F32)


def _ln(x):
    mu = jnp.mean(x, axis=-1, keepdims=True)
    xc = x - mu
    var = jnp.mean(xc * xc, axis=-1, keepdims=True)
    return xc * lax.rsqrt(var + LN_EPS)


def _sigmoid(x):
    return 1.0 / (1.0 + jnp.exp(-x))


def _silu(x):
    return x * _sigmoid(x)


def _softplus(x):
    return jnp.maximum(x, 0.0) + jnp.log(1.0 + jnp.exp(-jnp.abs(x)))


def _gelu_tanh(x):
    return 0.5 * x * (1.0 + jnp.tanh(math.sqrt(2.0 / math.pi) * (x + 0.044715 * (x * x * x))))


def _mod_body(c_ref, w_ref, b_ref, o_ref):
    o_ref[...] = jnp.dot(_silu(c_ref[...]), w_ref[...], preferred_element_type=F32,
                         precision=HI) + b_ref[...]


def _mod_vectors(cc, w_mod, b_mod):
    rows, d = cc.shape
    n = w_mod.shape[1]
    tn = 1536
    return pl.pallas_call(
        _mod_body,
        grid=(n // tn,),
        in_specs=[pl.BlockSpec((rows, d), lambda j: (0, 0)),
                  pl.BlockSpec((d, tn), lambda j: (0, j)),
                  pl.BlockSpec((1, tn), lambda j: (0, j))],
        out_specs=pl.BlockSpec((rows, tn), lambda j: (0, j)),
        out_shape=jax.ShapeDtypeStruct((rows, n), F32),
        compiler_params=_cparams("arbitrary"),
        name="mod_vectors",
    )(cc, w_mod, b_mod.reshape(1, n))


def _inproj_r_body(x_ref, sc_ref, sh_ref, w_ref, b_ref, *o_refs, splits):
    nb, tt, d = x_ref.shape
    tm = nb * tt
    xn = _ln(jnp.swapaxes(x_ref[...], 0, 1).reshape(tm, d)).reshape(tt, nb, d)
    xm = (xn * (1.0 + sc_ref[...])[None] + sh_ref[...][None]).reshape(tm, d).astype(BF16)
    for o_ref, (lo, hi) in zip(o_refs, splits):
        for n0 in range(lo, hi, 512):
            o_ref[:, n0 - lo:n0 - lo + 512] = (
                jnp.dot(xm, w_ref[:, n0:n0 + 512], preferred_element_type=F32) + b_ref[:, n0:n0 + 512])


def _inproj_raster(x, sc, sh, w, b, splits, tt):
    nb, length, d = x.shape
    n = w.shape[1]
    tm = tt * nb
    body = functools.partial(_inproj_r_body, splits=splits)
    return pl.pallas_call(
        body,
        grid=(length // tt,),
        in_specs=[pl.BlockSpec((nb, tt, d), lambda i: (0, i, 0)),
                  pl.BlockSpec((nb, d), lambda i: (0, 0)),
                  pl.BlockSpec((nb, d), lambda i: (0, 0)),
                  pl.BlockSpec((d, n), lambda i: (0, 0)),
                  pl.BlockSpec((1, n), lambda i: (0, 0))],
        out_specs=[pl.BlockSpec((tm, hi - lo), lambda i: (i, 0)) for lo, hi in splits],
        out_shape=[jax.ShapeDtypeStruct((length * nb, hi - lo), F32) for lo, hi in splits],
        compiler_params=_cparams("parallel"),
        name="inproj_raster",
    )(x, sc, sh, w, b)


def _inproj_g_body(x_ref, sc_ref, sh_ref, w_ref, b_ref, *o_refs, splits):
    gb, ch, d = x_ref.shape
    parts = []
    for i in range(gb):
        xn = _ln(x_ref[i])
        parts.append((xn * (1.0 + sc_ref[i:i + 1, :]) + sh_ref[i:i + 1, :]).astype(BF16))
    xm = jnp.concatenate(parts, axis=0)
    for o_ref, (lo, hi) in zip(o_refs, splits):
        step = 512 if (hi - lo) % 512 == 0 else hi - lo
        for n0 in range(lo, hi, step):
            res = jnp.dot(xm, w_ref[:, n0:n0 + step], preferred_element_type=F32) + b_ref[:, n0:n0 + step]
            for i in range(gb):
                o_ref[0, i, :, n0 - lo:n0 - lo + step] = res[i * ch:(i + 1) * ch]


def _inproj_column(xv, x_index_map, n_chunks, sc, sh, w, b, splits):
    nb, d = sc.shape
    gb = 8
    n = w.shape[1]
    body = functools.partial(_inproj_g_body, splits=splits)
    return pl.pallas_call(
        body,
        grid=(n_chunks, nb // gb),
        in_specs=[pl.BlockSpec((gb, GDN_CHUNK, d), x_index_map),
                  pl.BlockSpec((gb, d), lambda c, g: (g, 0)),
                  pl.BlockSpec((gb, d), lambda c, g: (g, 0)),
                  pl.BlockSpec((d, n), lambda c, g: (0, 0)),
                  pl.BlockSpec((1, n), lambda c, g: (0, 0))],
        out_specs=[pl.BlockSpec((1, gb, GDN_CHUNK, hi - lo), lambda c, g: (c, g, 0, 0)) for lo, hi in splits],
        out_shape=[jax.ShapeDtypeStruct((n_chunks, nb, GDN_CHUNK, hi - lo), F32) for lo, hi in splits],
        compiler_params=_cparams("parallel", "parallel"),
        name="inproj_column",
    )(xv, sc, sh, w, b)


def _inproj_gl_body(x_ref, sc_ref, sh_ref, w_ref, b_ref, *o_refs, splits):
    _, tm, d = x_ref.shape
    rr = tm // GRID_W
    xs = jnp.swapaxes(x_ref[0].reshape(rr, GRID_W, d), 0, 1).reshape(tm, d)
    xm = (_ln(xs) * (1.0 + sc_ref[0]) + sh_ref[0]).astype(BF16)
    for o_ref, (lo, hi) in zip(o_refs, splits):
        step = 512 if (hi - lo) % 512 == 0 else hi - lo
        for n0 in range(lo, hi, step):
            res = jnp.dot(xm, w_ref[:, n0:n0 + step], preferred_element_type=F32) + b_ref[:, n0:n0 + step]
            o_ref[:, 0, :, n0 - lo:n0 - lo + step] = res.reshape(GRID_W, rr, step)


def _inproj_column_lat(x, sc, sh, w, b, splits):
    nb, length, d = x.shape
    n = w.shape[1]
    rows = length // GRID_W
    rr = 8
    tm = rr * GRID_W
    body = functools.partial(_inproj_gl_body, splits=splits)
    vec = pl.BlockSpec((1, 1, d), lambda bi, ri: (bi, 0, 0))
    return pl.pallas_call(
        body,
        grid=(nb, rows // rr),
        in_specs=[pl.BlockSpec((1, tm, d), lambda bi, ri: (bi, ri, 0)),
                  vec, vec,
                  pl.BlockSpec((d, n), lambda bi, ri: (0, 0)),
                  pl.BlockSpec((1, n), lambda bi, ri: (0, 0))],
        out_specs=[pl.BlockSpec((GRID_W, 1, rr, hi - lo), lambda bi, ri: (0, bi, ri, 0)) for lo, hi in splits],
        out_shape=[jax.ShapeDtypeStruct((GRID_W, nb, rows, hi - lo), F32) for lo, hi in splits],
        compiler_params=_cparams("parallel", "parallel"),
        name="inproj_column_lat",
    )(x, sc.reshape(nb, 1, d), sh.reshape(nb, 1, d), w, b)


def _lru_body(*refs, tt, nb, reverse, emit):
    if emit == "ya":
        (xa_ref, prev_ref, next_ref, cw_ref, cb_ref, wa_ref, ba_ref, wx_ref, bx_ref, lam_ref, h0_ref,
         hf_ref, ga_ref, out_ref, hl_ref, a_s, b_s, h_s) = refs
    elif emit == "h":
        (xa_ref, prev_ref, next_ref, cw_ref, cb_ref, wa_ref, ba_ref, wx_ref, bx_ref, lam_ref, h0_ref,
         out_ref, hl_ref, a_s, b_s, h_s) = refs
    else:
        (xa_ref, prev_ref, next_ref, cw_ref, cb_ref, wa_ref, ba_ref, wx_ref, bx_ref, lam_ref, h0_ref,
         hl_ref, a_s, b_s, h_s) = refs
    step = pl.program_id(0)
    nsteps = pl.num_programs(0)
    blk = (nsteps - 1 - step) if reverse else step
    rows = tt * nb

    @pl.when(step == 0)
    def _():
        h_s[...] = h0_ref[...]

    prev = jnp.where(blk == 0, 0.0, prev_ref[...])
    nxt = jnp.where(blk == nsteps - 1, 0.0, next_ref[...])
    ext = jnp.concatenate([prev, xa_ref[...], nxt], axis=0)
    u = (cw_ref[0:1, :] * ext[0:rows] + cw_ref[1:2, :] * ext[nb:rows + nb]
         + cw_ref[2:3, :] * ext[2 * nb:rows + 2 * nb] + cw_ref[3:4, :] * ext[3 * nb:rows + 3 * nb]
         + cb_ref[...])
    for n in range(LRU_BLOCKS):
        sl = slice(n * LRU_BLOCK, (n + 1) * LRU_BLOCK)
        un = u[:, sl]
        ub = un.astype(BF16)
        r = _sigmoid(jnp.dot(ub, wa_ref[n], preferred_element_type=F32) + ba_ref[:, sl])
        ig = _sigmoid(jnp.dot(ub, wx_ref[n], preferred_element_type=F32) + bx_ref[:, sl])
        log_a = (-LRU_C) * r * _softplus(-lam_ref[:, sl])
        a = jnp.exp(log_a)
        a_s[:, sl] = a
        b_s[:, sl] = jnp.sqrt(1.0 - a * a) * (ig * un)

    def scan_step(j, h):
        t = (tt - 1 - j) if reverse else j
        r0 = pl.multiple_of(t * nb, nb)
        h = a_s[pl.ds(r0, nb), :] * h + b_s[pl.ds(r0, nb), :]
        if emit == "h":
            out_ref[pl.ds(r0, nb), :] = h
        elif emit == "ya":
            b_s[pl.ds(r0, nb), :] = h
        return h

    h = lax.fori_loop(0, tt, scan_step, h_s[...], unroll=4)
    h_s[...] = h
    hl_ref[...] = h
    if emit == "ya":
        out_ref[...] = (_gelu_tanh(ga_ref[...]) * (hf_ref[...] + b_s[...])).astype(BF16)


def _lru_direction(xa, lp, di, h0, reverse, emit, tt, hf=None, ga=None):
    t, w = xa.shape
    nb = h0.shape[0]
    rows = tt * nb
    nblk = t // rows
    assert rows % (2 * nb) == 0
    prev_per = rows // nb
    next_per = rows // (2 * nb)
    n_prev = t // nb
    n_next = t // (2 * nb)

    def bi(i):
        return (nblk - 1 - i) if reverse else i

    const2 = lambda i: (0, 0)
    in_specs = [
        pl.BlockSpec((rows, w), lambda i: (bi(i), 0)),
        pl.BlockSpec((nb, w), lambda i: (jnp.maximum(bi(i) * prev_per - 1, 0), 0)),
        pl.BlockSpec((2 * nb, w), lambda i: (jnp.minimum((bi(i) + 1) * next_per, n_next - 1), 0)),
        pl.BlockSpec((CONV_W, w), const2),
        pl.BlockSpec((1, w), const2),
        pl.BlockSpec((LRU_BLOCKS, LRU_BLOCK, LRU_BLOCK), lambda i: (0, 0, 0)),
        pl.BlockSpec((1, w), const2),
        pl.BlockSpec((LRU_BLOCKS, LRU_BLOCK, LRU_BLOCK), lambda i: (0, 0, 0)),
        pl.BlockSpec((1, w), const2),
        pl.BlockSpec((1, w), const2),
        pl.BlockSpec((nb, w), const2),
    ]
    args = [xa, xa, xa, lp["conv_a_w"], lp["conv_a_b"].reshape(1, w),
            lp["lru_wa"][di].astype(BF16), lp["lru_ba"][di].reshape(1, w),
            lp["lru_wx"][di].astype(BF16), lp["lru_bx"][di].reshape(1, w),
            lp["lru_lambda"][di].reshape(1, w), h0]
    out_specs = []
    out_shape = []
    if emit == "ya":
        in_specs += [pl.BlockSpec((rows, w), lambda i: (bi(i), 0)),
                     pl.BlockSpec((rows, w), lambda i: (bi(i), 0))]
        args += [hf, ga]
        out_specs.append(pl.BlockSpec((rows, w), lambda i: (bi(i), 0)))
        out_shape.append(jax.ShapeDtypeStruct((t, w), BF16))
    elif emit == "h":
        out_specs.append(pl.BlockSpec((rows, w), lambda i: (bi(i), 0)))
        out_shape.append(jax.ShapeDtypeStruct((t, w), F32))
    out_specs.append(pl.BlockSpec((nb, w), const2))
    out_shape.append(jax.ShapeDtypeStruct((nb, w), F32))
    body = functools.partial(_lru_body, tt=tt, nb=nb, reverse=reverse, emit=emit)
    return pl.pallas_call(
        body,
        grid=(nblk,),
        in_specs=in_specs,
        out_specs=out_specs,
        out_shape=out_shape,
        scratch_shapes=[pltpu.VMEM((rows, w), F32), pltpu.VMEM((rows, w), F32), pltpu.VMEM((nb, w), F32)],
        compiler_params=_cparams("arbitrary"),
        name="lru_" + ("bwd" if reverse else "fwd") + "_" + emit,
    )(*args)


def _gdn_prep_body(qkv_ref, prev_ref, next_ref, lg_ref, cw_ref, ga_ref, gd_ref, qkv_o, g_o):
    c = pl.program_id(0)
    nc = pl.num_programs(0)
    ch = GDN_CHUNK
    gb = qkv_ref.shape[1]
    for i in range(gb):
        prev = jnp.where(c == 0, 0.0, prev_ref[0, i])
        nxt = jnp.where(c == nc - 1, 0.0, next_ref[0, i])
        ext = jnp.concatenate([prev, qkv_ref[0, i], nxt], axis=0)
        u = (cw_ref[0:1, :] * ext[7:7 + ch] + cw_ref[1:2, :] * ext[8:8 + ch]
             + cw_ref[2:3, :] * ext[9:9 + ch] + cw_ref[3:4, :] * ext[10:10 + ch])
        act = _silu(u)
        for h in range(GDN_HEADS):
            sq = slice(h * GDN_DK, (h + 1) * GDN_DK)
            qh = act[:, sq]
            qn = qh * lax.rsqrt(jnp.sum(qh * qh, axis=-1, keepdims=True) + L2_EPS) * (GDN_DK ** -0.5)
            qkv_o[0, i, :, sq] = qn.astype(qkv_o.dtype)
            sk = slice(GDN_QK + h * GDN_DK, GDN_QK + (h + 1) * GDN_DK)
            kh = act[:, sk]
            kn = kh * lax.rsqrt(jnp.sum(kh * kh, axis=-1, keepdims=True) + L2_EPS)
            qkv_o[0, i, :, sk] = kn.astype(qkv_o.dtype)
        qkv_o[0, i, :, 2 * GDN_QK:] = act[:, 2 * GDN_QK:].astype(qkv_o.dtype)
        lg = lg_ref[0, i]
        lane = lax.broadcasted_iota(jnp.int32, lg.shape, 1)
        decay = ga_ref[...] * _softplus(lg + gd_ref[...])
        g_o[0, i] = jnp.where(lane < 2 * GDN_HEADS, decay, jnp.where(lane < N_GATES, _sigmoid(lg), 0.0))


def _gdn_prep(qkv, lg, conv_w, a_row, dt_row, gb):
    nc, nb, ch, n = qkv.shape
    hb = ch // 8
    return pl.pallas_call(
        _gdn_prep_body,
        grid=(nc, nb // gb),
        in_specs=[pl.BlockSpec((1, gb, ch, n), lambda c, b: (c, b, 0, 0)),
                  pl.BlockSpec((1, gb, 8, n), lambda c, b: (jnp.maximum(c - 1, 0), b, hb - 1, 0)),
                  pl.BlockSpec((1, gb, 8, n), lambda c, b: (jnp.minimum(c + 1, nc - 1), b, 0, 0)),
                  pl.BlockSpec((1, gb, ch, LANES), lambda c, b: (c, b, 0, 0)),
                  pl.BlockSpec((CONV_W, n), lambda c, b: (0, 0)),
                  pl.BlockSpec((1, LANES), lambda c, b: (0, 0)),
                  pl.BlockSpec((1, LANES), lambda c, b: (0, 0))],
        out_specs=[pl.BlockSpec((1, gb, ch, n), lambda c, b: (c, b, 0, 0)),
                   pl.BlockSpec((1, gb, ch, LANES), lambda c, b: (c, b, 0, 0))],
        out_shape=[jax.ShapeDtypeStruct(qkv.shape, BF16), jax.ShapeDtypeStruct(lg.shape, F32)],
        compiler_params=_cparams("parallel", "parallel"),
        name="gdn_prep",
    )(qkv, qkv, qkv, lg, conv_w, a_row, dt_row)


def _gdn_scan_body(qkv_ref, g_ref, s0_ref, *rest, reverse, emit_o):
    if emit_o:
        o_ref, sl_ref, s_s = rest
    else:
        sl_ref, s_s = rest
    step = pl.program_id(1)
    ch = GDN_CHUNK
    gb = qkv_ref.shape[1]
    n_sq = int(math.log2(ch)) - 1

    @pl.when(step == 0)
    def _():
        s_s[...] = s0_ref[...]

    row = lax.broadcasted_iota(jnp.int32, (ch, ch), 0)
    col = lax.broadcasted_iota(jnp.int32, (ch, ch), 1)
    incl = (row <= col) if reverse else (row >= col)
    strict = (row < col) if reverse else (row > col)
    eye = (row == col).astype(F32)
    tri = incl.astype(F32)
    goff = GDN_HEADS if reverse else 0
    boff = 2 * GDN_HEADS + goff

    units = [(i, h) for i in range(gb) for h in range(GDN_HEADS)]
    nu = len(units)
    gcum, gcum_t, gtot, gates = [], [], [], []
    for i in range(gb):
        g = g_ref[0, i]
        gates.append(g)
        cum = jnp.dot(tri, g, preferred_element_type=F32, precision=HI)
        gcum.append(cum)
        gcum_t.append(cum.T)
        gtot.append(jnp.sum(g, axis=0, keepdims=True))

    q, k, kb, xin, decay, eg, egl, gl = [], [], [], [], [], [], [], []
    for i, h in units:
        qh = qkv_ref[0, i, :, h * GDN_DK:(h + 1) * GDN_DK].astype(F32)
        kh = qkv_ref[0, i, :, GDN_QK + h * GDN_DK:GDN_QK + (h + 1) * GDN_DK].astype(F32)
        vh = qkv_ref[0, i, :, 2 * GDN_QK + h * GDN_DV:2 * GDN_QK + (h + 1) * GDN_DV].astype(F32)
        gc = gcum[i][:, goff + h:goff + h + 1]
        gr = gcum_t[i][goff + h:goff + h + 1, :]
        gt = gtot[i][:, goff + h:goff + h + 1]
        beta = gates[i][:, boff + h:boff + h + 1]
        decay.append(jnp.where(incl, jnp.exp(jnp.where(incl, gc - gr, 0.0)), 0.0))
        e = jnp.exp(gc)
        kbh = kh * beta
        q.append(qh)
        k.append(kh)
        kb.append(kbh)
        eg.append(e)
        egl.append(jnp.exp(gt - gc))
        gl.append(jnp.exp(gt))
        xin.append(jnp.concatenate([kbh * e, vh * beta], axis=1).astype(BF16))

    gram = [_bdot_nt(jnp.concatenate([kb[j], q[j]], axis=0), k[j]) for j in range(nu)]
    nil = [-jnp.where(strict, gram[j][:ch] * decay[j], 0.0) for j in range(nu)]
    amat = [gram[j][ch:] * decay[j] for j in range(nu)]
    inv = [eye + m for m in nil]
    power = nil
    for _ in range(n_sq):
        pb = [p.astype(BF16) for p in power]
        power = [jnp.dot(p, p, preferred_element_type=F32) for p in pb]
        inv = [iv + _bdot(iv, p) for iv, p in zip(inv, power)]
    wu = [_bdot(iv, x) for iv, x in zip(inv, xin)]
    s_old = [s_s[i, h] for i, h in units]
    s_bf = [s.astype(BF16) for s in s_old]
    ws = [_bdot(jnp.concatenate([wu[j][:, :GDN_DK], q[j] * eg[j]], axis=0), s_bf[j]) for j in range(nu)]
    v_new = [(wu[j][:, GDN_DK:] - ws[j][:ch]).astype(BF16) for j in range(nu)]
    upd = [_bdot(jnp.concatenate([amat[j], (k[j] * egl[j]).T], axis=0), v_new[j]) for j in range(nu)]
    for j, (i, h) in enumerate(units):
        if emit_o:
            o_ref[0, i, :, h * GDN_DV:(h + 1) * GDN_DV] = ws[j][ch:] + upd[j][:ch]
        s_new = s_old[j] * gl[j] + upd[j][ch:]
        s_s[i, h] = s_new
        sl_ref[i, h] = s_new


def _gdn_direction(qkv, gates, s0, reverse, emit_o, gb=1):
    nc, nb, ch, n = qkv.shape

    def ci(c):
        return (nc - 1 - c) if reverse else c

    out_specs = []
    out_shape = []
    if emit_o:
        out_specs.append(pl.BlockSpec((1, gb, ch, GDN_VW), lambda b, c: (ci(c), b, 0, 0)))
        out_shape.append(jax.ShapeDtypeStruct((nc, nb, ch, GDN_VW), F32))
    out_specs.append(pl.BlockSpec((gb, GDN_HEADS, GDN_DK, GDN_DV), lambda b, c: (b, 0, 0, 0)))
    out_shape.append(jax.ShapeDtypeStruct((nb, GDN_HEADS, GDN_DK, GDN_DV), F32))
    body = functools.partial(_gdn_scan_body, reverse=reverse, emit_o=emit_o)
    return pl.pallas_call(
        body,
        grid=(nb // gb, nc),
        in_specs=[pl.BlockSpec((1, gb, ch, n), lambda b, c: (ci(c), b, 0, 0)),
                  pl.BlockSpec((1, gb, ch, LANES), lambda b, c: (ci(c), b, 0, 0)),
                  pl.BlockSpec((gb, GDN_HEADS, GDN_DK, GDN_DV), lambda b, c: (b, 0, 0, 0))],
        out_specs=out_specs,
        out_shape=out_shape,
        scratch_shapes=[pltpu.VMEM((gb, GDN_HEADS, GDN_DK, GDN_DV), F32)],
        compiler_params=_cparams("parallel", "arbitrary"),
        name="gdn_scan_" + ("bwd" if reverse else "fwd"),
    )(qkv, gates, s0)


def _gdn_out_body(of_ref, ob_ref, z_ref, nw_ref, y_ref, y_s):
    nb = of_ref.shape[1]
    for i in range(nb):
        o = of_ref[0, i] + ob_ref[0, i]
        zg = _silu(z_ref[0, i])
        for h in range(GDN_HEADS):
            sl = slice(h * GDN_DV, (h + 1) * GDN_DV)
            oh = o[:, sl]
            nh = oh * lax.rsqrt(jnp.mean(oh * oh, axis=-1, keepdims=True) + LN_EPS) * nw_ref[...]
            y_s[i, :, sl] = nh * zg[:, sl]
    y_ref[:, 0] = jnp.swapaxes(y_s[...], 0, 1).astype(BF16)


def _gdn_out(o_f, o_b, z, norm_w):
    nc, nb, ch, vw = o_f.shape
    spec = pl.BlockSpec((1, nb, ch, vw), lambda c: (c, 0, 0, 0))
    y = pl.pallas_call(
        _gdn_out_body,
        grid=(nc,),
        in_specs=[spec, spec, spec, pl.BlockSpec((1, GDN_DV), lambda c: (0, 0))],
        out_specs=pl.BlockSpec((ch, 1, nb, vw), lambda c: (0, c, 0, 0)),
        out_shape=jax.ShapeDtypeStruct((ch, nc, nb, vw), BF16),
        scratch_shapes=[pltpu.VMEM((nb, ch, vw), F32)],
        compiler_params=_cparams("parallel"),
        name="gdn_out",
    )(o_f, o_b, z, norm_w.reshape(1, GDN_DV))
    return y.reshape(ch * nc * nb, vw)


def _mixer_body(ya_ref, yb_ref, mg_ref, x_ref, g1_ref, sh2_ref, sc2_ref, wpa_ref, wpb_ref, wo_ref,
                lg_ref, lb_ref, wr_ref, br_ref, x1_ref, h2_ref, rt_ref):
    nb, tt, d = x_ref.shape
    tm = nb * tt
    xs = jnp.swapaxes(x_ref[...], 0, 1).reshape(tm, d)
    pa = jnp.dot(ya_ref[...], wpa_ref[...], preferred_element_type=F32)
    pb = jnp.dot(yb_ref[...], wpb_ref[...], preferred_element_type=F32)
    merged = _sigmoid(mg_ref[:, :d]) * pa + _sigmoid(mg_ref[:, d:]) * pb
    mix = _bdot(merged, wo_ref[...])
    mix3 = mix.reshape(tm // nb, nb, d) * g1_ref[...][None]
    x1 = _ln(DN_ALPHA * xs + mix3.reshape(tm, d)) * lg_ref[...] + lb_ref[...]
    x1_ref[...] = x1
    h3 = _ln(x1).reshape(tm // nb, nb, d) * (1.0 + sc2_ref[...])[None] + sh2_ref[...][None]
    h2 = h3.reshape(tm, d)
    h2_ref[...] = h2.astype(BF16)

    logits = jnp.dot(h2, wr_ref[...], preferred_element_type=F32, precision=HI) + br_ref[...]
    lane = lax.broadcasted_iota(jnp.int32, logits.shape, 1).astype(F32)
    neg = jnp.float32(-jnp.inf)
    big = jnp.float32(1 << 20)
    is_g = lane < N_GROUPS
    gl = jnp.where(is_g, logits, neg)
    gmax = jnp.max(gl, axis=-1, keepdims=True)
    gsum = jnp.sum(jnp.where(is_g, jnp.exp(gl - gmax), 0.0), axis=-1, keepdims=True)
    p_group = 1.0 / gsum
    g_idx = jnp.min(jnp.where(gl == gmax, lane, big), axis=-1, keepdims=True)
    lo = N_GROUPS + g_idx * EXP_PER_GROUP
    in_g = (lane >= lo) & (lane < lo + EXP_PER_GROUP)
    el = jnp.where(in_g, logits, neg)
    m1 = jnp.max(el, axis=-1, keepdims=True)
    i1 = jnp.min(jnp.where(el == m1, lane, big), axis=-1, keepdims=True)
    el2 = jnp.where(lane == i1, neg, el)
    m2 = jnp.max(el2, axis=-1, keepdims=True)
    i2 = jnp.min(jnp.where(el2 == m2, lane, big), axis=-1, keepdims=True)
    esum = jnp.sum(jnp.where(in_g, jnp.exp(el - m1), 0.0), axis=-1, keepdims=True)
    p1 = 1.0 / esum
    p2 = jnp.exp(m2 - m1) / esum
    w1 = p_group * p1 / (p1 + p2)
    w2 = p_group * p2 / (p1 + p2)
    e1 = i1 - N_GROUPS
    e2 = i2 - N_GROUPS
    rt_ref[...] = jnp.where(lane == 0, e1, jnp.where(lane == 1, e2, jnp.where(lane == 2, w1,
                            jnp.where(lane == 3, w2, 0.0))))


def _mixer(ya, yb, mg, x, g1, sh2, sc2, w_pa, w_pb, w_out, ln_g, ln_b, w_r, b_r, tt):
    nb, length, d = x.shape
    t = nb * length
    tm = tt * nb
    row = lambda n: pl.BlockSpec((tm, n), lambda i: (i, 0))
    full = lambda a: pl.BlockSpec(a.shape, lambda i: (0,) * a.ndim)
    args = [ya, yb, mg, x, g1, sh2, sc2, w_pa, w_pb, w_out, ln_g, ln_b, w_r, b_r]
    in_specs = ([row(d), row(d), row(2 * d), pl.BlockSpec((nb, tt, d), lambda i: (0, i, 0))]
                + [full(a) for a in args[4:]])
    body = _mixer_body
    return pl.pallas_call(
        body,
        grid=(t // tm,),
        in_specs=in_specs,
        out_specs=[row(d), row(d), row(LANES)],
        out_shape=[jax.ShapeDtypeStruct((t, d), F32), jax.ShapeDtypeStruct((t, d), BF16),
                   jax.ShapeDtypeStruct((t, LANES), F32)],
        compiler_params=_cparams("parallel"),
        name="mixer_router",
    )(*args)


def _experts_body(be_ref, nu_ref, x_ref, wg_ref, wu_ref, wd_ref, y_ref, wg_s, wu_s, wd_s):
    i = pl.program_id(0)
    used = i < nu_ref[0]
    new_expert = jnp.logical_or(i == 0, be_ref[i] != be_ref[jnp.maximum(i - 1, 0)])

    @pl.when(jnp.logical_and(used, new_expert))
    def _():
        wg_s[...] = wg_ref[0].astype(BF16)
        wu_s[...] = wu_ref[0].astype(BF16)
        wd_s[...] = wd_ref[0].astype(BF16)

    @pl.when(used)
    def _():
        x = x_ref[...]
        hg = jnp.dot(x, wg_s[...], preferred_element_type=F32)
        hu = jnp.dot(x, wu_s[...], preferred_element_type=F32)
        y_ref[...] = _bdot(_silu(hg) * hu, wd_s[...]).astype(y_ref.dtype)

    @pl.when(jnp.logical_not(used))
    def _():
        y_ref[...] = jnp.zeros_like(y_ref)


def _experts(x_pad, block_expert, n_used, w_gate, w_up, w_down, tm):
    rows, d = x_pad.shape
    de = w_gate.shape[2]
    nblk = rows // tm
    wspec = lambda k, n: pl.BlockSpec((1, k, n), lambda i, be, nu: (be[i], 0, 0))
    return pl.pallas_call(
        _experts_body,
        grid_spec=pltpu.PrefetchScalarGridSpec(
            num_scalar_prefetch=2,
            grid=(nblk,),
            in_specs=[pl.BlockSpec((tm, d), lambda i, be, nu: (i, 0)),
                      wspec(d, de), wspec(d, de), wspec(de, d)],
            out_specs=pl.BlockSpec((tm, d), lambda i, be, nu: (i, 0)),
            scratch_shapes=[pltpu.VMEM((d, de), BF16), pltpu.VMEM((d, de), BF16), pltpu.VMEM((de, d), BF16)],
        ),
        out_shape=jax.ShapeDtypeStruct((rows, d), BF16),
        compiler_params=_cparams("arbitrary"),
        name="experts",
    )(block_expert, n_used, x_pad, w_gate, w_up, w_down)


def _final_body(x1_ref, y1_ref, y2_ref, rt_ref, g2_ref, lg_ref, lb_ref, o_ref):
    tm, d = x1_ref.shape
    nb = g2_ref.shape[0]
    y = rt_ref[:, 2:3] * y1_ref[...].astype(F32) + rt_ref[:, 3:4] * y2_ref[...].astype(F32)
    gy = (y.reshape(tm // nb, nb, d) * g2_ref[...][None]).reshape(tm, d)
    out = _ln(DN_ALPHA * x1_ref[...] + gy) * lg_ref[...] + lb_ref[...]
    o_ref[...] = jnp.swapaxes(out.reshape(tm // nb, nb, d), 0, 1)


def _final(x1_tb, y1, y2, route, g2, ln_g, ln_b, tq):
    t, d = x1_tb.shape
    nb = g2.shape[0]
    length = t // nb
    tm = tq * nb
    row = lambda n: pl.BlockSpec((tm, n), lambda i: (i, 0))
    return pl.pallas_call(
        _final_body,
        grid=(length // tq,),
        in_specs=[row(d), row(d), row(d), row(LANES),
                  pl.BlockSpec((nb, d), lambda i: (0, 0)),
                  pl.BlockSpec((1, d), lambda i: (0, 0)),
                  pl.BlockSpec((1, d), lambda i: (0, 0))],
        out_specs=pl.BlockSpec((nb, tq, d), lambda i: (0, i, 0)),
        out_shape=jax.ShapeDtypeStruct((nb, length, d), F32),
        compiler_params=_cparams("parallel"),
        name="moe_combine_postnorm",
    )(x1_tb, y1, y2, route, g2, ln_g, ln_b)


MOE_TM = 512
GDN_GB = 4


def _route_plan(e1, e2, tm):
    t = e1.shape[0]
    e_flat = jnp.stack([e1, e2], axis=1).reshape(-1)
    n_assign = e_flat.shape[0]
    experts = jnp.arange(N_EXPERTS, dtype=jnp.int32)
    onehot = (e_flat[:, None] == experts[None, :]).astype(jnp.int32)
    csum = jnp.cumsum(onehot, axis=0)
    rank = jnp.sum(jnp.where(onehot > 0, csum - 1, 0), axis=1)
    counts = csum[-1]
    starts = jnp.cumsum(counts) - counts
    padded = (counts + tm - 1) // tm * tm
    pends = jnp.cumsum(padded)
    pstarts = pends - padded
    dest = pstarts[e_flat] + rank
    n_blocks = (n_assign + N_EXPERTS * (tm - 1) + tm - 1) // tm
    n_used = (pends[-1] // tm).astype(jnp.int32).reshape(1)
    blk0 = jnp.arange(n_blocks, dtype=jnp.int32) * tm
    block_expert = jnp.minimum(jnp.sum((pends[None, :] <= blk0[:, None]).astype(jnp.int32), axis=1),
                               N_EXPERTS - 1).astype(jnp.int32)
    order = jnp.argsort(e_flat, stable=True).astype(jnp.int32)
    p = jnp.arange(n_blocks * tm, dtype=jnp.int32)
    pe = jnp.repeat(block_expert, tm)
    r = p - pstarts[pe]
    valid = r < counts[pe]
    a_idx = jnp.clip(starts[pe] + r, 0, n_assign - 1)
    src = jnp.where(valid, order.at[a_idx].get(mode="promise_in_bounds") // TOP_K, 0)
    return src, dest.reshape(t, TOP_K), block_expert, n_used


def kernel(x, c, ctx, c_ctx, w_mod, b_mod, w_in, b_in, conv_a_w, conv_a_b, lru_wa, lru_ba, lru_wx, lru_bx,
           lru_lambda, conv_qkv_w, gdn_a_log, gdn_dt_bias, gdn_norm_w, w_pa, w_pb, w_out, ln1_g, ln1_b,
           w_router_g, b_router_g, w_router_e, b_router_e, w_e_gate, w_e_up, w_e_down, ln2_g, ln2_b):
    nb, n_lat, d = x.shape
    n_ctx = ctx.shape[1]
    rows = n_lat // GRID_W
    assert rows == GDN_CHUNK and n_ctx % GDN_CHUNK == 0 and w_mod.shape[0] == 1
    layer = 0
    lp = {"conv_a_w": conv_a_w[layer], "conv_a_b": conv_a_b[layer], "lru_wa": lru_wa[layer],
          "lru_ba": lru_ba[layer], "lru_wx": lru_wx[layer], "lru_bx": lru_bx[layer],
          "lru_lambda": lru_lambda[layer]}

    pad_rows = (-(nb + 1)) % 8
    cc = jnp.concatenate([c, c_ctx[None, :], jnp.zeros((pad_rows, d), F32)], axis=0)
    mod = _mod_vectors(cc, w_mod[layer], b_mod[layer])
    sh1, sc1, g1, sh2, sc2, g2 = [mod[:nb, j * d:(j + 1) * d] for j in range(6)]
    csh1 = jnp.broadcast_to(mod[nb:nb + 1, 0:d], (nb, d))
    csc1 = jnp.broadcast_to(mod[nb:nb + 1, d:2 * d], (nb, d))

    w_l = w_in[layer]
    b_l = b_in[layer]
    w_r = jnp.concatenate([w_l[:, OFF_XA:OFF_GDN], w_l[:, OFF_MG:]], axis=1).astype(BF16)
    b_r = jnp.concatenate([b_l[OFF_XA:OFF_GDN], b_l[OFF_MG:]])[None, :]
    gate_pad = LANES - N_GATES
    w_g = jnp.concatenate([w_l[:, OFF_GDN:OFF_GDN + QKV_COLS],
                           jnp.pad(w_l[:, OFF_GDN + QKV_COLS:OFF_Z], ((0, 0), (0, gate_pad))),
                           w_l[:, OFF_Z:OFF_MG]], axis=1).astype(BF16)
    b_g = jnp.concatenate([b_l[OFF_GDN:OFF_GDN + QKV_COLS],
                           jnp.pad(b_l[OFF_GDN + QKV_COLS:OFF_Z], (0, gate_pad)),
                           b_l[OFF_Z:OFF_MG]])[None, :]
    n_qg = QKV_COLS + LANES

    (xa_ctx,) = _inproj_raster(ctx, csc1, csh1, w_r[:, :LRU_WIDTH], b_r[:, :LRU_WIDTH],
                               [(0, LRU_WIDTH)], tt=32)
    zero_lru = jnp.zeros((nb, LRU_WIDTH), F32)
    (sa_f,) = _lru_direction(xa_ctx, lp, 0, zero_lru, False, "none", tt=32)
    (sa_b,) = _lru_direction(xa_ctx, lp, 1, zero_lru, True, "none", tt=32)

    nc_ctx = n_ctx // GDN_CHUNK
    qkv_c, lg_c = _inproj_column(ctx, lambda ci, g: (g, ci, 0), nc_ctx, csc1, csh1,
                                 w_g[:, :n_qg], b_g[:, :n_qg], [(0, QKV_COLS), (QKV_COLS, n_qg)])
    a_row = jnp.pad(-jnp.exp(gdn_a_log[layer].reshape(-1)), (0, LANES - 2 * GDN_HEADS))[None, :]
    dt_row = jnp.pad(gdn_dt_bias[layer].reshape(-1), (0, LANES - 2 * GDN_HEADS))[None, :]
    cw_qkv = conv_qkv_w[layer]
    qkv_c, gates_c = _gdn_prep(qkv_c, lg_c, cw_qkv, a_row, dt_row, GDN_GB)
    zero_gdn = jnp.zeros((nb, GDN_HEADS, GDN_DK, GDN_DV), F32)
    (sb_f,) = _gdn_direction(qkv_c, gates_c, zero_gdn, False, False, GDN_GB)
    (sb_b,) = _gdn_direction(qkv_c, gates_c, zero_gdn, True, False, GDN_GB)

    xa, ga, mg = _inproj_raster(x, sc1, sh1, w_r, b_r,
                                [(0, LRU_WIDTH), (LRU_WIDTH, 2 * LRU_WIDTH), (2 * LRU_WIDTH, 2 * LRU_WIDTH + 2 * d)],
                                tt=32)
    h_f, _ = _lru_direction(xa, lp, 0, sa_f, False, "h", tt=32)
    ya, _ = _lru_direction(xa, lp, 1, sa_b, True, "ya", tt=32, hf=h_f, ga=ga)

    qkv_l, lg_l, z_l = _inproj_column_lat(x, sc1, sh1, w_g, b_g,
                                          [(0, QKV_COLS), (QKV_COLS, n_qg), (n_qg, n_qg + GDN_VW)])
    qkv_l, gates_l = _gdn_prep(qkv_l, lg_l, cw_qkv, a_row, dt_row, GDN_GB)
    o_f, _ = _gdn_direction(qkv_l, gates_l, sb_f, False, True, GDN_GB)
    o_b, _ = _gdn_direction(qkv_l, gates_l, sb_b, True, True, GDN_GB)
    yb = _gdn_out(o_f, o_b, z_l, gdn_norm_w[layer])

    w_rt = jnp.pad(jnp.concatenate([w_router_g[layer], w_router_e[layer]], axis=1),
                   ((0, 0), (0, LANES - N_GROUPS - N_EXPERTS)))
    b_rt = jnp.pad(jnp.concatenate([b_router_g[layer], b_router_e[layer]]),
                   (0, LANES - N_GROUPS - N_EXPERTS))[None, :]
    x1, h2, route = _mixer(ya, yb, mg, x, g1, sh2, sc2, w_pa[layer].astype(BF16), w_pb[layer].astype(BF16),
                           w_out[layer].astype(BF16), ln1_g[layer][None, :], ln1_b[layer][None, :], w_rt, b_rt,
                           tt=16)

    e1 = route[:, 0].astype(jnp.int32)
    e2 = route[:, 1].astype(jnp.int32)
    src, dest, block_expert, n_used = _route_plan(e1, e2, MOE_TM)
    x_pad = h2.at[src].get(mode="promise_in_bounds")
    y_pad = _experts(x_pad, block_expert, n_used, w_e_gate[layer], w_e_up[layer], w_e_down[layer], MOE_TM)
    y1 = y_pad.at[dest[:, 0]].get(mode="promise_in_bounds")
    y2 = y_pad.at[dest[:, 1]].get(mode="promise_in_bounds")
    return _final(x1, y1, y2, route, g2, ln2_g[layer][None, :], ln2_b[layer][None, :], tq=32)
```

```python
import functools
import math

import jax
import jax.numpy as jnp
from jax import lax
from jax.experimental import pallas as pl
from jax.experimental.pallas import tpu as pltpu

F32 = jnp.float32
BF16 = jnp.bfloat16

D_MODEL = 1024
GRID_W = 64
LRU_WIDTH = 1024
LRU_BLOCKS = 8
LRU_BLOCK = LRU_WIDTH // LRU_BLOCKS
LRU_C = 8.0
CONV_W = 4
GDN_HEADS = 8
GDN_DK = 128
GDN_DV = 128
GDN_QK = GDN_HEADS * GDN_DK
GDN_VW = GDN_HEADS * GDN_DV
GDN_CHUNK = 64
QKV_COLS = 2 * GDN_QK + GDN_VW
N_GATES = 4 * GDN_HEADS
OFF_XA = 0
OFF_GA = OFF_XA + LRU_WIDTH
OFF_GDN = OFF_GA + LRU_WIDTH
OFF_Z = OFF_GDN + QKV_COLS + N_GATES
OFF_MG = OFF_Z + GDN_VW
N_GROUPS = 4
EXP_PER_GROUP = 8
N_EXPERTS = N_GROUPS * EXP_PER_GROUP
TOP_K = 2
LN_EPS = 1e-6
L2_EPS = 1e-6
DEPTH = 1
DN_ALPHA = (2.0 * DEPTH) ** 0.25

LANES = 128
VMEM_LIMIT = 56 * 1024 * 1024

HI = lax.Precision.HIGHEST


def _cparams(*sem):
    return pltpu.CompilerParams(dimension_semantics=sem, vmem_limit_bytes=VMEM_LIMIT)


def _bdot(a, b):
    return jnp.dot(a.astype(BF16), b.astype(BF16), preferred_element_type=F32)


def _bdot_nt(a, b):
    return lax.dot_general(a.astype(BF16), b.astype(BF16), (((1,), (1,)), ((), ())),
                           preferred_element_type=F32)


def _ln(x):
    mu = jnp.mean(x, axis=-1, keepdims=True)
    xc = x - mu
    var = jnp.mean(xc * xc, axis=-1, keepdims=True)
    return xc * lax.rsqrt(var + LN_EPS)


def _sigmoid(x):
    return 0.5 * jnp.tanh(0.5 * x) + 0.5


def _silu(x):
    return x * _sigmoid(x)


def _softplus(x):
    return jnp.maximum(x, 0.0) + jnp.log(1.0 + jnp.exp(-jnp.abs(x)))


def _gelu_tanh(x):
    return 0.5 * x * (1.0 + jnp.tanh(math.sqrt(2.0 / math.pi) * (x + 0.044715 * (x * x * x))))


def _mod_body(c_ref, w_ref, b_ref, o_ref):
    o_ref[...] = jnp.dot(_silu(c_ref[...]), w_ref[...], preferred_element_type=F32,
                         precision=HI) + b_ref[...]


def _mod_vectors(cc, w_mod, b_mod):
    rows, d = cc.shape
    n = w_mod.shape[1]
    tn = 1536
    return pl.pallas_call(
        _mod_body,
        grid=(n // tn,),
        in_specs=[pl.BlockSpec((rows, d), lambda j: (0, 0)),
                  pl.BlockSpec((d, tn), lambda j: (0, j)),
                  pl.BlockSpec((1, tn), lambda j: (0, j))],
        out_specs=pl.BlockSpec((rows, tn), lambda j: (0, j)),
        out_shape=jax.ShapeDtypeStruct((rows, n), F32),
        compiler_params=_cparams("arbitrary"),
        name="mod_vectors",
    )(cc, w_mod, b_mod.reshape(1, n))


def _inproj_r_body(x_ref, sc_ref, sh_ref, w_ref, b_ref, *o_refs, splits):
    nb, tt, d = x_ref.shape
    tm = nb * tt
    xn = _ln(jnp.swapaxes(x_ref[...], 0, 1).reshape(tm, d)).reshape(tt, nb, d)
    xm = (xn * (1.0 + sc_ref[...])[None] + sh_ref[...][None]).reshape(tm, d).astype(BF16)
    for o_ref, (lo, hi) in zip(o_refs, splits):
        for n0 in range(lo, hi, 512):
            o_ref[:, n0 - lo:n0 - lo + 512] = (
                jnp.dot(xm, w_ref[:, n0:n0 + 512], preferred_element_type=F32) + b_ref[:, n0:n0 + 512])


def _inproj_raster(x, sc, sh, w, b, splits, tt):
    nb, length, d = x.shape
    n = w.shape[1]
    tm = tt * nb
    body = functools.partial(_inproj_r_body, splits=splits)
    return pl.pallas_call(
        body,
        grid=(length // tt,),
        in_specs=[pl.BlockSpec((nb, tt, d), lambda i: (0, i, 0)),
                  pl.BlockSpec((nb, d), lambda i: (0, 0)),
                  pl.BlockSpec((nb, d), lambda i: (0, 0)),
                  pl.BlockSpec((d, n), lambda i: (0, 0)),
                  pl.BlockSpec((1, n), lambda i: (0, 0))],
        out_specs=[pl.BlockSpec((tm, hi - lo), lambda i: (i, 0)) for lo, hi in splits],
        out_shape=[jax.ShapeDtypeStruct((length * nb, hi - lo), F32) for lo, hi in splits],
        compiler_params=_cparams("parallel"),
        name="inproj_raster",
    )(x, sc, sh, w, b)


def _inproj_g_body(x_ref, sc_ref, sh_ref, w_ref, b_ref, *o_refs, splits):
    gb, ch, d = x_ref.shape
    parts = []
    for i in range(gb):
        xn = _ln(x_ref[i])
        parts.append((xn * (1.0 + sc_ref[i:i + 1, :]) + sh_ref[i:i + 1, :]).astype(BF16))
    xm = jnp.concatenate(parts, axis=0)
    for o_ref, (lo, hi) in zip(o_refs, splits):
        step = 512 if (hi - lo) % 512 == 0 else hi - lo
        for n0 in range(lo, hi, step):
            res = jnp.dot(xm, w_ref[:, n0:n0 + step], preferred_element_type=F32) + b_ref[:, n0:n0 + step]
            for i in range(gb):
                o_ref[0, i, :, n0 - lo:n0 - lo + step] = res[i * ch:(i + 1) * ch]


def _inproj_column(xv, x_index_map, n_chunks, sc, sh, w, b, splits):
    nb, d = sc.shape
    gb = 8
    n = w.shape[1]
    body = functools.partial(_inproj_g_body, splits=splits)
    return pl.pallas_call(
        body,
        grid=(n_chunks, nb // gb),
        in_specs=[pl.BlockSpec((gb, GDN_CHUNK, d), x_index_map),
                  pl.BlockSpec((gb, d), lambda c, g: (g, 0)),
                  pl.BlockSpec((gb, d), lambda c, g: (g, 0)),
                  pl.BlockSpec((d, n), lambda c, g: (0, 0)),
                  pl.BlockSpec((1, n), lambda c, g: (0, 0))],
        out_specs=[pl.BlockSpec((1, gb, GDN_CHUNK, hi - lo), lambda c, g: (c, g, 0, 0)) for lo, hi in splits],
        out_shape=[jax.ShapeDtypeStruct((n_chunks, nb, GDN_CHUNK, hi - lo), F32) for lo, hi in splits],
        compiler_params=_cparams("parallel", "parallel"),
        name="inproj_column",
    )(xv, sc, sh, w, b)


def _inproj_gl_body(x_ref, sc_ref, sh_ref, w_ref, b_ref, *o_refs, splits):
    _, tm, d = x_ref.shape
    rr = tm // GRID_W
    xs = jnp.swapaxes(x_ref[0].reshape(rr, GRID_W, d), 0, 1).reshape(tm, d)
    xm = (_ln(xs) * (1.0 + sc_ref[0]) + sh_ref[0]).astype(BF16)
    for o_ref, (lo, hi) in zip(o_refs, splits):
        step = 512 if (hi - lo) % 512 == 0 else hi - lo
        for n0 in range(lo, hi, step):
            res = jnp.dot(xm, w_ref[:, n0:n0 + step], preferred_element_type=F32) + b_ref[:, n0:n0 + step]
            o_ref[:, 0, :, n0 - lo:n0 - lo + step] = res.reshape(GRID_W, rr, step)


def _inproj_column_lat(x, sc, sh, w, b, splits):
    nb, length, d = x.shape
    n = w.shape[1]
    rows = length // GRID_W
    rr = 8
    tm = rr * GRID_W
    body = functools.partial(_inproj_gl_body, splits=splits)
    vec = pl.BlockSpec((1, 1, d), lambda bi, ri: (bi, 0, 0))
    return pl.pallas_call(
        body,
        grid=(nb, rows // rr),
        in_specs=[pl.BlockSpec((1, tm, d), lambda bi, ri: (bi, ri, 0)),
                  vec, vec,
                  pl.BlockSpec((d, n), lambda bi, ri: (0, 0)),
                  pl.BlockSpec((1, n), lambda bi, ri: (0, 0))],
        out_specs=[pl.BlockSpec((GRID_W, 1, rr, hi - lo), lambda bi, ri: (0, bi, ri, 0)) for lo, hi in splits],
        out_shape=[jax.ShapeDtypeStruct((GRID_W, nb, rows, hi - lo), F32) for lo, hi in splits],
        compiler_params=_cparams("parallel", "parallel"),
        name="inproj_column_lat",
    )(x, sc.reshape(nb, 1, d), sh.reshape(nb, 1, d), w, b)


def _lru_body(*refs, tt, nb, reverse, emit):
    if emit == "ya":
        (xa_ref, prev_ref, next_ref, cw_ref, cb_ref, wa_ref, ba_ref, wx_ref, bx_ref, lam_ref, h0_ref,
         hf_ref, ga_ref, out_ref, hl_ref, a_s, b_s, h_s) = refs
    elif emit == "h":
        (xa_ref, prev_ref, next_ref, cw_ref, cb_ref, wa_ref, ba_ref, wx_ref, bx_ref, lam_ref, h0_ref,
         out_ref, hl_ref, a_s, b_s, h_s) = refs
    else:
        (xa_ref, prev_ref, next_ref, cw_ref, cb_ref, wa_ref, ba_ref, wx_ref, bx_ref, lam_ref, h0_ref,
         hl_ref, a_s, b_s, h_s) = refs
    step = pl.program_id(0)
    nsteps = pl.num_programs(0)
    blk = (nsteps - 1 - step) if reverse else step
    rows = tt * nb

    @pl.when(step == 0)
    def _():
        h_s[...] = h0_ref[...]

    prev = jnp.where(blk == 0, 0.0, prev_ref[...])
    nxt = jnp.where(blk == nsteps - 1, 0.0, next_ref[...])
    ext = jnp.concatenate([prev, xa_ref[...], nxt], axis=0)
    u = (cw_ref[0:1, :] * ext[0:rows] + cw_ref[1:2, :] * ext[nb:rows + nb]
         + cw_ref[2:3, :] * ext[2 * nb:rows + 2 * nb] + cw_ref[3:4, :] * ext[3 * nb:rows + 3 * nb]
         + cb_ref[...])
    for n in range(LRU_BLOCKS):
        sl = slice(n * LRU_BLOCK, (n + 1) * LRU_BLOCK)
        un = u[:, sl]
        ub = un.astype(BF16)
        r = _sigmoid(jnp.dot(ub, wa_ref[n], preferred_element_type=F32) + ba_ref[:, sl])
        ig = _sigmoid(jnp.dot(ub, wx_ref[n], preferred_element_type=F32) + bx_ref[:, sl])
        log_a = (-LRU_C) * r * _softplus(-lam_ref[:, sl])
        a = jnp.exp(log_a)
        a_s[:, sl] = a
        b_s[:, sl] = jnp.sqrt(1.0 - a * a) * (ig * un)

    def scan_step(j, h):
        t = (tt - 1 - j) if reverse else j
        r0 = pl.multiple_of(t * nb, nb)
        h = a_s[pl.ds(r0, nb), :] * h + b_s[pl.ds(r0, nb), :]
        if emit == "h":
            out_ref[pl.ds(r0, nb), :] = h
        elif emit == "ya":
            b_s[pl.ds(r0, nb), :] = h
        return h

    h = lax.fori_loop(0, tt, scan_step, h_s[...], unroll=4)
    h_s[...] = h
    hl_ref[...] = h
    if emit == "ya":
        out_ref[...] = (_gelu_tanh(ga_ref[...]) * (hf_ref[...] + b_s[...])).astype(BF16)


def _lru_direction(xa, lp, di, h0, reverse, emit, tt, hf=None, ga=None):
    t, w = xa.shape
    nb = h0.shape[0]
    rows = tt * nb
    nblk = t // rows
    assert rows % (2 * nb) == 0
    prev_per = rows // nb
    next_per = rows // (2 * nb)
    n_prev = t // nb
    n_next = t // (2 * nb)

    def bi(i):
        return (nblk - 1 - i) if reverse else i

    const2 = lambda i: (0, 0)
    in_specs = [
        pl.BlockSpec((rows, w), lambda i: (bi(i), 0)),
        pl.BlockSpec((nb, w), lambda i: (jnp.maximum(bi(i) * prev_per - 1, 0), 0)),
        pl.BlockSpec((2 * nb, w), lambda i: (jnp.minimum((bi(i) + 1) * next_per, n_next - 1), 0)),
        pl.BlockSpec((CONV_W, w), const2),
        pl.BlockSpec((1, w), const2),
        pl.BlockSpec((LRU_BLOCKS, LRU_BLOCK, LRU_BLOCK), lambda i: (0, 0, 0)),
        pl.BlockSpec((1, w), const2),
        pl.BlockSpec((LRU_BLOCKS, LRU_BLOCK, LRU_BLOCK), lambda i: (0, 0, 0)),
        pl.BlockSpec((1, w), const2),
        pl.BlockSpec((1, w), const2),
        pl.BlockSpec((nb, w), const2),
    ]
    args = [xa, xa, xa, lp["conv_a_w"], lp["conv_a_b"].reshape(1, w),
            lp["lru_wa"][di].astype(BF16), lp["lru_ba"][di].reshape(1, w),
            lp["lru_wx"][di].astype(BF16), lp["lru_bx"][di].reshape(1, w),
            lp["lru_lambda"][di].reshape(1, w), h0]
    out_specs = []
    out_shape = []
    if emit == "ya":
        in_specs += [pl.BlockSpec((rows, w), lambda i: (bi(i), 0)),
                     pl.BlockSpec((rows, w), lambda i: (bi(i), 0))]
        args += [hf, ga]
        out_specs.append(pl.BlockSpec((rows, w), lambda i: (bi(i), 0)))
        out_shape.append(jax.ShapeDtypeStruct((t, w), BF16))
    elif emit == "h":
        out_specs.append(pl.BlockSpec((rows, w), lambda i: (bi(i), 0)))
        out_shape.append(jax.ShapeDtypeStruct((t, w), F32))
    out_specs.append(pl.BlockSpec((nb, w), const2))
    out_shape.append(jax.ShapeDtypeStruct((nb, w), F32))
    body = functools.partial(_lru_body, tt=tt, nb=nb, reverse=reverse, emit=emit)
    return pl.pallas_call(
        body,
        grid=(nblk,),
        in_specs=in_specs,
        out_specs=out_specs,
        out_shape=out_shape,
        scratch_shapes=[pltpu.VMEM((rows, w), F32), pltpu.VMEM((rows, w), F32), pltpu.VMEM((nb, w), F32)],
        compiler_params=_cparams("arbitrary"),
        name="lru_" + ("bwd" if reverse else "fwd") + "_" + emit,
    )(*args)


def _gdn_prep_body(qkv_ref, prev_ref, next_ref, lg_ref, cw_ref, ga_ref, gd_ref, qkv_o, g_o):
    c = pl.program_id(0)
    nc = pl.num_programs(0)
    ch = GDN_CHUNK
    gb = qkv_ref.shape[1]
    for i in range(gb):
        prev = jnp.where(c == 0, 0.0, prev_ref[0, i])
        nxt = jnp.where(c == nc - 1, 0.0, next_ref[0, i])
        ext = jnp.concatenate([prev, qkv_ref[0, i], nxt], axis=0)
        u = (cw_ref[0:1, :] * ext[7:7 + ch] + cw_ref[1:2, :] * ext[8:8 + ch]
             + cw_ref[2:3, :] * ext[9:9 + ch] + cw_ref[3:4, :] * ext[10:10 + ch])
        act = _silu(u)
        for h in range(GDN_HEADS):
            sq = slice(h * GDN_DK, (h + 1) * GDN_DK)
            qh = act[:, sq]
            qn = qh * lax.rsqrt(jnp.sum(qh * qh, axis=-1, keepdims=True) + L2_EPS) * (GDN_DK ** -0.5)
            qkv_o[0, i, :, sq] = qn.astype(qkv_o.dtype)
            sk = slice(GDN_QK + h * GDN_DK, GDN_QK + (h + 1) * GDN_DK)
            kh = act[:, sk]
            kn = kh * lax.rsqrt(jnp.sum(kh * kh, axis=-1, keepdims=True) + L2_EPS)
            qkv_o[0, i, :, sk] = kn.astype(qkv_o.dtype)
        qkv_o[0, i, :, 2 * GDN_QK:] = act[:, 2 * GDN_QK:].astype(qkv_o.dtype)
        lg = lg_ref[0, i]
        lane = lax.broadcasted_iota(jnp.int32, lg.shape, 1)
        decay = ga_ref[...] * _softplus(lg + gd_ref[...])
        g_o[0, i] = jnp.where(lane < 2 * GDN_HEADS, decay, jnp.where(lane < N_GATES, _sigmoid(lg), 0.0))


def _gdn_prep(qkv, lg, conv_w, a_row, dt_row, gb):
    nc, nb, ch, n = qkv.shape
    hb = ch // 8
    return pl.pallas_call(
        _gdn_prep_body,
        grid=(nc, nb // gb),
        in_specs=[pl.BlockSpec((1, gb, ch, n), lambda c, b: (c, b, 0, 0)),
                  pl.BlockSpec((1, gb, 8, n), lambda c, b: (jnp.maximum(c - 1, 0), b, hb - 1, 0)),
                  pl.BlockSpec((1, gb, 8, n), lambda c, b: (jnp.minimum(c + 1, nc - 1), b, 0, 0)),
                  pl.BlockSpec((1, gb, ch, LANES), lambda c, b: (c, b, 0, 0)),
                  pl.BlockSpec((CONV_W, n), lambda c, b: (0, 0)),
                  pl.BlockSpec((1, LANES), lambda c, b: (0, 0)),
                  pl.BlockSpec((1, LANES), lambda c, b: (0, 0))],
        out_specs=[pl.BlockSpec((1, gb, ch, n), lambda c, b: (c, b, 0, 0)),
                   pl.BlockSpec((1, gb, ch, LANES), lambda c, b: (c, b, 0, 0))],
        out_shape=[jax.ShapeDtypeStruct(qkv.shape, BF16), jax.ShapeDtypeStruct(lg.shape, F32)],
        compiler_params=_cparams("parallel", "parallel"),
        name="gdn_prep",
    )(qkv, qkv, qkv, lg, conv_w, a_row, dt_row)


def _gdn_scan_body(qkv_ref, g_ref, s0_ref, *rest, reverse, emit_o):
    if emit_o:
        o_ref, sl_ref, s_s = rest
    else:
        sl_ref, s_s = rest
    step = pl.program_id(1)
    ch = GDN_CHUNK
    gb = qkv_ref.shape[1]
    n_sq = int(math.log2(ch)) - 1

    @pl.when(step == 0)
    def _():
        s_s[...] = s0_ref[...]

    row = lax.broadcasted_iota(jnp.int32, (ch, ch), 0)
    col = lax.broadcasted_iota(jnp.int32, (ch, ch), 1)
    incl = (row <= col) if reverse else (row >= col)
    strict = (row < col) if reverse else (row > col)
    eye = (row == col).astype(F32)
    tri = incl.astype(F32)
    goff = GDN_HEADS if reverse else 0
    boff = 2 * GDN_HEADS + goff

    units = [(i, h) for i in range(gb) for h in range(GDN_HEADS)]
    nu = len(units)
    gcum, gcum_t, gtot, gates = [], [], [], []
    for i in range(gb):
        g = g_ref[0, i]
        gates.append(g)
        cum = jnp.dot(tri, g, preferred_element_type=F32, precision=HI)
        gcum.append(cum)
        gcum_t.append(cum.T)
        gtot.append(jnp.sum(g, axis=0, keepdims=True))

    q, k, kb, xin, decay, eg, egl, gl = [], [], [], [], [], [], [], []
    for i, h in units:
        qh = qkv_ref[0, i, :, h * GDN_DK:(h + 1) * GDN_DK].astype(F32)
        kh = qkv_ref[0, i, :, GDN_QK + h * GDN_DK:GDN_QK + (h + 1) * GDN_DK].astype(F32)
        vh = qkv_ref[0, i, :, 2 * GDN_QK + h * GDN_DV:2 * GDN_QK + (h + 1) * GDN_DV].astype(F32)
        gc = jnp.broadcast_to(gcum[i][:, goff + h:goff + h + 1], (ch, GDN_DK))
        beta = jnp.broadcast_to(gates[i][:, boff + h:boff + h + 1], (ch, GDN_DK))
        gr = gcum_t[i][goff + h:goff + h + 1, :]
        gt = gtot[i][:, goff + h:goff + h + 1]
        decay.append(jnp.where(incl, jnp.exp(jnp.where(incl, gc[:, :ch] - gr, 0.0)), 0.0))
        e = jnp.exp(gc)
        kbh = kh * beta
        q.append(qh)
        k.append(kh)
        kb.append(kbh)
        eg.append(e)
        egl.append(jnp.exp(gt - gc))
        gl.append(jnp.exp(gt))
        xin.append(jnp.concatenate([kbh * e, vh * beta], axis=1).astype(BF16))

    gram = [_bdot_nt(jnp.concatenate([kb[j], q[j]], axis=0), k[j]) for j in range(nu)]
    nil = [-jnp.where(strict, gram[j][:ch] * decay[j], 0.0) for j in range(nu)]
    amat = [gram[j][ch:] * decay[j] for j in range(nu)]
    inv = [eye + m for m in nil]
    power = nil
    for _ in range(n_sq):
        pb = [p.astype(BF16) for p in power]
        power = [jnp.dot(p, p, preferred_element_type=F32) for p in pb]
        inv = [iv + _bdot(iv, p) for iv, p in zip(inv, power)]
    wu = [_bdot(iv, x) for iv, x in zip(inv, xin)]
    s_old = [s_s[i, h] for i, h in units]
    s_bf = [s.astype(BF16) for s in s_old]
    ws = [_bdot(jnp.concatenate([wu[j][:, :GDN_DK], q[j] * eg[j]], axis=0), s_bf[j]) for j in range(nu)]
    v_new = [(wu[j][:, GDN_DK:] - ws[j][:ch]).astype(BF16) for j in range(nu)]
    upd = [_bdot(jnp.concatenate([amat[j], (k[j] * egl[j]).T], axis=0), v_new[j]) for j in range(nu)]
    for j, (i, h) in enumerate(units):
        if emit_o:
            o_ref[0, i, :, h * GDN_DV:(h + 1) * GDN_DV] = ws[j][ch:] + upd[j][:ch]
        s_new = s_old[j] * gl[j] + upd[j][ch:]
        s_s[i, h] = s_new
        sl_ref[i, h] = s_new


def _gdn_direction(qkv, gates, s0, reverse, emit_o, gb=1):
    nc, nb, ch, n = qkv.shape

    def ci(c):
        return (nc - 1 - c) if reverse else c

    out_specs = []
    out_shape = []
    if emit_o:
        out_specs.append(pl.BlockSpec((1, gb, ch, GDN_VW), lambda b, c: (ci(c), b, 0, 0)))
        out_shape.append(jax.ShapeDtypeStruct((nc, nb, ch, GDN_VW), F32))
    out_specs.append(pl.BlockSpec((gb, GDN_HEADS, GDN_DK, GDN_DV), lambda b, c: (b, 0, 0, 0)))
    out_shape.append(jax.ShapeDtypeStruct((nb, GDN_HEADS, GDN_DK, GDN_DV), F32))
    body = functools.partial(_gdn_scan_body, reverse=reverse, emit_o=emit_o)
    return pl.pallas_call(
        body,
        grid=(nb // gb, nc),
        in_specs=[pl.BlockSpec((1, gb, ch, n), lambda b, c: (ci(c), b, 0, 0)),
                  pl.BlockSpec((1, gb, ch, LANES), lambda b, c: (ci(c), b, 0, 0)),
                  pl.BlockSpec((gb, GDN_HEADS, GDN_DK, GDN_DV), lambda b, c: (b, 0, 0, 0))],
        out_specs=out_specs,
        out_shape=out_shape,
        scratch_shapes=[pltpu.VMEM((gb, GDN_HEADS, GDN_DK, GDN_DV), F32)],
        compiler_params=_cparams("parallel", "arbitrary"),
        name="gdn_scan_" + ("bwd" if reverse else "fwd"),
    )(qkv, gates, s0)


def _gdn_out_body(of_ref, ob_ref, z_ref, nw_ref, y_ref, y_s):
    nb = of_ref.shape[1]
    for i in range(nb):
        o = of_ref[0, i] + ob_ref[0, i]
        zg = _silu(z_ref[0, i])
        for h in range(GDN_HEADS):
            sl = slice(h * GDN_DV, (h + 1) * GDN_DV)
            oh = o[:, sl]
            nh = oh * lax.rsqrt(jnp.mean(oh * oh, axis=-1, keepdims=True) + LN_EPS) * nw_ref[...]
            y_s[i, :, sl] = nh * zg[:, sl]
    y_ref[:, 0] = jnp.swapaxes(y_s[...], 0, 1).astype(BF16)


def _gdn_out(o_f, o_b, z, norm_w):
    nc, nb, ch, vw = o_f.shape
    spec = pl.BlockSpec((1, nb, ch, vw), lambda c: (c, 0, 0, 0))
    y = pl.pallas_call(
        _gdn_out_body,
        grid=(nc,),
        in_specs=[spec, spec, spec, pl.BlockSpec((1, GDN_DV), lambda c: (0, 0))],
        out_specs=pl.BlockSpec((ch, 1, nb, vw), lambda c: (0, c, 0, 0)),
        out_shape=jax.ShapeDtypeStruct((ch, nc, nb, vw), BF16),
        scratch_shapes=[pltpu.VMEM((nb, ch, vw), F32)],
        compiler_params=_cparams("parallel"),
        name="gdn_out",
    )(o_f, o_b, z, norm_w.reshape(1, GDN_DV))
    return y.reshape(ch * nc * nb, vw)


def _mixer_body(ya_ref, yb_ref, mg_ref, x_ref, g1_ref, sh2_ref, sc2_ref, wpa_ref, wpb_ref, wo_ref,
                lg_ref, lb_ref, wrh_ref, wrl_ref, br_ref, x1_ref, h2_ref, rt_ref):
    nb, tt, d = x_ref.shape
    tm = nb * tt
    xs = jnp.swapaxes(x_ref[...], 0, 1).reshape(tm, d)
    pa = jnp.dot(ya_ref[...], wpa_ref[...], preferred_element_type=F32)
    pb = jnp.dot(yb_ref[...], wpb_ref[...], preferred_element_type=F32)
    merged = _sigmoid(mg_ref[:, :d]) * pa + _sigmoid(mg_ref[:, d:]) * pb
    mix = _bdot(merged, wo_ref[...])
    mix3 = mix.reshape(tm // nb, nb, d) * g1_ref[...][None]
    x1 = _ln(DN_ALPHA * xs + mix3.reshape(tm, d)) * lg_ref[...] + lb_ref[...]
    x1_ref[...] = x1
    h3 = _ln(x1).reshape(tm // nb, nb, d) * (1.0 + sc2_ref[...])[None] + sh2_ref[...][None]
    h2 = h3.reshape(tm, d)
    h2_hi = h2.astype(BF16)
    h2_ref[...] = h2_hi

    h2_lo = (h2 - h2_hi.astype(F32)).astype(BF16)
    logits = (jnp.dot(h2_hi, wrh_ref[...], preferred_element_type=F32)
              + jnp.dot(h2_lo, wrh_ref[...], preferred_element_type=F32)
              + jnp.dot(h2_hi, wrl_ref[...], preferred_element_type=F32) + br_ref[...])
    lane = lax.broadcasted_iota(jnp.int32, logits.shape, 1).astype(F32)
    neg = jnp.float32(-jnp.inf)
    big = jnp.float32(1 << 20)
    is_g = lane < N_GROUPS
    gl = jnp.where(is_g, logits, neg)
    gmax = jnp.max(gl, axis=-1, keepdims=True)
    gsum = jnp.sum(jnp.where(is_g, jnp.exp(gl - gmax), 0.0), axis=-1, keepdims=True)
    p_group = 1.0 / gsum
    g_idx = jnp.min(jnp.where(gl == gmax, lane, big), axis=-1, keepdims=True)
    lo = N_GROUPS + g_idx * EXP_PER_GROUP
    in_g = (lane >= lo) & (lane < lo + EXP_PER_GROUP)
    el = jnp.where(in_g, logits, neg)
    m1 = jnp.max(el, axis=-1, keepdims=True)
    i1 = jnp.min(jnp.where(el == m1, lane, big), axis=-1, keepdims=True)
    el2 = jnp.where(lane == i1, neg, el)
    m2 = jnp.max(el2, axis=-1, keepdims=True)
    i2 = jnp.min(jnp.where(el2 == m2, lane, big), axis=-1, keepdims=True)
    esum = jnp.sum(jnp.where(in_g, jnp.exp(el - m1), 0.0), axis=-1, keepdims=True)
    p1 = 1.0 / esum
    p2 = jnp.exp(m2 - m1) / esum
    w1 = p_group * p1 / (p1 + p2)
    w2 = p_group * p2 / (p1 + p2)
    e1 = i1 - N_GROUPS
    e2 = i2 - N_GROUPS
    rt_ref[...] = jnp.where(lane == 0, e1, jnp.where(lane == 1, e2, jnp.where(lane == 2, w1,
                            jnp.where(lane == 3, w2, 0.0))))


def _mixer(ya, yb, mg, x, g1, sh2, sc2, w_pa, w_pb, w_out, ln_g, ln_b, w_r_hi, w_r_lo, b_r, tt):
    nb, length, d = x.shape
    t = nb * length
    tm = tt * nb
    row = lambda n: pl.BlockSpec((tm, n), lambda i: (i, 0))
    full = lambda a: pl.BlockSpec(a.shape, lambda i: (0,) * a.ndim)
    args = [ya, yb, mg, x, g1, sh2, sc2, w_pa, w_pb, w_out, ln_g, ln_b, w_r_hi, w_r_lo, b_r]
    in_specs = ([row(d), row(d), row(2 * d), pl.BlockSpec((nb, tt, d), lambda i: (0, i, 0))]
                + [full(a) for a in args[4:]])
    body = _mixer_body
    return pl.pallas_call(
        body,
        grid=(t // tm,),
        in_specs=in_specs,
        out_specs=[row(d), row(d), row(LANES)],
        out_shape=[jax.ShapeDtypeStruct((t, d), F32), jax.ShapeDtypeStruct((t, d), BF16),
                   jax.ShapeDtypeStruct((t, LANES), F32)],
        compiler_params=_cparams("parallel"),
        name="mixer_router",
    )(*args)


def _experts_body(be_ref, nu_ref, x_ref, wg_ref, wu_ref, wd_ref, y_ref, wg_s, wu_s, wd_s):
    i = pl.program_id(0)
    used = i < nu_ref[0]
    new_expert = jnp.logical_or(i == 0, be_ref[i] != be_ref[jnp.maximum(i - 1, 0)])

    @pl.when(jnp.logical_and(used, new_expert))
    def _():
        wg_s[...] = wg_ref[0].astype(BF16)
        wu_s[...] = wu_ref[0].astype(BF16)
        wd_s[...] = wd_ref[0].astype(BF16)

    @pl.when(used)
    def _():
        x = x_ref[...]
        hg = jnp.dot(x, wg_s[...], preferred_element_type=F32)
        hu = jnp.dot(x, wu_s[...], preferred_element_type=F32)
        y_ref[...] = _bdot(_silu(hg) * hu, wd_s[...]).astype(y_ref.dtype)

    @pl.when(jnp.logical_not(used))
    def _():
        y_ref[...] = jnp.zeros_like(y_ref)


def _experts(x_pad, block_expert, n_used, w_gate, w_up, w_down, tm):
    rows, d = x_pad.shape
    de = w_gate.shape[2]
    nblk = rows // tm
    wspec = lambda k, n: pl.BlockSpec((1, k, n), lambda i, be, nu: (be[i], 0, 0))
    return pl.pallas_call(
        _experts_body,
        grid_spec=pltpu.PrefetchScalarGridSpec(
            num_scalar_prefetch=2,
            grid=(nblk,),
            in_specs=[pl.BlockSpec((tm, d), lambda i, be, nu: (i, 0)),
                      wspec(d, de), wspec(d, de), wspec(de, d)],
            out_specs=pl.BlockSpec((tm, d), lambda i, be, nu: (i, 0)),
            scratch_shapes=[pltpu.VMEM((d, de), BF16), pltpu.VMEM((d, de), BF16), pltpu.VMEM((de, d), BF16)],
        ),
        out_shape=jax.ShapeDtypeStruct((rows, d), BF16),
        compiler_params=_cparams("arbitrary"),
        name="experts",
    )(block_expert, n_used, x_pad, w_gate, w_up, w_down)


def _final_body(x1_ref, y1_ref, y2_ref, rt_ref, g2_ref, lg_ref, lb_ref, o_ref):
    tm, d = x1_ref.shape
    nb = g2_ref.shape[0]
    y = rt_ref[:, 2:3] * y1_ref[...].astype(F32) + rt_ref[:, 3:4] * y2_ref[...].astype(F32)
    gy = (y.reshape(tm // nb, nb, d) * g2_ref[...][None]).reshape(tm, d)
    out = _ln(DN_ALPHA * x1_ref[...] + gy) * lg_ref[...] + lb_ref[...]
    o_ref[...] = jnp.swapaxes(out.reshape(tm // nb, nb, d), 0, 1)


def _final(x1_tb, y1, y2, route, g2, ln_g, ln_b, tq):
    t, d = x1_tb.shape
    nb = g2.shape[0]
    length = t // nb
    tm = tq * nb
    row = lambda n: pl.BlockSpec((tm, n), lambda i: (i, 0))
    return pl.pallas_call(
        _final_body,
        grid=(length // tq,),
        in_specs=[row(d), row(d), row(d), row(LANES),
                  pl.BlockSpec((nb, d), lambda i: (0, 0)),
                  pl.BlockSpec((1, d), lambda i: (0, 0)),
                  pl.BlockSpec((1, d), lambda i: (0, 0))],
        out_specs=pl.BlockSpec((nb, tq, d), lambda i: (0, i, 0)),
        out_shape=jax.ShapeDtypeStruct((nb, length, d), F32),
        compiler_params=_cparams("parallel"),
        name="moe_combine_postnorm",
    )(x1_tb, y1, y2, route, g2, ln_g, ln_b)


MOE_TM = 512
GDN_GB = 4


def _route_plan(e1, e2, tm):
    t = e1.shape[0]
    e_flat = jnp.stack([e1, e2], axis=1).reshape(-1)
    n_assign = e_flat.shape[0]
    experts = jnp.arange(N_EXPERTS, dtype=jnp.int32)
    onehot = (e_flat[:, None] == experts[None, :]).astype(jnp.int32)
    csum = jnp.cumsum(onehot, axis=0)
    rank = jnp.sum(jnp.where(onehot > 0, csum - 1, 0), axis=1)
    counts = csum[-1]
    starts = jnp.cumsum(counts) - counts
    padded = (counts + tm - 1) // tm * tm
    pends = jnp.cumsum(padded)
    pstarts = pends - padded
    dest = pstarts[e_flat] + rank
    n_blocks = (n_assign + N_EXPERTS * (tm - 1) + tm - 1) // tm
    n_used = (pends[-1] // tm).astype(jnp.int32).reshape(1)
    blk0 = jnp.arange(n_blocks, dtype=jnp.int32) * tm
    block_expert = jnp.minimum(jnp.sum((pends[None, :] <= blk0[:, None]).astype(jnp.int32), axis=1),
                               N_EXPERTS - 1).astype(jnp.int32)
    order = jnp.argsort(e_flat, stable=True).astype(jnp.int32)
    p = jnp.arange(n_blocks * tm, dtype=jnp.int32)
    pe = jnp.repeat(block_expert, tm)
    r = p - pstarts[pe]
    valid = r < counts[pe]
    a_idx = jnp.clip(starts[pe] + r, 0, n_assign - 1)
    src = jnp.where(valid, order.at[a_idx].get(mode="promise_in_bounds") // TOP_K, p % t)
    return src, dest.reshape(t, TOP_K), block_expert, n_used


def kernel(x, c, ctx, c_ctx, w_mod, b_mod, w_in, b_in, conv_a_w, conv_a_b, lru_wa, lru_ba, lru_wx, lru_bx,
           lru_lambda, conv_qkv_w, gdn_a_log, gdn_dt_bias, gdn_norm_w, w_pa, w_pb, w_out, ln1_g, ln1_b,
           w_router_g, b_router_g, w_router_e, b_router_e, w_e_gate, w_e_up, w_e_down, ln2_g, ln2_b):
    nb, n_lat, d = x.shape
    n_ctx = ctx.shape[1]
    rows = n_lat // GRID_W
    assert rows == GDN_CHUNK and n_ctx % GDN_CHUNK == 0 and w_mod.shape[0] == 1
    layer = 0
    lp = {"conv_a_w": conv_a_w[layer], "conv_a_b": conv_a_b[layer], "lru_wa": lru_wa[layer],
          "lru_ba": lru_ba[layer], "lru_wx": lru_wx[layer], "lru_bx": lru_bx[layer],
          "lru_lambda": lru_lambda[layer]}

    pad_rows = (-(nb + 1)) % 8
    cc = jnp.concatenate([c, c_ctx[None, :], jnp.zeros((pad_rows, d), F32)], axis=0)
    mod = _mod_vectors(cc, w_mod[layer], b_mod[layer])
    sh1, sc1, g1, sh2, sc2, g2 = [mod[:nb, j * d:(j + 1) * d] for j in range(6)]
    csh1 = jnp.broadcast_to(mod[nb:nb + 1, 0:d], (nb, d))
    csc1 = jnp.broadcast_to(mod[nb:nb + 1, d:2 * d], (nb, d))

    w_l = w_in[layer]
    b_l = b_in[layer]
    w_r = jnp.concatenate([w_l[:, OFF_XA:OFF_GDN], w_l[:, OFF_MG:]], axis=1).astype(BF16)
    b_r = jnp.concatenate([b_l[OFF_XA:OFF_GDN], b_l[OFF_MG:]])[None, :]
    gate_pad = LANES - N_GATES
    w_g = jnp.concatenate([w_l[:, OFF_GDN:OFF_GDN + QKV_COLS],
                           jnp.pad(w_l[:, OFF_GDN + QKV_COLS:OFF_Z], ((0, 0), (0, gate_pad))),
                           w_l[:, OFF_Z:OFF_MG]], axis=1).astype(BF16)
    b_g = jnp.concatenate([b_l[OFF_GDN:OFF_GDN + QKV_COLS],
                           jnp.pad(b_l[OFF_GDN + QKV_COLS:OFF_Z], (0, gate_pad)),
                           b_l[OFF_Z:OFF_MG]])[None, :]
    n_qg = QKV_COLS + LANES

    (xa_ctx,) = _inproj_raster(ctx, csc1, csh1, w_r[:, :LRU_WIDTH], b_r[:, :LRU_WIDTH],
                               [(0, LRU_WIDTH)], tt=32)
    zero_lru = jnp.zeros((nb, LRU_WIDTH), F32)
    (sa_f,) = _lru_direction(xa_ctx, lp, 0, zero_lru, False, "none", tt=32)
    (sa_b,) = _lru_direction(xa_ctx, lp, 1, zero_lru, True, "none", tt=32)

    nc_ctx = n_ctx // GDN_CHUNK
    qkv_c, lg_c = _inproj_column(ctx, lambda ci, g: (g, ci, 0), nc_ctx, csc1, csh1,
                                 w_g[:, :n_qg], b_g[:, :n_qg], [(0, QKV_COLS), (QKV_COLS, n_qg)])
    a_row = jnp.pad(-jnp.exp(gdn_a_log[layer].reshape(-1)), (0, LANES - 2 * GDN_HEADS))[None, :]
    dt_row = jnp.pad(gdn_dt_bias[layer].reshape(-1), (0, LANES - 2 * GDN_HEADS))[None, :]
    cw_qkv = conv_qkv_w[layer]
    qkv_c, gates_c = _gdn_prep(qkv_c, lg_c, cw_qkv, a_row, dt_row, GDN_GB)
    zero_gdn = jnp.zeros((nb, GDN_HEADS, GDN_DK, GDN_DV), F32)
    (sb_f,) = _gdn_direction(qkv_c, gates_c, zero_gdn, False, False, GDN_GB)
    (sb_b,) = _gdn_direction(qkv_c, gates_c, zero_gdn, True, False, GDN_GB)

    xa, ga, mg = _inproj_raster(x, sc1, sh1, w_r, b_r,
                                [(0, LRU_WIDTH), (LRU_WIDTH, 2 * LRU_WIDTH), (2 * LRU_WIDTH, 2 * LRU_WIDTH + 2 * d)],
                                tt=32)
    h_f, _ = _lru_direction(xa, lp, 0, sa_f, False, "h", tt=32)
    ya, _ = _lru_direction(xa, lp, 1, sa_b, True, "ya", tt=32, hf=h_f, ga=ga)

    qkv_l, lg_l, z_l = _inproj_column_lat(x, sc1, sh1, w_g, b_g,
                                          [(0, QKV_COLS), (QKV_COLS, n_qg), (n_qg, n_qg + GDN_VW)])
    qkv_l, gates_l = _gdn_prep(qkv_l, lg_l, cw_qkv, a_row, dt_row, GDN_GB)
    o_f, _ = _gdn_direction(qkv_l, gates_l, sb_f, False, True, GDN_GB)
    o_b, _ = _gdn_direction(qkv_l, gates_l, sb_b, True, True, GDN_GB)
    yb = _gdn_out(o_f, o_b, z_l, gdn_norm_w[layer])

    w_rt = jnp.pad(jnp.concatenate([w_router_g[layer], w_router_e[layer]], axis=1),
                   ((0, 0), (0, LANES - N_GROUPS - N_EXPERTS)))
    b_rt = jnp.pad(jnp.concatenate([b_router_g[layer], b_router_e[layer]]),
                   (0, LANES - N_GROUPS - N_EXPERTS))[None, :]
    w_rt_hi = w_rt.astype(BF16)
    w_rt_lo = (w_rt - w_rt_hi.astype(F32)).astype(BF16)
    x1, h2, route = _mixer(ya, yb, mg, x, g1, sh2, sc2, w_pa[layer].astype(BF16), w_pb[layer].astype(BF16),
                           w_out[layer].astype(BF16), ln1_g[layer][None, :], ln1_b[layer][None, :],
                           w_rt_hi, w_rt_lo, b_rt, tt=32)

    e1 = route[:, 0].astype(jnp.int32)
    e2 = route[:, 1].astype(jnp.int32)
    src, dest, block_expert, n_used = _route_plan(e1, e2, MOE_TM)
    x_pad = h2.at[src].get(mode="promise_in_bounds")
    y_pad = _experts(x_pad, block_expert, n_used, w_e_gate[layer], w_e_up[layer], w_e_down[layer], MOE_TM)
    y1 = y_pad.at[dest[:, 0]].get(mode="promise_in_bounds")
    y2 = y_pad.at[dest[:, 1]].get(mode="promise_in_bounds")
    return _final(x1, y1, y2, route, g2, ln2_g[layer][None, :], ln2_b[layer][None, :], tq=32)
```

```python
import functools
import math

import jax
import jax.numpy as jnp
from jax import lax
from jax.experimental import pallas as pl
from jax.experimental.pallas import tpu as pltpu

F32 = jnp.float32
BF16 = jnp.bfloat16

D_MODEL = 1024
GRID_W = 64
LRU_WIDTH = 1024
LRU_BLOCKS = 8
LRU_BLOCK = LRU_WIDTH // LRU_BLOCKS
LRU_C = 8.0
CONV_W = 4
GDN_HEADS = 8
GDN_DK = 128
GDN_DV = 128
GDN_QK = GDN_HEADS * GDN_DK
GDN_VW = GDN_HEADS * GDN_DV
GDN_CHUNK = 64
QKV_COLS = 2 * GDN_QK + GDN_VW
N_GATES = 4 * GDN_HEADS
OFF_XA = 0
OFF_GA = OFF_XA + LRU_WIDTH
OFF_GDN = OFF_GA + LRU_WIDTH
OFF_Z = OFF_GDN + QKV_COLS + N_GATES
OFF_MG = OFF_Z + GDN_VW
N_GROUPS = 4
EXP_PER_GROUP = 8
N_EXPERTS = N_GROUPS * EXP_PER_GROUP
TOP_K = 2
LN_EPS = 1e-6
L2_EPS = 1e-6
DEPTH = 1
DN_ALPHA = (2.0 * DEPTH) ** 0.25

LANES = 128
VMEM_LIMIT = 56 * 1024 * 1024

HI = lax.Precision.HIGHEST


def _cparams(*sem):
    return pltpu.CompilerParams(dimension_semantics=sem, vmem_limit_bytes=VMEM_LIMIT)


def _bdot(a, b):
    return jnp.dot(a.astype(BF16), b.astype(BF16), preferred_element_type=F32)


def _bdot_nt(a, b):
    return lax.dot_general(a.astype(BF16), b.astype(BF16), (((1,), (1,)), ((), ())),
                           preferred_element_type=F32)


def _ln(x):
    mu = jnp.mean(x, axis=-1, keepdims=True)
    xc = x - mu
    var = jnp.mean(xc * xc, axis=-1, keepdims=True)
    return xc * lax.rsqrt(var + LN_EPS)


def _sigmoid(x):
    return 0.5 * jnp.tanh(0.5 * x) + 0.5


def _silu(x):
    return x * _sigmoid(x)


def _softplus(x):
    return jnp.maximum(x, 0.0) + jnp.log(1.0 + jnp.exp(-jnp.abs(x)))


def _gelu_tanh(x):
    return 0.5 * x * (1.0 + jnp.tanh(math.sqrt(2.0 / math.pi) * (x + 0.044715 * (x * x * x))))


def _mod_body(c_ref, w_ref, b_ref, o_ref):
    o_ref[...] = jnp.dot(_silu(c_ref[...]), w_ref[...], preferred_element_type=F32,
                         precision=HI) + b_ref[...]


def _mod_vectors(cc, w_mod, b_mod):
    rows, d = cc.shape
    n = w_mod.shape[1]
    tn = 1536
    return pl.pallas_call(
        _mod_body,
        grid=(n // tn,),
        in_specs=[pl.BlockSpec((rows, d), lambda j: (0, 0)),
                  pl.BlockSpec((d, tn), lambda j: (0, j)),
                  pl.BlockSpec((1, tn), lambda j: (0, j))],
        out_specs=pl.BlockSpec((rows, tn), lambda j: (0, j)),
        out_shape=jax.ShapeDtypeStruct((rows, n), F32),
        compiler_params=_cparams("arbitrary"),
        name="mod_vectors",
    )(cc, w_mod, b_mod.reshape(1, n))


def _inproj_r_body(x_ref, sc_ref, sh_ref, w_ref, b_ref, *o_refs, splits):
    nb, tt, d = x_ref.shape
    tm = nb * tt
    xn = _ln(jnp.swapaxes(x_ref[...], 0, 1).reshape(tm, d)).reshape(tt, nb, d)
    xm = (xn * (1.0 + sc_ref[...])[None] + sh_ref[...][None]).reshape(tm, d).astype(BF16)
    for o_ref, (lo, hi) in zip(o_refs, splits):
        for n0 in range(lo, hi, 512):
            o_ref[:, n0 - lo:n0 - lo + 512] = (
                jnp.dot(xm, w_ref[:, n0:n0 + 512], preferred_element_type=F32) + b_ref[:, n0:n0 + 512])


def _inproj_raster(x, sc, sh, w, b, splits, tt):
    nb, length, d = x.shape
    n = w.shape[1]
    tm = tt * nb
    body = functools.partial(_inproj_r_body, splits=splits)
    return pl.pallas_call(
        body,
        grid=(length // tt,),
        in_specs=[pl.BlockSpec((nb, tt, d), lambda i: (0, i, 0)),
                  pl.BlockSpec((nb, d), lambda i: (0, 0)),
                  pl.BlockSpec((nb, d), lambda i: (0, 0)),
                  pl.BlockSpec((d, n), lambda i: (0, 0)),
                  pl.BlockSpec((1, n), lambda i: (0, 0))],
        out_specs=[pl.BlockSpec((tm, hi - lo), lambda i: (i, 0)) for lo, hi in splits],
        out_shape=[jax.ShapeDtypeStruct((length * nb, hi - lo), F32) for lo, hi in splits],
        compiler_params=_cparams("parallel"),
        name="inproj_raster",
    )(x, sc, sh, w, b)


def _inproj_g_body(x_ref, sc_ref, sh_ref, w_ref, b_ref, *o_refs, splits):
    gb, ch, d = x_ref.shape
    parts = []
    for i in range(gb):
        xn = _ln(x_ref[i])
        parts.append((xn * (1.0 + sc_ref[i:i + 1, :]) + sh_ref[i:i + 1, :]).astype(BF16))
    xm = jnp.concatenate(parts, axis=0)
    for o_ref, (lo, hi) in zip(o_refs, splits):
        step = 512 if (hi - lo) % 512 == 0 else hi - lo
        for n0 in range(lo, hi, step):
            res = jnp.dot(xm, w_ref[:, n0:n0 + step], preferred_element_type=F32) + b_ref[:, n0:n0 + step]
            for i in range(gb):
                o_ref[0, i, :, n0 - lo:n0 - lo + step] = res[i * ch:(i + 1) * ch]


def _inproj_column(xv, x_index_map, n_chunks, sc, sh, w, b, splits):
    nb, d = sc.shape
    gb = 8
    n = w.shape[1]
    body = functools.partial(_inproj_g_body, splits=splits)
    return pl.pallas_call(
        body,
        grid=(n_chunks, nb // gb),
        in_specs=[pl.BlockSpec((gb, GDN_CHUNK, d), x_index_map),
                  pl.BlockSpec((gb, d), lambda c, g: (g, 0)),
                  pl.BlockSpec((gb, d), lambda c, g: (g, 0)),
                  pl.BlockSpec((d, n), lambda c, g: (0, 0)),
                  pl.BlockSpec((1, n), lambda c, g: (0, 0))],
        out_specs=[pl.BlockSpec((1, gb, GDN_CHUNK, hi - lo), lambda c, g: (c, g, 0, 0)) for lo, hi in splits],
        out_shape=[jax.ShapeDtypeStruct((n_chunks, nb, GDN_CHUNK, hi - lo), F32) for lo, hi in splits],
        compiler_params=_cparams("parallel", "parallel"),
        name="inproj_column",
    )(xv, sc, sh, w, b)


def _inproj_gl_body(x_ref, sc_ref, sh_ref, w_ref, b_ref, qkv_ref, lg_ref, z_ref):
    _, tm, d = x_ref.shape
    rr = tm // GRID_W
    xm = _ln(x_ref[0]) * (1.0 + sc_ref[0]) + sh_ref[0]
    xb = xm.astype(BF16)
    for n0 in range(0, QKV_COLS, 512):
        qkv_ref[0, :, n0:n0 + 512] = (jnp.dot(xb, w_ref[:, n0:n0 + 512], preferred_element_type=F32)
                                      + b_ref[:, n0:n0 + 512])
    xs = jnp.swapaxes(xm.reshape(rr, GRID_W, d), 0, 1).reshape(tm, d).astype(BF16)
    n_qg = QKV_COLS + LANES
    lg = jnp.dot(xs, w_ref[:, QKV_COLS:n_qg], preferred_element_type=F32) + b_ref[:, QKV_COLS:n_qg]
    lg_ref[:, 0] = lg.reshape(GRID_W, rr, LANES)
    for n0 in range(0, GDN_VW, 512):
        res = (jnp.dot(xs, w_ref[:, n_qg + n0:n_qg + n0 + 512], preferred_element_type=F32)
               + b_ref[:, n_qg + n0:n_qg + n0 + 512])
        z_ref[:, 0, :, n0:n0 + 512] = res.reshape(GRID_W, rr, 512)


def _inproj_column_lat(x, sc, sh, w, b):
    nb, length, d = x.shape
    n = w.shape[1]
    rows = length // GRID_W
    rr = 8
    tm = rr * GRID_W
    vec = pl.BlockSpec((1, 1, d), lambda bi, ri: (bi, 0, 0))
    chunked = lambda m: pl.BlockSpec((GRID_W, 1, rr, m), lambda bi, ri: (0, bi, ri, 0))
    return pl.pallas_call(
        _inproj_gl_body,
        grid=(nb, rows // rr),
        in_specs=[pl.BlockSpec((1, tm, d), lambda bi, ri: (bi, ri, 0)),
                  vec, vec,
                  pl.BlockSpec((d, n), lambda bi, ri: (0, 0)),
                  pl.BlockSpec((1, n), lambda bi, ri: (0, 0))],
        out_specs=[pl.BlockSpec((1, tm, QKV_COLS), lambda bi, ri: (bi, ri, 0)), chunked(LANES), chunked(GDN_VW)],
        out_shape=[jax.ShapeDtypeStruct((nb, length, QKV_COLS), F32),
                   jax.ShapeDtypeStruct((GRID_W, nb, rows, LANES), F32),
                   jax.ShapeDtypeStruct((GRID_W, nb, rows, GDN_VW), F32)],
        compiler_params=_cparams("parallel", "parallel"),
        name="inproj_column_lat",
    )(x, sc.reshape(nb, 1, d), sh.reshape(nb, 1, d), w, b)


def _lru_body(*refs, tt, nb, reverse, emit):
    if emit == "ya":
        (xa_ref, prev_ref, next_ref, cw_ref, cb_ref, wa_ref, ba_ref, wx_ref, bx_ref, lam_ref, h0_ref,
         hf_ref, ga_ref, out_ref, hl_ref, a_s, b_s, h_s) = refs
    elif emit == "h":
        (xa_ref, prev_ref, next_ref, cw_ref, cb_ref, wa_ref, ba_ref, wx_ref, bx_ref, lam_ref, h0_ref,
         out_ref, hl_ref, a_s, b_s, h_s) = refs
    else:
        (xa_ref, prev_ref, next_ref, cw_ref, cb_ref, wa_ref, ba_ref, wx_ref, bx_ref, lam_ref, h0_ref,
         hl_ref, a_s, b_s, h_s) = refs
    step = pl.program_id(0)
    nsteps = pl.num_programs(0)
    blk = (nsteps - 1 - step) if reverse else step
    rows = tt * nb

    @pl.when(step == 0)
    def _():
        h_s[...] = h0_ref[...]

    prev = jnp.where(blk == 0, 0.0, prev_ref[...])
    nxt = jnp.where(blk == nsteps - 1, 0.0, next_ref[...])
    ext = jnp.concatenate([prev, xa_ref[...], nxt], axis=0)
    u = (cw_ref[0:1, :] * ext[0:rows] + cw_ref[1:2, :] * ext[nb:rows + nb]
         + cw_ref[2:3, :] * ext[2 * nb:rows + 2 * nb] + cw_ref[3:4, :] * ext[3 * nb:rows + 3 * nb]
         + cb_ref[...])
    for n in range(LRU_BLOCKS):
        sl = slice(n * LRU_BLOCK, (n + 1) * LRU_BLOCK)
        un = u[:, sl]
        ub = un.astype(BF16)
        r = _sigmoid(jnp.dot(ub, wa_ref[n], preferred_element_type=F32) + ba_ref[:, sl])
        ig = _sigmoid(jnp.dot(ub, wx_ref[n], preferred_element_type=F32) + bx_ref[:, sl])
        log_a = (-LRU_C) * r * _softplus(-lam_ref[:, sl])
        a = jnp.exp(log_a)
        a_s[:, sl] = a
        b_s[:, sl] = jnp.sqrt(1.0 - a * a) * (ig * un)

    def scan_step(j, h):
        t = (tt - 1 - j) if reverse else j
        r0 = pl.multiple_of(t * nb, nb)
        h = a_s[pl.ds(r0, nb), :] * h + b_s[pl.ds(r0, nb), :]
        if emit == "h":
            out_ref[pl.ds(r0, nb), :] = h
        elif emit == "ya":
            b_s[pl.ds(r0, nb), :] = h
        return h

    h = lax.fori_loop(0, tt, scan_step, h_s[...], unroll=4)
    h_s[...] = h
    hl_ref[...] = h
    if emit == "ya":
        out_ref[...] = (_gelu_tanh(ga_ref[...]) * (hf_ref[...] + b_s[...])).astype(BF16)


def _lru_direction(xa, lp, di, h0, reverse, emit, tt, hf=None, ga=None):
    t, w = xa.shape
    nb = h0.shape[0]
    rows = tt * nb
    nblk = t // rows
    assert rows % (2 * nb) == 0
    prev_per = rows // nb
    next_per = rows // (2 * nb)
    n_prev = t // nb
    n_next = t // (2 * nb)

    def bi(i):
        return (nblk - 1 - i) if reverse else i

    const2 = lambda i: (0, 0)
    in_specs = [
        pl.BlockSpec((rows, w), lambda i: (bi(i), 0)),
        pl.BlockSpec((nb, w), lambda i: (jnp.maximum(bi(i) * prev_per - 1, 0), 0)),
        pl.BlockSpec((2 * nb, w), lambda i: (jnp.minimum((bi(i) + 1) * next_per, n_next - 1), 0)),
        pl.BlockSpec((CONV_W, w), const2),
        pl.BlockSpec((1, w), const2),
        pl.BlockSpec((LRU_BLOCKS, LRU_BLOCK, LRU_BLOCK), lambda i: (0, 0, 0)),
        pl.BlockSpec((1, w), const2),
        pl.BlockSpec((LRU_BLOCKS, LRU_BLOCK, LRU_BLOCK), lambda i: (0, 0, 0)),
        pl.BlockSpec((1, w), const2),
        pl.BlockSpec((1, w), const2),
        pl.BlockSpec((nb, w), const2),
    ]
    args = [xa, xa, xa, lp["conv_a_w"], lp["conv_a_b"].reshape(1, w),
            lp["lru_wa"][di].astype(BF16), lp["lru_ba"][di].reshape(1, w),
            lp["lru_wx"][di].astype(BF16), lp["lru_bx"][di].reshape(1, w),
            lp["lru_lambda"][di].reshape(1, w), h0]
    out_specs = []
    out_shape = []
    if emit == "ya":
        in_specs += [pl.BlockSpec((rows, w), lambda i: (bi(i), 0)),
                     pl.BlockSpec((rows, w), lambda i: (bi(i), 0))]
        args += [hf, ga]
        out_specs.append(pl.BlockSpec((rows, w), lambda i: (bi(i), 0)))
        out_shape.append(jax.ShapeDtypeStruct((t, w), BF16))
    elif emit == "h":
        out_specs.append(pl.BlockSpec((rows, w), lambda i: (bi(i), 0)))
        out_shape.append(jax.ShapeDtypeStruct((t, w), F32))
    out_specs.append(pl.BlockSpec((nb, w), const2))
    out_shape.append(jax.ShapeDtypeStruct((nb, w), F32))
    body = functools.partial(_lru_body, tt=tt, nb=nb, reverse=reverse, emit=emit)
    return pl.pallas_call(
        body,
        grid=(nblk,),
        in_specs=in_specs,
        out_specs=out_specs,
        out_shape=out_shape,
        scratch_shapes=[pltpu.VMEM((rows, w), F32), pltpu.VMEM((rows, w), F32), pltpu.VMEM((nb, w), F32)],
        compiler_params=_cparams("arbitrary"),
        name="lru_" + ("bwd" if reverse else "fwd") + "_" + emit,
    )(*args)


def _gdn_prep_body(qkv_ref, prev_ref, next_ref, lg_ref, cw_ref, ga_ref, gd_ref, qkv_o, g_o):
    c = pl.program_id(0)
    nc = pl.num_programs(0)
    ch = GDN_CHUNK
    gb = qkv_ref.shape[1]
    for i in range(gb):
        prev = jnp.where(c == 0, 0.0, prev_ref[0, i])
        nxt = jnp.where(c == nc - 1, 0.0, next_ref[0, i])
        ext = jnp.concatenate([prev, qkv_ref[0, i], nxt], axis=0)
        u = (cw_ref[0:1, :] * ext[7:7 + ch] + cw_ref[1:2, :] * ext[8:8 + ch]
             + cw_ref[2:3, :] * ext[9:9 + ch] + cw_ref[3:4, :] * ext[10:10 + ch])
        act = _silu(u)
        for h in range(GDN_HEADS):
            sq = slice(h * GDN_DK, (h + 1) * GDN_DK)
            qh = act[:, sq]
            qn = qh * lax.rsqrt(jnp.sum(qh * qh, axis=-1, keepdims=True) + L2_EPS) * (GDN_DK ** -0.5)
            qkv_o[0, i, :, sq] = qn.astype(qkv_o.dtype)
            sk = slice(GDN_QK + h * GDN_DK, GDN_QK + (h + 1) * GDN_DK)
            kh = act[:, sk]
            kn = kh * lax.rsqrt(jnp.sum(kh * kh, axis=-1, keepdims=True) + L2_EPS)
            qkv_o[0, i, :, sk] = kn.astype(qkv_o.dtype)
        qkv_o[0, i, :, 2 * GDN_QK:] = act[:, 2 * GDN_QK:].astype(qkv_o.dtype)
        lg = lg_ref[0, i]
        lane = lax.broadcasted_iota(jnp.int32, lg.shape, 1)
        decay = ga_ref[...] * _softplus(lg + gd_ref[...])
        g_o[0, i] = jnp.where(lane < 2 * GDN_HEADS, decay, jnp.where(lane < N_GATES, _sigmoid(lg), 0.0))


def _gdn_prep(qkv, lg, conv_w, a_row, dt_row, gb):
    nc, nb, ch, n = qkv.shape
    hb = ch // 8
    return pl.pallas_call(
        _gdn_prep_body,
        grid=(nc, nb // gb),
        in_specs=[pl.BlockSpec((1, gb, ch, n), lambda c, b: (c, b, 0, 0)),
                  pl.BlockSpec((1, gb, 8, n), lambda c, b: (jnp.maximum(c - 1, 0), b, hb - 1, 0)),
                  pl.BlockSpec((1, gb, 8, n), lambda c, b: (jnp.minimum(c + 1, nc - 1), b, 0, 0)),
                  pl.BlockSpec((1, gb, ch, LANES), lambda c, b: (c, b, 0, 0)),
                  pl.BlockSpec((CONV_W, n), lambda c, b: (0, 0)),
                  pl.BlockSpec((1, LANES), lambda c, b: (0, 0)),
                  pl.BlockSpec((1, LANES), lambda c, b: (0, 0))],
        out_specs=[pl.BlockSpec((1, gb, ch, n), lambda c, b: (c, b, 0, 0)),
                   pl.BlockSpec((1, gb, ch, LANES), lambda c, b: (c, b, 0, 0))],
        out_shape=[jax.ShapeDtypeStruct(qkv.shape, BF16), jax.ShapeDtypeStruct(lg.shape, F32)],
        compiler_params=_cparams("parallel", "parallel"),
        name="gdn_prep",
    )(qkv, qkv, qkv, lg, conv_w, a_row, dt_row)


def _gdn_prep_lat_body(x_ref, prev_ref, next_ref, cw_ref, o_ref):
    ri = pl.program_id(1)
    nr = pl.num_programs(1)
    part = pl.program_id(2)
    gw = GRID_W
    main = x_ref[0]
    rows, n = main.shape
    prev = prev_ref[0]
    nxt = next_ref[0]
    zrow = jnp.zeros((1, n), F32)
    prev_wrap = jnp.concatenate([zrow, prev[:gw - 1]], axis=0)
    nxt_wrap = jnp.concatenate([nxt[1:gw], zrow, nxt[gw + 1:], zrow], axis=0)
    prev = jnp.where(ri == 0, prev_wrap, prev)
    nxt = jnp.where(ri == nr - 1, nxt_wrap, nxt)
    ext = jnp.concatenate([prev, main, nxt], axis=0)
    cwh = 0.5 * cw_ref[...]
    hu = (cwh[0:1, :] * ext[0:rows] + cwh[1:2, :] * ext[gw:gw + rows]
          + cwh[2:3, :] * ext[2 * gw:2 * gw + rows] + cwh[3:4, :] * ext[3 * gw:3 * gw + rows])
    act = hu * jnp.tanh(hu) + hu
    scale = jnp.where(part == 0, GDN_DK ** -0.5, 1.0)
    parts = []
    for h in range(n // GDN_DK):
        ah = act[:, h * GDN_DK:(h + 1) * GDN_DK]
        nh = ah * (lax.rsqrt(jnp.sum(ah * ah, axis=-1, keepdims=True) + L2_EPS) * scale)
        parts.append(jnp.where(part < 2, nh, ah))
    y = jnp.concatenate(parts, axis=1)
    o_ref[:, 0] = jnp.swapaxes(y.reshape(rows // gw, gw, n), 0, 1).astype(BF16)


def _gdn_prep_lat(qkv_raw, conv_w):
    nb, length, n3 = qkv_raw.shape
    gw = GRID_W
    rows = length // gw
    rr = 16
    n = GDN_QK
    assert n3 == 3 * n and rows % rr == 0
    return pl.pallas_call(
        _gdn_prep_lat_body,
        grid=(nb, rows // rr, 3),
        in_specs=[pl.BlockSpec((1, rr * gw, n), lambda b, ri, j: (b, ri, j)),
                  pl.BlockSpec((1, gw, n), lambda b, ri, j: (b, (ri * rr + rows - 1) % rows, j)),
                  pl.BlockSpec((1, 2 * gw, n), lambda b, ri, j: (b, (((ri + 1) * rr) % rows) // 2, j)),
                  pl.BlockSpec((CONV_W, n), lambda b, ri, j: (0, j))],
        out_specs=pl.BlockSpec((gw, 1, rr, n), lambda b, ri, j: (0, b, ri, j)),
        out_shape=jax.ShapeDtypeStruct((gw, nb, rows, n3), BF16),
        compiler_params=_cparams("parallel", "parallel", "parallel"),
        name="gdn_prep_lat",
    )(qkv_raw, qkv_raw, qkv_raw, conv_w)


def _gdn_gates_body(lg_ref, ga_ref, gd_ref, g_o):
    lg = lg_ref[...]
    lane = lax.broadcasted_iota(jnp.int32, lg.shape, lg.ndim - 1)
    decay = ga_ref[...] * _softplus(lg + gd_ref[...])
    g_o[...] = jnp.where(lane < 2 * GDN_HEADS, decay, jnp.where(lane < N_GATES, _sigmoid(lg), 0.0))


def _gdn_gates(lg, a_row, dt_row):
    nc, nb, ch, n = lg.shape
    cb = 8
    spec = pl.BlockSpec((cb, nb, ch, n), lambda c: (c, 0, 0, 0))
    vec = pl.BlockSpec((1, n), lambda c: (0, 0))
    return pl.pallas_call(
        _gdn_gates_body,
        grid=(nc // cb,),
        in_specs=[spec, vec, vec],
        out_specs=spec,
        out_shape=jax.ShapeDtypeStruct(lg.shape, F32),
        compiler_params=_cparams("parallel"),
        name="gdn_gates",
    )(lg, a_row, dt_row)


def _gdn_scan_body(qkv_ref, g_ref, s0_ref, *rest, reverse, emit_o):
    if emit_o:
        o_ref, sl_ref, s_s = rest
    else:
        sl_ref, s_s = rest
    step = pl.program_id(1)
    ch = GDN_CHUNK
    gb = qkv_ref.shape[1]
    n_sq = int(math.log2(ch)) - 1

    @pl.when(step == 0)
    def _():
        s_s[...] = s0_ref[...]

    row = lax.broadcasted_iota(jnp.int32, (ch, ch), 0)
    col = lax.broadcasted_iota(jnp.int32, (ch, ch), 1)
    incl = (row <= col) if reverse else (row >= col)
    strict = (row < col) if reverse else (row > col)
    eye = (row == col).astype(F32)
    tri = incl.astype(F32)
    goff = GDN_HEADS if reverse else 0
    boff = 2 * GDN_HEADS + goff

    units = [(i, h) for i in range(gb) for h in range(GDN_HEADS)]
    nu = len(units)
    gcum, gcum_t, gtot, gates = [], [], [], []
    for i in range(gb):
        g = g_ref[0, i]
        gates.append(g)
        cum = jnp.dot(tri, g, preferred_element_type=F32, precision=HI)
        gcum.append(cum)
        gcum_t.append(cum.T)
        gtot.append(jnp.sum(g, axis=0, keepdims=True))

    q, k, kb, xin, decay, eg, egl, gl = [], [], [], [], [], [], [], []
    for i, h in units:
        qh = qkv_ref[0, i, :, h * GDN_DK:(h + 1) * GDN_DK].astype(F32)
        kh = qkv_ref[0, i, :, GDN_QK + h * GDN_DK:GDN_QK + (h + 1) * GDN_DK].astype(F32)
        vh = qkv_ref[0, i, :, 2 * GDN_QK + h * GDN_DV:2 * GDN_QK + (h + 1) * GDN_DV].astype(F32)
        gc = jnp.broadcast_to(gcum[i][:, goff + h:goff + h + 1], (ch, GDN_DK))
        beta = jnp.broadcast_to(gates[i][:, boff + h:boff + h + 1], (ch, GDN_DK))
        gr = gcum_t[i][goff + h:goff + h + 1, :]
        gt = gtot[i][:, goff + h:goff + h + 1]
        decay.append(jnp.where(incl, jnp.exp(jnp.where(incl, gc[:, :ch] - gr, 0.0)), 0.0))
        e = jnp.exp(gc)
        kbh = kh * beta
        q.append(qh)
        k.append(kh)
        kb.append(kbh)
        eg.append(e)
        egl.append(jnp.exp(gt - gc))
        gl.append(jnp.exp(gt))
        xin.append(jnp.concatenate([kbh * e, vh * beta], axis=1).astype(BF16))

    gram = [_bdot_nt(jnp.concatenate([kb[j], q[j]], axis=0), k[j]) for j in range(nu)]
    nil = [-jnp.where(strict, gram[j][:ch] * decay[j], 0.0) for j in range(nu)]
    amat = [gram[j][ch:] * decay[j] for j in range(nu)]
    inv = [eye + m for m in nil]
    power = nil
    for _ in range(n_sq):
        pb = [p.astype(BF16) for p in power]
        power = [jnp.dot(p, p, preferred_element_type=F32) for p in pb]
        inv = [iv + _bdot(iv, p) for iv, p in zip(inv, power)]
    wu = [_bdot(iv, x) for iv, x in zip(inv, xin)]
    s_old = [s_s[i, h] for i, h in units]
    s_bf = [s.astype(BF16) for s in s_old]
    ws = [_bdot(jnp.concatenate([wu[j][:, :GDN_DK], q[j] * eg[j]], axis=0), s_bf[j]) for j in range(nu)]
    v_new = [(wu[j][:, GDN_DK:] - ws[j][:ch]).astype(BF16) for j in range(nu)]
    upd = [_bdot(jnp.concatenate([amat[j], (k[j] * egl[j]).T], axis=0), v_new[j]) for j in range(nu)]
    for j, (i, h) in enumerate(units):
        if emit_o:
            o_ref[0, i, :, h * GDN_DV:(h + 1) * GDN_DV] = (ws[j][ch:] + upd[j][:ch]).astype(o_ref.dtype)
        s_new = s_old[j] * gl[j] + upd[j][ch:]
        s_s[i, h] = s_new
        sl_ref[i, h] = s_new


def _gdn_direction(qkv, gates, s0, reverse, emit_o, gb=1):
    nc, nb, ch, n = qkv.shape

    def ci(c):
        return (nc - 1 - c) if reverse else c

    out_specs = []
    out_shape = []
    if emit_o:
        out_specs.append(pl.BlockSpec((1, gb, ch, GDN_VW), lambda b, c: (ci(c), b, 0, 0)))
        out_shape.append(jax.ShapeDtypeStruct((nc, nb, ch, GDN_VW), BF16))
    out_specs.append(pl.BlockSpec((gb, GDN_HEADS, GDN_DK, GDN_DV), lambda b, c: (b, 0, 0, 0)))
    out_shape.append(jax.ShapeDtypeStruct((nb, GDN_HEADS, GDN_DK, GDN_DV), F32))
    body = functools.partial(_gdn_scan_body, reverse=reverse, emit_o=emit_o)
    return pl.pallas_call(
        body,
        grid=(nb // gb, nc),
        in_specs=[pl.BlockSpec((1, gb, ch, n), lambda b, c: (ci(c), b, 0, 0)),
                  pl.BlockSpec((1, gb, ch, LANES), lambda b, c: (ci(c), b, 0, 0)),
                  pl.BlockSpec((gb, GDN_HEADS, GDN_DK, GDN_DV), lambda b, c: (b, 0, 0, 0))],
        out_specs=out_specs,
        out_shape=out_shape,
        scratch_shapes=[pltpu.VMEM((gb, GDN_HEADS, GDN_DK, GDN_DV), F32)],
        compiler_params=_cparams("parallel", "arbitrary"),
        name="gdn_scan_" + ("bwd" if reverse else "fwd"),
    )(qkv, gates, s0)


def _gdn_out_body(of_ref, ob_ref, z_ref, nw_ref, y_ref, y_s):
    nb = of_ref.shape[1]
    for i in range(nb):
        o = of_ref[0, i].astype(F32) + ob_ref[0, i].astype(F32)
        zg = _silu(z_ref[0, i])
        for h in range(GDN_HEADS):
            sl = slice(h * GDN_DV, (h + 1) * GDN_DV)
            oh = o[:, sl]
            nh = oh * lax.rsqrt(jnp.mean(oh * oh, axis=-1, keepdims=True) + LN_EPS) * nw_ref[...]
            y_s[i, :, sl] = nh * zg[:, sl]
    y_ref[:, 0] = jnp.swapaxes(y_s[...], 0, 1).astype(BF16)


def _gdn_out(o_f, o_b, z, norm_w):
    nc, nb, ch, vw = o_f.shape
    spec = pl.BlockSpec((1, nb, ch, vw), lambda c: (c, 0, 0, 0))
    y = pl.pallas_call(
        _gdn_out_body,
        grid=(nc,),
        in_specs=[spec, spec, spec, pl.BlockSpec((1, GDN_DV), lambda c: (0, 0))],
        out_specs=pl.BlockSpec((ch, 1, nb, vw), lambda c: (0, c, 0, 0)),
        out_shape=jax.ShapeDtypeStruct((ch, nc, nb, vw), BF16),
        scratch_shapes=[pltpu.VMEM((nb, ch, vw), F32)],
        compiler_params=_cparams("parallel"),
        name="gdn_out",
    )(o_f, o_b, z, norm_w.reshape(1, GDN_DV))
    return y.reshape(ch * nc * nb, vw)


def _mixer_body(ya_ref, yb_ref, mg_ref, x_ref, g1_ref, sh2_ref, sc2_ref, wpa_ref, wpb_ref, wo_ref,
                lg_ref, lb_ref, wrh_ref, wrl_ref, br_ref, x1_ref, h2_ref, rt_ref):
    nb, tt, d = x_ref.shape
    tm = nb * tt
    xs = jnp.swapaxes(x_ref[...], 0, 1).reshape(tm, d)
    pa = jnp.dot(ya_ref[...], wpa_ref[...], preferred_element_type=F32)
    pb = jnp.dot(yb_ref[...], wpb_ref[...], preferred_element_type=F32)
    merged = _sigmoid(mg_ref[:, :d]) * pa + _sigmoid(mg_ref[:, d:]) * pb
    mix = _bdot(merged, wo_ref[...])
    mix3 = mix.reshape(tm // nb, nb, d) * g1_ref[...][None]
    x1 = _ln(DN_ALPHA * xs + mix3.reshape(tm, d)) * lg_ref[...] + lb_ref[...]
    x1_ref[...] = x1
    h3 = _ln(x1).reshape(tm // nb, nb, d) * (1.0 + sc2_ref[...])[None] + sh2_ref[...][None]
    h2 = h3.reshape(tm, d)
    h2_hi = h2.astype(BF16)
    h2_ref[...] = h2_hi

    h2_lo = (h2 - h2_hi.astype(F32)).astype(BF16)
    logits = (jnp.dot(h2_hi, wrh_ref[...], preferred_element_type=F32)
              + jnp.dot(h2_lo, wrh_ref[...], preferred_element_type=F32)
              + jnp.dot(h2_hi, wrl_ref[...], preferred_element_type=F32) + br_ref[...])
    lane = lax.broadcasted_iota(jnp.int32, logits.shape, 1).astype(F32)
    neg = jnp.float32(-jnp.inf)
    big = jnp.float32(1 << 20)
    is_g = lane < N_GROUPS
    gl = jnp.where(is_g, logits, neg)
    gmax = jnp.max(gl, axis=-1, keepdims=True)
    gsum = jnp.sum(jnp.where(is_g, jnp.exp(gl - gmax), 0.0), axis=-1, keepdims=True)
    p_group = 1.0 / gsum
    g_idx = jnp.min(jnp.where(gl == gmax, lane, big), axis=-1, keepdims=True)
    lo = N_GROUPS + g_idx * EXP_PER_GROUP
    in_g = (lane >= lo) & (lane < lo + EXP_PER_GROUP)
    el = jnp.where(in_g, logits, neg)
    m1 = jnp.max(el, axis=-1, keepdims=True)
    i1 = jnp.min(jnp.where(el == m1, lane, big), axis=-1, keepdims=True)
    el2 = jnp.where(lane == i1, neg, el)
    m2 = jnp.max(el2, axis=-1, keepdims=True)
    i2 = jnp.min(jnp.where(el2 == m2, lane, big), axis=-1, keepdims=True)
    esum = jnp.sum(jnp.where(in_g, jnp.exp(el - m1), 0.0), axis=-1, keepdims=True)
    p1 = 1.0 / esum
    p2 = jnp.exp(m2 - m1) / esum
    w1 = p_group * p1 / (p1 + p2)
    w2 = p_group * p2 / (p1 + p2)
    e1 = i1 - N_GROUPS
    e2 = i2 - N_GROUPS
    rt_ref[...] = jnp.where(lane == 0, e1, jnp.where(lane == 1, e2, jnp.where(lane == 2, w1,
                            jnp.where(lane == 3, w2, 0.0))))


def _mixer(ya, yb, mg, x, g1, sh2, sc2, w_pa, w_pb, w_out, ln_g, ln_b, w_r_hi, w_r_lo, b_r, tt):
    nb, length, d = x.shape
    t = nb * length
    tm = tt * nb
    row = lambda n: pl.BlockSpec((tm, n), lambda i: (i, 0))
    full = lambda a: pl.BlockSpec(a.shape, lambda i: (0,) * a.ndim)
    args = [ya, yb, mg, x, g1, sh2, sc2, w_pa, w_pb, w_out, ln_g, ln_b, w_r_hi, w_r_lo, b_r]
    in_specs = ([row(d), row(d), row(2 * d), pl.BlockSpec((nb, tt, d), lambda i: (0, i, 0))]
                + [full(a) for a in args[4:]])
    body = _mixer_body
    return pl.pallas_call(
        body,
        grid=(t // tm,),
        in_specs=in_specs,
        out_specs=[row(d), row(d), row(LANES)],
        out_shape=[jax.ShapeDtypeStruct((t, d), F32), jax.ShapeDtypeStruct((t, d), BF16),
                   jax.ShapeDtypeStruct((t, LANES), F32)],
        compiler_params=_cparams("parallel"),
        name="mixer_router",
    )(*args)


def _experts_body(be_ref, nu_ref, x_ref, wg_ref, wu_ref, wd_ref, y_ref, wg_s, wu_s, wd_s):
    i = pl.program_id(0)
    used = i < nu_ref[0]
    new_expert = jnp.logical_or(i == 0, be_ref[i] != be_ref[jnp.maximum(i - 1, 0)])

    @pl.when(jnp.logical_and(used, new_expert))
    def _():
        wg_s[...] = wg_ref[0].astype(BF16)
        wu_s[...] = wu_ref[0].astype(BF16)
        wd_s[...] = wd_ref[0].astype(BF16)

    @pl.when(used)
    def _():
        x = x_ref[...]
        hg = jnp.dot(x, wg_s[...], preferred_element_type=F32)
        hu = jnp.dot(x, wu_s[...], preferred_element_type=F32)
        y_ref[...] = _bdot(_silu(hg) * hu, wd_s[...]).astype(y_ref.dtype)

    @pl.when(jnp.logical_not(used))
    def _():
        y_ref[...] = jnp.zeros_like(y_ref)


def _experts(x_pad, block_expert, n_used, w_gate, w_up, w_down, tm):
    rows, d = x_pad.shape
    de = w_gate.shape[2]
    nblk = rows // tm
    wspec = lambda k, n: pl.BlockSpec((1, k, n), lambda i, be, nu: (be[i], 0, 0))
    return pl.pallas_call(
        _experts_body,
        grid_spec=pltpu.PrefetchScalarGridSpec(
            num_scalar_prefetch=2,
            grid=(nblk,),
            in_specs=[pl.BlockSpec((tm, d), lambda i, be, nu: (i, 0)),
                      wspec(d, de), wspec(d, de), wspec(de, d)],
            out_specs=pl.BlockSpec((tm, d), lambda i, be, nu: (i, 0)),
            scratch_shapes=[pltpu.VMEM((d, de), BF16), pltpu.VMEM((d, de), BF16), pltpu.VMEM((de, d), BF16)],
        ),
        out_shape=jax.ShapeDtypeStruct((rows, d), BF16),
        compiler_params=_cparams("arbitrary"),
        name="experts",
    )(block_expert, n_used, x_pad, w_gate, w_up, w_down)


def _final_body(x1_ref, y1_ref, y2_ref, rt_ref, g2_ref, lg_ref, lb_ref, o_ref):
    tm, d = x1_ref.shape
    nb = g2_ref.shape[0]
    y = rt_ref[:, 2:3] * y1_ref[...].astype(F32) + rt_ref[:, 3:4] * y2_ref[...].astype(F32)
    gy = (y.reshape(tm // nb, nb, d) * g2_ref[...][None]).reshape(tm, d)
    out = _ln(DN_ALPHA * x1_ref[...] + gy) * lg_ref[...] + lb_ref[...]
    o_ref[...] = jnp.swapaxes(out.reshape(tm // nb, nb, d), 0, 1)


def _final(x1_tb, y1, y2, route, g2, ln_g, ln_b, tq):
    t, d = x1_tb.shape
    nb = g2.shape[0]
    length = t // nb
    tm = tq * nb
    row = lambda n: pl.BlockSpec((tm, n), lambda i: (i, 0))
    return pl.pallas_call(
        _final_body,
        grid=(length // tq,),
        in_specs=[row(d), row(d), row(d), row(LANES),
                  pl.BlockSpec((nb, d), lambda i: (0, 0)),
                  pl.BlockSpec((1, d), lambda i: (0, 0)),
                  pl.BlockSpec((1, d), lambda i: (0, 0))],
        out_specs=pl.BlockSpec((nb, tq, d), lambda i: (0, i, 0)),
        out_shape=jax.ShapeDtypeStruct((nb, length, d), F32),
        compiler_params=_cparams("parallel"),
        name="moe_combine_postnorm",
    )(x1_tb, y1, y2, route, g2, ln_g, ln_b)


MOE_TM = 512
GDN_GB = 4


def _route_plan(e1, e2, tm):
    t = e1.shape[0]
    e_flat = jnp.concatenate([e1, e2])
    n_assign = e_flat.shape[0]
    experts = jnp.arange(N_EXPERTS, dtype=jnp.int32)
    onehot = (e_flat[:, None] == experts[None, :]).astype(jnp.int32)
    csum = jnp.cumsum(onehot, axis=0)
    rank = jnp.sum(jnp.where(onehot > 0, csum - 1, 0), axis=1)
    counts = csum[-1]
    starts = jnp.cumsum(counts) - counts
    padded = (counts + tm - 1) // tm * tm
    pends = jnp.cumsum(padded)
    pstarts = pends - padded
    dest = pstarts[e_flat] + rank
    n_blocks = (n_assign + N_EXPERTS * (tm - 1) + tm - 1) // tm
    n_used = (pends[-1] // tm).astype(jnp.int32).reshape(1)
    blk0 = jnp.arange(n_blocks, dtype=jnp.int32) * tm
    block_expert = jnp.minimum(jnp.sum((pends[None, :] <= blk0[:, None]).astype(jnp.int32), axis=1),
                               N_EXPERTS - 1).astype(jnp.int32)
    order = jnp.argsort(e_flat, stable=True).astype(jnp.int32)
    p = jnp.arange(n_blocks * tm, dtype=jnp.int32)
    pe = jnp.repeat(block_expert, tm)
    r = p - pstarts[pe]
    valid = r < counts[pe]
    a_idx = jnp.clip(starts[pe] + r, 0, n_assign - 1)
    src = jnp.where(valid, order.at[a_idx].get(mode="promise_in_bounds") % t, p % t)
    return src, dest[:t], dest[t:], block_expert, n_used


def kernel(x, c, ctx, c_ctx, w_mod, b_mod, w_in, b_in, conv_a_w, conv_a_b, lru_wa, lru_ba, lru_wx, lru_bx,
           lru_lambda, conv_qkv_w, gdn_a_log, gdn_dt_bias, gdn_norm_w, w_pa, w_pb, w_out, ln1_g, ln1_b,
           w_router_g, b_router_g, w_router_e, b_router_e, w_e_gate, w_e_up, w_e_down, ln2_g, ln2_b):
    nb, n_lat, d = x.shape
    n_ctx = ctx.shape[1]
    rows = n_lat // GRID_W
    assert rows == GDN_CHUNK and n_ctx % GDN_CHUNK == 0 and w_mod.shape[0] == 1
    layer = 0
    lp = {"conv_a_w": conv_a_w[layer], "conv_a_b": conv_a_b[layer], "lru_wa": lru_wa[layer],
          "lru_ba": lru_ba[layer], "lru_wx": lru_wx[layer], "lru_bx": lru_bx[layer],
          "lru_lambda": lru_lambda[layer]}

    pad_rows = (-(nb + 1)) % 8
    cc = jnp.concatenate([c, c_ctx[None, :], jnp.zeros((pad_rows, d), F32)], axis=0)
    mod = _mod_vectors(cc, w_mod[layer], b_mod[layer])
    sh1, sc1, g1, sh2, sc2, g2 = [mod[:nb, j * d:(j + 1) * d] for j in range(6)]
    csh1 = jnp.broadcast_to(mod[nb:nb + 1, 0:d], (nb, d))
    csc1 = jnp.broadcast_to(mod[nb:nb + 1, d:2 * d], (nb, d))

    w_l = w_in[layer]
    b_l = b_in[layer]
    w_r = jnp.concatenate([w_l[:, OFF_XA:OFF_GDN], w_l[:, OFF_MG:]], axis=1).astype(BF16)
    b_r = jnp.concatenate([b_l[OFF_XA:OFF_GDN], b_l[OFF_MG:]])[None, :]
    gate_pad = LANES - N_GATES
    w_g = jnp.concatenate([w_l[:, OFF_GDN:OFF_GDN + QKV_COLS],
                           jnp.pad(w_l[:, OFF_GDN + QKV_COLS:OFF_Z], ((0, 0), (0, gate_pad))),
                           w_l[:, OFF_Z:OFF_MG]], axis=1).astype(BF16)
    b_g = jnp.concatenate([b_l[OFF_GDN:OFF_GDN + QKV_COLS],
                           jnp.pad(b_l[OFF_GDN + QKV_COLS:OFF_Z], (0, gate_pad)),
                           b_l[OFF_Z:OFF_MG]])[None, :]
    n_qg = QKV_COLS + LANES

    (xa_ctx,) = _inproj_raster(ctx, csc1, csh1, w_r[:, :LRU_WIDTH], b_r[:, :LRU_WIDTH],
                               [(0, LRU_WIDTH)], tt=32)
    zero_lru = jnp.zeros((nb, LRU_WIDTH), F32)
    (sa_f,) = _lru_direction(xa_ctx, lp, 0, zero_lru, False, "none", tt=32)
    (sa_b,) = _lru_direction(xa_ctx, lp, 1, zero_lru, True, "none", tt=32)

    nc_ctx = n_ctx // GDN_CHUNK
    qkv_c, lg_c = _inproj_column(ctx, lambda ci, g: (g, ci, 0), nc_ctx, csc1, csh1,
                                 w_g[:, :n_qg], b_g[:, :n_qg], [(0, QKV_COLS), (QKV_COLS, n_qg)])
    a_row = jnp.pad(-jnp.exp(gdn_a_log[layer].reshape(-1)), (0, LANES - 2 * GDN_HEADS))[None, :]
    dt_row = jnp.pad(gdn_dt_bias[layer].reshape(-1), (0, LANES - 2 * GDN_HEADS))[None, :]
    cw_qkv = conv_qkv_w[layer]
    qkv_c, gates_c = _gdn_prep(qkv_c, lg_c, cw_qkv, a_row, dt_row, GDN_GB)
    zero_gdn = jnp.zeros((nb, GDN_HEADS, GDN_DK, GDN_DV), F32)
    (sb_f,) = _gdn_direction(qkv_c, gates_c, zero_gdn, False, False, GDN_GB)
    (sb_b,) = _gdn_direction(qkv_c, gates_c, zero_gdn, True, False, GDN_GB)

    xa, ga, mg = _inproj_raster(x, sc1, sh1, w_r, b_r,
                                [(0, LRU_WIDTH), (LRU_WIDTH, 2 * LRU_WIDTH), (2 * LRU_WIDTH, 2 * LRU_WIDTH + 2 * d)],
                                tt=32)
    h_f, _ = _lru_direction(xa, lp, 0, sa_f, False, "h", tt=32)
    ya, _ = _lru_direction(xa, lp, 1, sa_b, True, "ya", tt=32, hf=h_f, ga=ga)

    qkv_raw, lg_l, z_l = _inproj_column_lat(x, sc1, sh1, w_g, b_g)
    qkv_l = _gdn_prep_lat(qkv_raw, cw_qkv)
    gates_l = _gdn_gates(lg_l, a_row, dt_row)
    o_f, _ = _gdn_direction(qkv_l, gates_l, sb_f, False, True, GDN_GB)
    o_b, _ = _gdn_direction(qkv_l, gates_l, sb_b, True, True, GDN_GB)
    yb = _gdn_out(o_f, o_b, z_l, gdn_norm_w[layer])

    w_rt = jnp.pad(jnp.concatenate([w_router_g[layer], w_router_e[layer]], axis=1),
                   ((0, 0), (0, LANES - N_GROUPS - N_EXPERTS)))
    b_rt = jnp.pad(jnp.concatenate([b_router_g[layer], b_router_e[layer]]),
                   (0, LANES - N_GROUPS - N_EXPERTS))[None, :]
    w_rt_hi = w_rt.astype(BF16)
    w_rt_lo = (w_rt - w_rt_hi.astype(F32)).astype(BF16)
    x1, h2, route = _mixer(ya, yb, mg, x, g1, sh2, sc2, w_pa[layer].astype(BF16), w_pb[layer].astype(BF16),
                           w_out[layer].astype(BF16), ln1_g[layer][None, :], ln1_b[layer][None, :],
                           w_rt_hi, w_rt_lo, b_rt, tt=32)

    e1 = route[:, 0].astype(jnp.int32)
    e2 = route[:, 1].astype(jnp.int32)
    src, dest1, dest2, block_expert, n_used = _route_plan(e1, e2, MOE_TM)
    x_pad = h2.at[src].get(mode="promise_in_bounds")
    y_pad = _experts(x_pad, block_expert, n_used, w_e_gate[layer], w_e_up[layer], w_e_down[layer], MOE_TM)
    y1 = y_pad.at[dest1].get(mode="promise_in_bounds")
    y2 = y_pad.at[dest2].get(mode="promise_in_bounds")
    return _final(x1, y1, y2, route, g2, ln2_g[layer][None, :], ln2_b[layer][None, :], tq=32)
```

```python
import functools
import math

import jax
import jax.numpy as jnp
from jax import lax
from jax.experimental import pallas as pl
from jax.experimental.pallas import tpu as pltpu

F32 = jnp.float32
BF16 = jnp.bfloat16

D_MODEL = 1024
GRID_W = 64
LRU_WIDTH = 1024
LRU_BLOCKS = 8
LRU_BLOCK = LRU_WIDTH // LRU_BLOCKS
LRU_C = 8.0
CONV_W = 4
GDN_HEADS = 8
GDN_DK = 128
GDN_DV = 128
GDN_QK = GDN_HEADS * GDN_DK
GDN_VW = GDN_HEADS * GDN_DV
GDN_CHUNK = 64
QKV_COLS = 2 * GDN_QK + GDN_VW
N_GATES = 4 * GDN_HEADS
OFF_XA = 0
OFF_GA = OFF_XA + LRU_WIDTH
OFF_GDN = OFF_GA + LRU_WIDTH
OFF_Z = OFF_GDN + QKV_COLS + N_GATES
OFF_MG = OFF_Z + GDN_VW
N_GROUPS = 4
EXP_PER_GROUP = 8
N_EXPERTS = N_GROUPS * EXP_PER_GROUP
TOP_K = 2
LN_EPS = 1e-6
L2_EPS = 1e-6
DEPTH = 1
DN_ALPHA = (2.0 * DEPTH) ** 0.25

LANES = 128
VMEM_LIMIT = 56 * 1024 * 1024

HI = lax.Precision.HIGHEST


def _cparams(*sem):
    return pltpu.CompilerParams(dimension_semantics=sem, vmem_limit_bytes=VMEM_LIMIT)


def _bdot(a, b):
    return jnp.dot(a.astype(BF16), b.astype(BF16), preferred_element_type=F32)


def _bdot_nt(a, b):
    return lax.dot_general(a.astype(BF16), b.astype(BF16), (((1,), (1,)), ((), ())),
                           preferred_element_type=F32)


def _ln(x):
    mu = jnp.mean(x, axis=-1, keepdims=True)
    xc = x - mu
    var = jnp.mean(xc * xc, axis=-1, keepdims=True)
    return xc * lax.rsqrt(var + LN_EPS)


def _sigmoid(x):
    return 0.5 * jnp.tanh(0.5 * x) + 0.5


def _silu(x):
    return x * _sigmoid(x)


def _softplus(x):
    return jnp.maximum(x, 0.0) + jnp.log(1.0 + jnp.exp(-jnp.abs(x)))


def _gelu_tanh(x):
    return 0.5 * x * (1.0 + jnp.tanh(math.sqrt(2.0 / math.pi) * (x + 0.044715 * (x * x * x))))


def _mod_body(c_ref, w_ref, b_ref, o_ref):
    o_ref[...] = jnp.dot(_silu(c_ref[...]), w_ref[...], preferred_element_type=F32,
                         precision=HI) + b_ref[...]


def _mod_vectors(cc, w_mod, b_mod):
    rows, d = cc.shape
    n = w_mod.shape[1]
    tn = 1536
    return pl.pallas_call(
        _mod_body,
        grid=(n // tn,),
        in_specs=[pl.BlockSpec((rows, d), lambda j: (0, 0)),
                  pl.BlockSpec((d, tn), lambda j: (0, j)),
                  pl.BlockSpec((1, tn), lambda j: (0, j))],
        out_specs=pl.BlockSpec((rows, tn), lambda j: (0, j)),
        out_shape=jax.ShapeDtypeStruct((rows, n), F32),
        compiler_params=_cparams("arbitrary"),
        name="mod_vectors",
    )(cc, w_mod, b_mod.reshape(1, n))


def _inproj_r_body(x_ref, sc_ref, sh_ref, w_ref, b_ref, *o_refs, splits):
    nb, tt, d = x_ref.shape
    tm = nb * tt
    xn = _ln(jnp.swapaxes(x_ref[...], 0, 1).reshape(tm, d)).reshape(tt, nb, d)
    xm = (xn * (1.0 + sc_ref[...])[None] + sh_ref[...][None]).reshape(tm, d).astype(BF16)
    for o_ref, (lo, hi) in zip(o_refs, splits):
        for n0 in range(lo, hi, 512):
            o_ref[:, n0 - lo:n0 - lo + 512] = (
                jnp.dot(xm, w_ref[:, n0:n0 + 512], preferred_element_type=F32) + b_ref[:, n0:n0 + 512])


def _inproj_raster(x, sc, sh, w, b, splits, tt):
    nb, length, d = x.shape
    n = w.shape[1]
    tm = tt * nb
    body = functools.partial(_inproj_r_body, splits=splits)
    return pl.pallas_call(
        body,
        grid=(length // tt,),
        in_specs=[pl.BlockSpec((nb, tt, d), lambda i: (0, i, 0)),
                  pl.BlockSpec((nb, d), lambda i: (0, 0)),
                  pl.BlockSpec((nb, d), lambda i: (0, 0)),
                  pl.BlockSpec((d, n), lambda i: (0, 0)),
                  pl.BlockSpec((1, n), lambda i: (0, 0))],
        out_specs=[pl.BlockSpec((tm, hi - lo), lambda i: (i, 0)) for lo, hi in splits],
        out_shape=[jax.ShapeDtypeStruct((length * nb, hi - lo), F32) for lo, hi in splits],
        compiler_params=_cparams("parallel"),
        name="inproj_raster",
    )(x, sc, sh, w, b)


def _inproj_g_body(x_ref, sc_ref, sh_ref, w_ref, b_ref, *o_refs, splits):
    gb, ch, d = x_ref.shape
    parts = []
    for i in range(gb):
        xn = _ln(x_ref[i])
        parts.append((xn * (1.0 + sc_ref[i:i + 1, :]) + sh_ref[i:i + 1, :]).astype(BF16))
    xm = jnp.concatenate(parts, axis=0)
    for o_ref, (lo, hi) in zip(o_refs, splits):
        step = 512 if (hi - lo) % 512 == 0 else hi - lo
        for n0 in range(lo, hi, step):
            res = jnp.dot(xm, w_ref[:, n0:n0 + step], preferred_element_type=F32) + b_ref[:, n0:n0 + step]
            for i in range(gb):
                o_ref[0, i, :, n0 - lo:n0 - lo + step] = res[i * ch:(i + 1) * ch]


def _inproj_column(xv, x_index_map, n_chunks, sc, sh, w, b, splits):
    nb, d = sc.shape
    gb = 8
    n = w.shape[1]
    body = functools.partial(_inproj_g_body, splits=splits)
    return pl.pallas_call(
        body,
        grid=(n_chunks, nb // gb),
        in_specs=[pl.BlockSpec((gb, GDN_CHUNK, d), x_index_map),
                  pl.BlockSpec((gb, d), lambda c, g: (g, 0)),
                  pl.BlockSpec((gb, d), lambda c, g: (g, 0)),
                  pl.BlockSpec((d, n), lambda c, g: (0, 0)),
                  pl.BlockSpec((1, n), lambda c, g: (0, 0))],
        out_specs=[pl.BlockSpec((1, gb, GDN_CHUNK, hi - lo), lambda c, g: (c, g, 0, 0)) for lo, hi in splits],
        out_shape=[jax.ShapeDtypeStruct((n_chunks, nb, GDN_CHUNK, hi - lo), F32) for lo, hi in splits],
        compiler_params=_cparams("parallel", "parallel"),
        name="inproj_column",
    )(xv, sc, sh, w, b)


def _inproj_gl_body(x_ref, sc_ref, sh_ref, w_ref, b_ref, qkv_ref, lg_ref, z_ref):
    _, tm, d = x_ref.shape
    rr = tm // GRID_W
    xm = _ln(x_ref[0]) * (1.0 + sc_ref[0]) + sh_ref[0]
    xb = xm.astype(BF16)
    for n0 in range(0, QKV_COLS, 512):
        qkv_ref[0, :, n0:n0 + 512] = (jnp.dot(xb, w_ref[:, n0:n0 + 512], preferred_element_type=F32)
                                      + b_ref[:, n0:n0 + 512])
    xs = jnp.swapaxes(xm.reshape(rr, GRID_W, d), 0, 1).reshape(tm, d).astype(BF16)
    n_qg = QKV_COLS + LANES
    lg = jnp.dot(xs, w_ref[:, QKV_COLS:n_qg], preferred_element_type=F32) + b_ref[:, QKV_COLS:n_qg]
    lg_ref[:, 0] = lg.reshape(GRID_W, rr, LANES)
    for n0 in range(0, GDN_VW, 512):
        res = (jnp.dot(xs, w_ref[:, n_qg + n0:n_qg + n0 + 512], preferred_element_type=F32)
               + b_ref[:, n_qg + n0:n_qg + n0 + 512])
        z_ref[:, 0, :, n0:n0 + 512] = res.reshape(GRID_W, rr, 512)


def _inproj_column_lat(x, sc, sh, w, b):
    nb, length, d = x.shape
    n = w.shape[1]
    rows = length // GRID_W
    rr = 8
    tm = rr * GRID_W
    vec = pl.BlockSpec((1, 1, d), lambda bi, ri: (bi, 0, 0))
    chunked = lambda m: pl.BlockSpec((GRID_W, 1, rr, m), lambda bi, ri: (0, bi, ri, 0))
    return pl.pallas_call(
        _inproj_gl_body,
        grid=(nb, rows // rr),
        in_specs=[pl.BlockSpec((1, tm, d), lambda bi, ri: (bi, ri, 0)),
                  vec, vec,
                  pl.BlockSpec((d, n), lambda bi, ri: (0, 0)),
                  pl.BlockSpec((1, n), lambda bi, ri: (0, 0))],
        out_specs=[pl.BlockSpec((1, tm, QKV_COLS), lambda bi, ri: (bi, ri, 0)), chunked(LANES), chunked(GDN_VW)],
        out_shape=[jax.ShapeDtypeStruct((nb, length, QKV_COLS), F32),
                   jax.ShapeDtypeStruct((GRID_W, nb, rows, LANES), F32),
                   jax.ShapeDtypeStruct((GRID_W, nb, rows, GDN_VW), F32)],
        compiler_params=_cparams("parallel", "parallel"),
        name="inproj_column_lat",
    )(x, sc.reshape(nb, 1, d), sh.reshape(nb, 1, d), w, b)


def _lru_body(*refs, tt, nb, reverse, emit):
    if emit == "ya":
        (xa_ref, prev_ref, next_ref, cw_ref, cb_ref, wa_ref, ba_ref, wx_ref, bx_ref, lam_ref, h0_ref,
         hf_ref, ga_ref, out_ref, hl_ref, a_s, b_s, h_s) = refs
    elif emit == "h":
        (xa_ref, prev_ref, next_ref, cw_ref, cb_ref, wa_ref, ba_ref, wx_ref, bx_ref, lam_ref, h0_ref,
         out_ref, hl_ref, a_s, b_s, h_s) = refs
    else:
        (xa_ref, prev_ref, next_ref, cw_ref, cb_ref, wa_ref, ba_ref, wx_ref, bx_ref, lam_ref, h0_ref,
         hl_ref, a_s, b_s, h_s) = refs
    step = pl.program_id(0)
    nsteps = pl.num_programs(0)
    blk = (nsteps - 1 - step) if reverse else step
    rows = tt * nb

    @pl.when(step == 0)
    def _():
        h_s[...] = h0_ref[...]

    prev = jnp.where(blk == 0, 0.0, prev_ref[...])
    nxt = jnp.where(blk == nsteps - 1, 0.0, next_ref[...])
    ext = jnp.concatenate([prev, xa_ref[...], nxt], axis=0)
    u = (cw_ref[0:1, :] * ext[0:rows] + cw_ref[1:2, :] * ext[nb:rows + nb]
         + cw_ref[2:3, :] * ext[2 * nb:rows + 2 * nb] + cw_ref[3:4, :] * ext[3 * nb:rows + 3 * nb]
         + cb_ref[...])
    c_all = (-0.5 * LRU_C) * _softplus(-lam_ref[...])
    for n in range(LRU_BLOCKS):
        sl = slice(n * LRU_BLOCK, (n + 1) * LRU_BLOCK)
        un = u[:, sl]
        ub = un.astype(BF16)
        t_r = jnp.tanh(jnp.dot(ub, wa_ref[n], preferred_element_type=F32) + ba_ref[:, sl])
        t_i = jnp.tanh(jnp.dot(ub, wx_ref[n], preferred_element_type=F32) + bx_ref[:, sl])
        c = c_all[:, sl]
        a = jnp.exp(c * t_r + c)
        hun = 0.5 * un
        a_s[:, sl] = a
        b_s[:, sl] = jnp.sqrt(1.0 - a * a) * (t_i * hun + hun)

    def scan_step(j, h):
        t = (tt - 1 - j) if reverse else j
        r0 = pl.multiple_of(t * nb, nb)
        h = a_s[pl.ds(r0, nb), :] * h + b_s[pl.ds(r0, nb), :]
        if emit == "h":
            out_ref[pl.ds(r0, nb), :] = h
        elif emit == "ya":
            b_s[pl.ds(r0, nb), :] = h
        return h

    h = lax.fori_loop(0, tt, scan_step, h_s[...], unroll=4)
    h_s[...] = h
    hl_ref[...] = h
    if emit == "ya":
        out_ref[...] = (_gelu_tanh(ga_ref[...]) * (hf_ref[...] + b_s[...])).astype(BF16)


def _lru_direction(xa, lp, di, h0, reverse, emit, tt, hf=None, ga=None):
    t, w = xa.shape
    nb = h0.shape[0]
    rows = tt * nb
    nblk = t // rows
    assert rows % (2 * nb) == 0
    prev_per = rows // nb
    next_per = rows // (2 * nb)
    n_prev = t // nb
    n_next = t // (2 * nb)

    def bi(i):
        return (nblk - 1 - i) if reverse else i

    const2 = lambda i: (0, 0)
    in_specs = [
        pl.BlockSpec((rows, w), lambda i: (bi(i), 0)),
        pl.BlockSpec((nb, w), lambda i: (jnp.maximum(bi(i) * prev_per - 1, 0), 0)),
        pl.BlockSpec((2 * nb, w), lambda i: (jnp.minimum((bi(i) + 1) * next_per, n_next - 1), 0)),
        pl.BlockSpec((CONV_W, w), const2),
        pl.BlockSpec((1, w), const2),
        pl.BlockSpec((LRU_BLOCKS, LRU_BLOCK, LRU_BLOCK), lambda i: (0, 0, 0)),
        pl.BlockSpec((1, w), const2),
        pl.BlockSpec((LRU_BLOCKS, LRU_BLOCK, LRU_BLOCK), lambda i: (0, 0, 0)),
        pl.BlockSpec((1, w), const2),
        pl.BlockSpec((1, w), const2),
        pl.BlockSpec((nb, w), const2),
    ]
    args = [xa, xa, xa, lp["conv_a_w"], lp["conv_a_b"].reshape(1, w),
            (0.5 * lp["lru_wa"][di]).astype(BF16), (0.5 * lp["lru_ba"][di]).reshape(1, w),
            (0.5 * lp["lru_wx"][di]).astype(BF16), (0.5 * lp["lru_bx"][di]).reshape(1, w),
            lp["lru_lambda"][di].reshape(1, w), h0]
    out_specs = []
    out_shape = []
    if emit == "ya":
        in_specs += [pl.BlockSpec((rows, w), lambda i: (bi(i), 0)),
                     pl.BlockSpec((rows, w), lambda i: (bi(i), 0))]
        args += [hf, ga]
        out_specs.append(pl.BlockSpec((rows, w), lambda i: (bi(i), 0)))
        out_shape.append(jax.ShapeDtypeStruct((t, w), BF16))
    elif emit == "h":
        out_specs.append(pl.BlockSpec((rows, w), lambda i: (bi(i), 0)))
        out_shape.append(jax.ShapeDtypeStruct((t, w), F32))
    out_specs.append(pl.BlockSpec((nb, w), const2))
    out_shape.append(jax.ShapeDtypeStruct((nb, w), F32))
    body = functools.partial(_lru_body, tt=tt, nb=nb, reverse=reverse, emit=emit)
    return pl.pallas_call(
        body,
        grid=(nblk,),
        in_specs=in_specs,
        out_specs=out_specs,
        out_shape=out_shape,
        scratch_shapes=[pltpu.VMEM((rows, w), F32), pltpu.VMEM((rows, w), F32), pltpu.VMEM((nb, w), F32)],
        compiler_params=_cparams("arbitrary"),
        name="lru_" + ("bwd" if reverse else "fwd") + "_" + emit,
    )(*args)


def _gdn_prep_body(qkv_ref, prev_ref, next_ref, lg_ref, cw_ref, ga_ref, gd_ref, qkv_o, g_o):
    c = pl.program_id(0)
    nc = pl.num_programs(0)
    ch = GDN_CHUNK
    gb = qkv_ref.shape[1]
    for i in range(gb):
        prev = jnp.where(c == 0, 0.0, prev_ref[0, i])
        nxt = jnp.where(c == nc - 1, 0.0, next_ref[0, i])
        ext = jnp.concatenate([prev, qkv_ref[0, i], nxt], axis=0)
        u = (cw_ref[0:1, :] * ext[7:7 + ch] + cw_ref[1:2, :] * ext[8:8 + ch]
             + cw_ref[2:3, :] * ext[9:9 + ch] + cw_ref[3:4, :] * ext[10:10 + ch])
        act = _silu(u)
        for h in range(GDN_HEADS):
            sq = slice(h * GDN_DK, (h + 1) * GDN_DK)
            qh = act[:, sq]
            qn = qh * lax.rsqrt(jnp.sum(qh * qh, axis=-1, keepdims=True) + L2_EPS) * (GDN_DK ** -0.5)
            qkv_o[0, i, :, sq] = qn.astype(qkv_o.dtype)
            sk = slice(GDN_QK + h * GDN_DK, GDN_QK + (h + 1) * GDN_DK)
            kh = act[:, sk]
            kn = kh * lax.rsqrt(jnp.sum(kh * kh, axis=-1, keepdims=True) + L2_EPS)
            qkv_o[0, i, :, sk] = kn.astype(qkv_o.dtype)
        qkv_o[0, i, :, 2 * GDN_QK:] = act[:, 2 * GDN_QK:].astype(qkv_o.dtype)
        lg = lg_ref[0, i]
        lane = lax.broadcasted_iota(jnp.int32, lg.shape, 1)
        decay = ga_ref[...] * _softplus(lg + gd_ref[...])
        g_o[0, i] = jnp.where(lane < 2 * GDN_HEADS, decay, jnp.where(lane < N_GATES, _sigmoid(lg), 0.0))


def _gdn_prep(qkv, lg, conv_w, a_row, dt_row, gb):
    nc, nb, ch, n = qkv.shape
    hb = ch // 8
    return pl.pallas_call(
        _gdn_prep_body,
        grid=(nc, nb // gb),
        in_specs=[pl.BlockSpec((1, gb, ch, n), lambda c, b: (c, b, 0, 0)),
                  pl.BlockSpec((1, gb, 8, n), lambda c, b: (jnp.maximum(c - 1, 0), b, hb - 1, 0)),
                  pl.BlockSpec((1, gb, 8, n), lambda c, b: (jnp.minimum(c + 1, nc - 1), b, 0, 0)),
                  pl.BlockSpec((1, gb, ch, LANES), lambda c, b: (c, b, 0, 0)),
                  pl.BlockSpec((CONV_W, n), lambda c, b: (0, 0)),
                  pl.BlockSpec((1, LANES), lambda c, b: (0, 0)),
                  pl.BlockSpec((1, LANES), lambda c, b: (0, 0))],
        out_specs=[pl.BlockSpec((1, gb, ch, n), lambda c, b: (c, b, 0, 0)),
                   pl.BlockSpec((1, gb, ch, LANES), lambda c, b: (c, b, 0, 0))],
        out_shape=[jax.ShapeDtypeStruct(qkv.shape, BF16), jax.ShapeDtypeStruct(lg.shape, F32)],
        compiler_params=_cparams("parallel", "parallel"),
        name="gdn_prep",
    )(qkv, qkv, qkv, lg, conv_w, a_row, dt_row)


def _gdn_prep_lat_body(x_ref, prev_ref, next_ref, cw_ref, o_ref):
    ri = pl.program_id(1)
    nr = pl.num_programs(1)
    part = pl.program_id(2)
    gw = GRID_W
    main = x_ref[0]
    rows, n = main.shape
    prev = prev_ref[0]
    nxt = next_ref[0]
    zrow = jnp.zeros((1, n), F32)
    prev_wrap = jnp.concatenate([zrow, prev[:gw - 1]], axis=0)
    nxt_wrap = jnp.concatenate([nxt[1:gw], zrow, nxt[gw + 1:], zrow], axis=0)
    prev = jnp.where(ri == 0, prev_wrap, prev)
    nxt = jnp.where(ri == nr - 1, nxt_wrap, nxt)
    ext = jnp.concatenate([prev, main, nxt], axis=0)
    cwh = 0.5 * cw_ref[...]
    hu = (cwh[0:1, :] * ext[0:rows] + cwh[1:2, :] * ext[gw:gw + rows]
          + cwh[2:3, :] * ext[2 * gw:2 * gw + rows] + cwh[3:4, :] * ext[3 * gw:3 * gw + rows])
    act = hu * jnp.tanh(hu) + hu
    scale = jnp.where(part == 0, GDN_DK ** -0.5, 1.0)
    parts = []
    for h in range(n // GDN_DK):
        ah = act[:, h * GDN_DK:(h + 1) * GDN_DK]
        nh = ah * (lax.rsqrt(jnp.sum(ah * ah, axis=-1, keepdims=True) + L2_EPS) * scale)
        parts.append(jnp.where(part < 2, nh, ah))
    y = jnp.concatenate(parts, axis=1)
    o_ref[:, 0] = jnp.swapaxes(y.reshape(rows // gw, gw, n), 0, 1).astype(BF16)


def _gdn_prep_lat(qkv_raw, conv_w):
    nb, length, n3 = qkv_raw.shape
    gw = GRID_W
    rows = length // gw
    rr = 16
    n = GDN_QK
    assert n3 == 3 * n and rows % rr == 0
    return pl.pallas_call(
        _gdn_prep_lat_body,
        grid=(nb, rows // rr, 3),
        in_specs=[pl.BlockSpec((1, rr * gw, n), lambda b, ri, j: (b, ri, j)),
                  pl.BlockSpec((1, gw, n), lambda b, ri, j: (b, (ri * rr + rows - 1) % rows, j)),
                  pl.BlockSpec((1, 2 * gw, n), lambda b, ri, j: (b, (((ri + 1) * rr) % rows) // 2, j)),
                  pl.BlockSpec((CONV_W, n), lambda b, ri, j: (0, j))],
        out_specs=pl.BlockSpec((gw, 1, rr, n), lambda b, ri, j: (0, b, ri, j)),
        out_shape=jax.ShapeDtypeStruct((gw, nb, rows, n3), BF16),
        compiler_params=_cparams("parallel", "parallel", "parallel"),
        name="gdn_prep_lat",
    )(qkv_raw, qkv_raw, qkv_raw, conv_w)


def _gdn_gates_body(lg_ref, ga_ref, gd_ref, g_o):
    lg = lg_ref[...]
    lane = lax.broadcasted_iota(jnp.int32, lg.shape, lg.ndim - 1)
    decay = ga_ref[...] * _softplus(lg + gd_ref[...])
    g_o[...] = jnp.where(lane < 2 * GDN_HEADS, decay, jnp.where(lane < N_GATES, _sigmoid(lg), 0.0))


def _gdn_gates(lg, a_row, dt_row):
    nc, nb, ch, n = lg.shape
    cb = 8
    spec = pl.BlockSpec((cb, nb, ch, n), lambda c: (c, 0, 0, 0))
    vec = pl.BlockSpec((1, n), lambda c: (0, 0))
    return pl.pallas_call(
        _gdn_gates_body,
        grid=(nc // cb,),
        in_specs=[spec, vec, vec],
        out_specs=spec,
        out_shape=jax.ShapeDtypeStruct(lg.shape, F32),
        compiler_params=_cparams("parallel"),
        name="gdn_gates",
    )(lg, a_row, dt_row)


def _gdn_scan_body(qkv_ref, g_ref, s0_ref, *rest, reverse, emit_o):
    if emit_o:
        o_ref, sl_ref, s_s = rest
    else:
        sl_ref, s_s = rest
    step = pl.program_id(1)
    ch = GDN_CHUNK
    gb = qkv_ref.shape[1]
    n_sq = int(math.log2(ch)) - 1

    @pl.when(step == 0)
    def _():
        s_s[...] = s0_ref[...]

    row = lax.broadcasted_iota(jnp.int32, (ch, ch), 0)
    col = lax.broadcasted_iota(jnp.int32, (ch, ch), 1)
    incl = (row <= col) if reverse else (row >= col)
    strict = (row < col) if reverse else (row > col)
    eye = (row == col).astype(F32)
    tri = incl.astype(F32)
    goff = GDN_HEADS if reverse else 0
    boff = 2 * GDN_HEADS + goff

    units = [(i, h) for i in range(gb) for h in range(GDN_HEADS)]
    nu = len(units)
    gcum, gcum_t, gtot, gates = [], [], [], []
    for i in range(gb):
        g = g_ref[0, i]
        gates.append(g)
        cum = jnp.dot(tri, g, preferred_element_type=F32, precision=HI)
        gcum.append(cum)
        gcum_t.append(cum.T)
        gtot.append(jnp.sum(g, axis=0, keepdims=True))

    q, k, kb, xin, decay, eg, egl, gl = [], [], [], [], [], [], [], []
    for i, h in units:
        qh = qkv_ref[0, i, :, h * GDN_DK:(h + 1) * GDN_DK].astype(F32)
        kh = qkv_ref[0, i, :, GDN_QK + h * GDN_DK:GDN_QK + (h + 1) * GDN_DK].astype(F32)
        vh = qkv_ref[0, i, :, 2 * GDN_QK + h * GDN_DV:2 * GDN_QK + (h + 1) * GDN_DV].astype(F32)
        gc = jnp.broadcast_to(gcum[i][:, goff + h:goff + h + 1], (ch, GDN_DK))
        beta = jnp.broadcast_to(gates[i][:, boff + h:boff + h + 1], (ch, GDN_DK))
        gr = gcum_t[i][goff + h:goff + h + 1, :]
        gt = gtot[i][:, goff + h:goff + h + 1]
        decay.append(jnp.where(incl, jnp.exp(jnp.where(incl, gc[:, :ch] - gr, 0.0)), 0.0))
        e = jnp.exp(gc)
        kbh = kh * beta
        q.append(qh)
        k.append(kh)
        kb.append(kbh)
        eg.append(e)
        egl.append(jnp.exp(gt - gc))
        gl.append(jnp.exp(gt))
        xin.append(jnp.concatenate([kbh * e, vh * beta], axis=1).astype(BF16))

    gram = [_bdot_nt(jnp.concatenate([kb[j], q[j]], axis=0), k[j]) for j in range(nu)]
    nil = [-jnp.where(strict, gram[j][:ch] * decay[j], 0.0) for j in range(nu)]
    amat = [gram[j][ch:] * decay[j] for j in range(nu)]
    inv = [eye + m for m in nil]
    power = nil
    for _ in range(n_sq):
        pb = [p.astype(BF16) for p in power]
        power = [jnp.dot(p, p, preferred_element_type=F32) for p in pb]
        inv = [iv + _bdot(iv, p) for iv, p in zip(inv, power)]
    wu = [_bdot(iv, x) for iv, x in zip(inv, xin)]
    s_old = [s_s[i, h] for i, h in units]
    s_bf = [s.astype(BF16) for s in s_old]
    ws = [_bdot(jnp.concatenate([wu[j][:, :GDN_DK], q[j] * eg[j]], axis=0), s_bf[j]) for j in range(nu)]
    v_new = [(wu[j][:, GDN_DK:] - ws[j][:ch]).astype(BF16) for j in range(nu)]
    upd = [_bdot(jnp.concatenate([amat[j], (k[j] * egl[j]).T], axis=0), v_new[j]) for j in range(nu)]
    for j, (i, h) in enumerate(units):
        if emit_o:
            o_ref[0, i, :, h * GDN_DV:(h + 1) * GDN_DV] = (ws[j][ch:] + upd[j][:ch]).astype(o_ref.dtype)
        s_new = s_old[j] * gl[j] + upd[j][ch:]
        s_s[i, h] = s_new
        sl_ref[i, h] = s_new


def _gdn_direction(qkv, gates, s0, reverse, emit_o, gb=1):
    nc, nb, ch, n = qkv.shape

    def ci(c):
        return (nc - 1 - c) if reverse else c

    out_specs = []
    out_shape = []
    if emit_o:
        out_specs.append(pl.BlockSpec((1, gb, ch, GDN_VW), lambda b, c: (ci(c), b, 0, 0)))
        out_shape.append(jax.ShapeDtypeStruct((nc, nb, ch, GDN_VW), BF16))
    out_specs.append(pl.BlockSpec((gb, GDN_HEADS, GDN_DK, GDN_DV), lambda b, c: (b, 0, 0, 0)))
    out_shape.append(jax.ShapeDtypeStruct((nb, GDN_HEADS, GDN_DK, GDN_DV), F32))
    body = functools.partial(_gdn_scan_body, reverse=reverse, emit_o=emit_o)
    return pl.pallas_call(
        body,
        grid=(nb // gb, nc),
        in_specs=[pl.BlockSpec((1, gb, ch, n), lambda b, c: (ci(c), b, 0, 0)),
                  pl.BlockSpec((1, gb, ch, LANES), lambda b, c: (ci(c), b, 0, 0)),
                  pl.BlockSpec((gb, GDN_HEADS, GDN_DK, GDN_DV), lambda b, c: (b, 0, 0, 0))],
        out_specs=out_specs,
        out_shape=out_shape,
        scratch_shapes=[pltpu.VMEM((gb, GDN_HEADS, GDN_DK, GDN_DV), F32)],
        compiler_params=_cparams("parallel", "arbitrary"),
        name="gdn_scan_" + ("bwd" if reverse else "fwd"),
    )(qkv, gates, s0)


def _gdn_out_body(of_ref, ob_ref, z_ref, nw_ref, y_ref, y_s):
    nb = of_ref.shape[1]
    for i in range(nb):
        o = of_ref[0, i].astype(F32) + ob_ref[0, i].astype(F32)
        zg = _silu(z_ref[0, i])
        for h in range(GDN_HEADS):
            sl = slice(h * GDN_DV, (h + 1) * GDN_DV)
            oh = o[:, sl]
            nh = oh * lax.rsqrt(jnp.mean(oh * oh, axis=-1, keepdims=True) + LN_EPS) * nw_ref[...]
            y_s[i, :, sl] = nh * zg[:, sl]
    y_ref[:, 0] = jnp.swapaxes(y_s[...], 0, 1).astype(BF16)


def _gdn_out(o_f, o_b, z, norm_w):
    nc, nb, ch, vw = o_f.shape
    spec = pl.BlockSpec((1, nb, ch, vw), lambda c: (c, 0, 0, 0))
    y = pl.pallas_call(
        _gdn_out_body,
        grid=(nc,),
        in_specs=[spec, spec, spec, pl.BlockSpec((1, GDN_DV), lambda c: (0, 0))],
        out_specs=pl.BlockSpec((ch, 1, nb, vw), lambda c: (0, c, 0, 0)),
        out_shape=jax.ShapeDtypeStruct((ch, nc, nb, vw), BF16),
        scratch_shapes=[pltpu.VMEM((nb, ch, vw), F32)],
        compiler_params=_cparams("parallel"),
        name="gdn_out",
    )(o_f, o_b, z, norm_w.reshape(1, GDN_DV))
    return y.reshape(ch * nc * nb, vw)


def _mixer_body(ya_ref, yb_ref, mg_ref, x_ref, g1_ref, sh2_ref, sc2_ref, wpa_ref, wpb_ref, wo_ref,
                lg_ref, lb_ref, wrh_ref, wrl_ref, br_ref, x1_ref, h2_ref, rt_ref, *, n_sub):
    nb, tt, d = x_ref.shape
    ts = tt // n_sub
    rs = ts * nb
    subs = [slice(k * rs, (k + 1) * rs) for k in range(n_sub)]
    xs = [jnp.swapaxes(x_ref[:, k * ts:(k + 1) * ts, :], 0, 1).reshape(rs, d) for k in range(n_sub)]
    pa = [jnp.dot(ya_ref[s, :], wpa_ref[...], preferred_element_type=F32) for s in subs]
    pb = [jnp.dot(yb_ref[s, :], wpb_ref[...], preferred_element_type=F32) for s in subs]
    merged = [(_sigmoid(mg_ref[s, :d]) * a + _sigmoid(mg_ref[s, d:]) * b).astype(BF16)
              for s, a, b in zip(subs, pa, pb)]
    mix = [jnp.dot(m, wo_ref[...], preferred_element_type=F32) for m in merged]
    x1 = []
    for s, xk, mk in zip(subs, xs, mix):
        gm = (mk.reshape(ts, nb, d) * g1_ref[...][None]).reshape(rs, d)
        x1k = _ln(DN_ALPHA * xk + gm) * lg_ref[...] + lb_ref[...]
        x1_ref[s, :] = x1k
        x1.append(x1k)
    h2 = [(_ln(v).reshape(ts, nb, d) * (1.0 + sc2_ref[...])[None] + sh2_ref[...][None]).reshape(rs, d) for v in x1]
    h2_hi = [v.astype(BF16) for v in h2]
    for s, v in zip(subs, h2_hi):
        h2_ref[s, :] = v
    h2_lo = [(v - hi.astype(F32)).astype(BF16) for v, hi in zip(h2, h2_hi)]
    all_logits = [(jnp.dot(hi, wrh_ref[...], preferred_element_type=F32)
                   + jnp.dot(lo, wrh_ref[...], preferred_element_type=F32)
                   + jnp.dot(hi, wrl_ref[...], preferred_element_type=F32) + br_ref[...])
                  for hi, lo in zip(h2_hi, h2_lo)]
    for s, logits in zip(subs, all_logits):
        rt_ref[s, :] = _route(logits)


def _route(logits):
    lane = lax.broadcasted_iota(jnp.int32, logits.shape, 1).astype(F32)
    neg = jnp.float32(-jnp.inf)
    big = jnp.float32(1 << 20)
    is_g = lane < N_GROUPS
    gl = jnp.where(is_g, logits, neg)
    gmax = jnp.max(gl, axis=-1, keepdims=True)
    gsum = jnp.sum(jnp.where(is_g, jnp.exp(gl - gmax), 0.0), axis=-1, keepdims=True)
    p_group = 1.0 / gsum
    g_idx = jnp.min(jnp.where(gl == gmax, lane, big), axis=-1, keepdims=True)
    lo = N_GROUPS + g_idx * EXP_PER_GROUP
    in_g = (lane >= lo) & (lane < lo + EXP_PER_GROUP)
    el = jnp.where(in_g, logits, neg)
    m1 = jnp.max(el, axis=-1, keepdims=True)
    i1 = jnp.min(jnp.where(el == m1, lane, big), axis=-1, keepdims=True)
    el2 = jnp.where(lane == i1, neg, el)
    m2 = jnp.max(el2, axis=-1, keepdims=True)
    i2 = jnp.min(jnp.where(el2 == m2, lane, big), axis=-1, keepdims=True)
    esum = jnp.sum(jnp.where(in_g, jnp.exp(el - m1), 0.0), axis=-1, keepdims=True)
    p1 = 1.0 / esum
    p2 = jnp.exp(m2 - m1) / esum
    w1 = p_group * p1 / (p1 + p2)
    w2 = p_group * p2 / (p1 + p2)
    e1 = i1 - N_GROUPS
    e2 = i2 - N_GROUPS
    return jnp.where(lane == 0, e1, jnp.where(lane == 1, e2, jnp.where(lane == 2, w1,
                     jnp.where(lane == 3, w2, 0.0))))


def _mixer(ya, yb, mg, x, g1, sh2, sc2, w_pa, w_pb, w_out, ln_g, ln_b, w_r_hi, w_r_lo, b_r, tt):
    nb, length, d = x.shape
    t = nb * length
    tm = tt * nb
    row = lambda n: pl.BlockSpec((tm, n), lambda i: (i, 0))
    full = lambda a: pl.BlockSpec(a.shape, lambda i: (0,) * a.ndim)
    args = [ya, yb, mg, x, g1, sh2, sc2, w_pa, w_pb, w_out, ln_g, ln_b, w_r_hi, w_r_lo, b_r]
    in_specs = ([row(d), row(d), row(2 * d), pl.BlockSpec((nb, tt, d), lambda i: (0, i, 0))]
                + [full(a) for a in args[4:]])
    body = functools.partial(_mixer_body, n_sub=MIXER_SUB)
    return pl.pallas_call(
        body,
        grid=(t // tm,),
        in_specs=in_specs,
        out_specs=[row(d), row(d), row(LANES)],
        out_shape=[jax.ShapeDtypeStruct((t, d), F32), jax.ShapeDtypeStruct((t, d), BF16),
                   jax.ShapeDtypeStruct((t, LANES), F32)],
        compiler_params=_cparams("parallel"),
        name="mixer_router",
    )(*args)


def _experts_body(be_ref, nu_ref, x_ref, wg_ref, wu_ref, wd_ref, y_ref, wg_s, wu_s, wd_s):
    i = pl.program_id(0)
    used = i < nu_ref[0]
    new_expert = jnp.logical_or(i == 0, be_ref[i] != be_ref[jnp.maximum(i - 1, 0)])

    @pl.when(jnp.logical_and(used, new_expert))
    def _():
        wg_s[...] = wg_ref[0].astype(BF16)
        wu_s[...] = wu_ref[0].astype(BF16)
        wd_s[...] = wd_ref[0].astype(BF16)

    @pl.when(used)
    def _():
        x = x_ref[...]
        hg = jnp.dot(x, wg_s[...], preferred_element_type=F32)
        hu = jnp.dot(x, wu_s[...], preferred_element_type=F32)
        y_ref[...] = _bdot(_silu(hg) * hu, wd_s[...]).astype(y_ref.dtype)

    @pl.when(jnp.logical_not(used))
    def _():
        y_ref[...] = jnp.zeros_like(y_ref)


def _experts(x_pad, block_expert, n_used, w_gate, w_up, w_down, tm):
    rows, d = x_pad.shape
    de = w_gate.shape[2]
    nblk = rows // tm
    wspec = lambda k, n: pl.BlockSpec((1, k, n), lambda i, be, nu: (be[i], 0, 0))
    return pl.pallas_call(
        _experts_body,
        grid_spec=pltpu.PrefetchScalarGridSpec(
            num_scalar_prefetch=2,
            grid=(nblk,),
            in_specs=[pl.BlockSpec((tm, d), lambda i, be, nu: (i, 0)),
                      wspec(d, de), wspec(d, de), wspec(de, d)],
            out_specs=pl.BlockSpec((tm, d), lambda i, be, nu: (i, 0)),
            scratch_shapes=[pltpu.VMEM((d, de), BF16), pltpu.VMEM((d, de), BF16), pltpu.VMEM((de, d), BF16)],
        ),
        out_shape=jax.ShapeDtypeStruct((rows, d), BF16),
        compiler_params=_cparams("arbitrary"),
        name="experts",
    )(block_expert, n_used, x_pad, w_gate, w_up, w_down)


def _final_body(x1_ref, y1_ref, y2_ref, rt_ref, g2_ref, lg_ref, lb_ref, o_ref):
    tm, d = x1_ref.shape
    nb = g2_ref.shape[0]
    y = rt_ref[:, 2:3] * y1_ref[...].astype(F32) + rt_ref[:, 3:4] * y2_ref[...].astype(F32)
    gy = (y.reshape(tm // nb, nb, d) * g2_ref[...][None]).reshape(tm, d)
    out = _ln(DN_ALPHA * x1_ref[...] + gy) * lg_ref[...] + lb_ref[...]
    o_ref[...] = jnp.swapaxes(out.reshape(tm // nb, nb, d), 0, 1)


def _final(x1_tb, y1, y2, route, g2, ln_g, ln_b, tq):
    t, d = x1_tb.shape
    nb = g2.shape[0]
    length = t // nb
    tm = tq * nb
    row = lambda n: pl.BlockSpec((tm, n), lambda i: (i, 0))
    return pl.pallas_call(
        _final_body,
        grid=(length // tq,),
        in_specs=[row(d), row(d), row(d), row(LANES),
                  pl.BlockSpec((nb, d), lambda i: (0, 0)),
                  pl.BlockSpec((1, d), lambda i: (0, 0)),
                  pl.BlockSpec((1, d), lambda i: (0, 0))],
        out_specs=pl.BlockSpec((nb, tq, d), lambda i: (0, i, 0)),
        out_shape=jax.ShapeDtypeStruct((nb, length, d), F32),
        compiler_params=_cparams("parallel"),
        name="moe_combine_postnorm",
    )(x1_tb, y1, y2, route, g2, ln_g, ln_b)


MOE_TM = 512
GDN_GB = 4
MIXER_SUB = 2


def _route_plan(e1, e2, tm):
    t = e1.shape[0]
    e_flat = jnp.concatenate([e1, e2])
    n_assign = e_flat.shape[0]
    experts = jnp.arange(N_EXPERTS, dtype=jnp.int32)
    onehot = (e_flat[:, None] == experts[None, :]).astype(jnp.int32)
    csum = jnp.cumsum(onehot, axis=0)
    rank = jnp.sum(jnp.where(onehot > 0, csum - 1, 0), axis=1)
    counts = csum[-1]
    starts = jnp.cumsum(counts) - counts
    padded = (counts + tm - 1) // tm * tm
    pends = jnp.cumsum(padded)
    pstarts = pends - padded
    dest = pstarts[e_flat] + rank
    n_blocks = (n_assign + N_EXPERTS * (tm - 1) + tm - 1) // tm
    n_used = (pends[-1] // tm).astype(jnp.int32).reshape(1)
    blk0 = jnp.arange(n_blocks, dtype=jnp.int32) * tm
    block_expert = jnp.minimum(jnp.sum((pends[None, :] <= blk0[:, None]).astype(jnp.int32), axis=1),
                               N_EXPERTS - 1).astype(jnp.int32)
    order = jnp.argsort(e_flat, stable=True).astype(jnp.int32)
    p = jnp.arange(n_blocks * tm, dtype=jnp.int32)
    pe = jnp.repeat(block_expert, tm)
    r = p - pstarts[pe]
    valid = r < counts[pe]
    a_idx = jnp.clip(starts[pe] + r, 0, n_assign - 1)
    src = jnp.where(valid, order.at[a_idx].get(mode="promise_in_bounds") % t, p % t)
    return src, dest[:t], dest[t:], block_expert, n_used


def kernel(x, c, ctx, c_ctx, w_mod, b_mod, w_in, b_in, conv_a_w, conv_a_b, lru_wa, lru_ba, lru_wx, lru_bx,
           lru_lambda, conv_qkv_w, gdn_a_log, gdn_dt_bias, gdn_norm_w, w_pa, w_pb, w_out, ln1_g, ln1_b,
           w_router_g, b_router_g, w_router_e, b_router_e, w_e_gate, w_e_up, w_e_down, ln2_g, ln2_b):
    nb, n_lat, d = x.shape
    n_ctx = ctx.shape[1]
    rows = n_lat // GRID_W
    assert rows == GDN_CHUNK and n_ctx % GDN_CHUNK == 0 and w_mod.shape[0] == 1
    layer = 0
    lp = {"conv_a_w": conv_a_w[layer], "conv_a_b": conv_a_b[layer], "lru_wa": lru_wa[layer],
          "lru_ba": lru_ba[layer], "lru_wx": lru_wx[layer], "lru_bx": lru_bx[layer],
          "lru_lambda": lru_lambda[layer]}

    pad_rows = (-(nb + 1)) % 8
    cc = jnp.concatenate([c, c_ctx[None, :], jnp.zeros((pad_rows, d), F32)], axis=0)
    mod = _mod_vectors(cc, w_mod[layer], b_mod[layer])
    sh1, sc1, g1, sh2, sc2, g2 = [mod[:nb, j * d:(j + 1) * d] for j in range(6)]
    csh1 = jnp.broadcast_to(mod[nb:nb + 1, 0:d], (nb, d))
    csc1 = jnp.broadcast_to(mod[nb:nb + 1, d:2 * d], (nb, d))

    w_l = w_in[layer]
    b_l = b_in[layer]
    w_r = jnp.concatenate([w_l[:, OFF_XA:OFF_GDN], w_l[:, OFF_MG:]], axis=1).astype(BF16)
    b_r = jnp.concatenate([b_l[OFF_XA:OFF_GDN], b_l[OFF_MG:]])[None, :]
    gate_pad = LANES - N_GATES
    w_g = jnp.concatenate([w_l[:, OFF_GDN:OFF_GDN + QKV_COLS],
                           jnp.pad(w_l[:, OFF_GDN + QKV_COLS:OFF_Z], ((0, 0), (0, gate_pad))),
                           w_l[:, OFF_Z:OFF_MG]], axis=1).astype(BF16)
    b_g = jnp.concatenate([b_l[OFF_GDN:OFF_GDN + QKV_COLS],
                           jnp.pad(b_l[OFF_GDN + QKV_COLS:OFF_Z], (0, gate_pad)),
                           b_l[OFF_Z:OFF_MG]])[None, :]
    n_qg = QKV_COLS + LANES

    (xa_ctx,) = _inproj_raster(ctx, csc1, csh1, w_r[:, :LRU_WIDTH], b_r[:, :LRU_WIDTH],
                               [(0, LRU_WIDTH)], tt=32)
    zero_lru = jnp.zeros((nb, LRU_WIDTH), F32)
    (sa_f,) = _lru_direction(xa_ctx, lp, 0, zero_lru, False, "none", tt=32)
    (sa_b,) = _lru_direction(xa_ctx, lp, 1, zero_lru, True, "none", tt=32)

    nc_ctx = n_ctx // GDN_CHUNK
    qkv_c, lg_c = _inproj_column(ctx, lambda ci, g: (g, ci, 0), nc_ctx, csc1, csh1,
                                 w_g[:, :n_qg], b_g[:, :n_qg], [(0, QKV_COLS), (QKV_COLS, n_qg)])
    a_row = jnp.pad(-jnp.exp(gdn_a_log[layer].reshape(-1)), (0, LANES - 2 * GDN_HEADS))[None, :]
    dt_row = jnp.pad(gdn_dt_bias[layer].reshape(-1), (0, LANES - 2 * GDN_HEADS))[None, :]
    cw_qkv = conv_qkv_w[layer]
    qkv_c, gates_c = _gdn_prep(qkv_c, lg_c, cw_qkv, a_row, dt_row, GDN_GB)
    zero_gdn = jnp.zeros((nb, GDN_HEADS, GDN_DK, GDN_DV), F32)
    (sb_f,) = _gdn_direction(qkv_c, gates_c, zero_gdn, False, False, GDN_GB)
    (sb_b,) = _gdn_direction(qkv_c, gates_c, zero_gdn, True, False, GDN_GB)

    xa, ga, mg = _inproj_raster(x, sc1, sh1, w_r, b_r,
                                [(0, LRU_WIDTH), (LRU_WIDTH, 2 * LRU_WIDTH), (2 * LRU_WIDTH, 2 * LRU_WIDTH + 2 * d)],
                                tt=32)
    h_f, _ = _lru_direction(xa, lp, 0, sa_f, False, "h", tt=32)
    ya, _ = _lru_direction(xa, lp, 1, sa_b, True, "ya", tt=32, hf=h_f, ga=ga)

    qkv_raw, lg_l, z_l = _inproj_column_lat(x, sc1, sh1, w_g, b_g)
    qkv_l = _gdn_prep_lat(qkv_raw, cw_qkv)
    gates_l = _gdn_gates(lg_l, a_row, dt_row)
    o_f, _ = _gdn_direction(qkv_l, gates_l, sb_f, False, True, GDN_GB)
    o_b, _ = _gdn_direction(qkv_l, gates_l, sb_b, True, True, GDN_GB)
    yb = _gdn_out(o_f, o_b, z_l, gdn_norm_w[layer])

    w_rt = jnp.pad(jnp.concatenate([w_router_g[layer], w_router_e[layer]], axis=1),
                   ((0, 0), (0, LANES - N_GROUPS - N_EXPERTS)))
    b_rt = jnp.pad(jnp.concatenate([b_router_g[layer], b_router_e[layer]]),
                   (0, LANES - N_GROUPS - N_EXPERTS))[None, :]
    w_rt_hi = w_rt.astype(BF16)
    w_rt_lo = (w_rt - w_rt_hi.astype(F32)).astype(BF16)
    x1, h2, route = _mixer(ya, yb, mg, x, g1, sh2, sc2, w_pa[layer].astype(BF16), w_pb[layer].astype(BF16),
                           w_out[layer].astype(BF16), ln1_g[layer][None, :], ln1_b[layer][None, :],
                           w_rt_hi, w_rt_lo, b_rt, tt=32)

    e1 = route[:, 0].astype(jnp.int32)
    e2 = route[:, 1].astype(jnp.int32)
    src, dest1, dest2, block_expert, n_used = _route_plan(e1, e2, MOE_TM)
    x_pad = h2.at[src].get(mode="promise_in_bounds")
    y_pad = _experts(x_pad, block_expert, n_used, w_e_gate[layer], w_e_up[layer], w_e_down[layer], MOE_TM)
    y1 = y_pad.at[dest1].get(mode="promise_in_bounds")
    y2 = y_pad.at[dest2].get(mode="promise_in_bounds")
    return _final(x1, y1, y2, route, g2, ln2_g[layer][None, :], ln2_b[layer][None, :], tq=32)
```

```python
import functools
import math

import jax
import jax.numpy as jnp
from jax import lax
from jax.experimental import pallas as pl
from jax.experimental.pallas import tpu as pltpu
from jax.experimental.pallas import tpu_sc as plsc

F32 = jnp.float32
BF16 = jnp.bfloat16

D_MODEL = 1024
GRID_W = 64
LRU_WIDTH = 1024
LRU_BLOCKS = 8
LRU_BLOCK = LRU_WIDTH // LRU_BLOCKS
LRU_C = 8.0
CONV_W = 4
GDN_HEADS = 8
GDN_DK = 128
GDN_DV = 128
GDN_QK = GDN_HEADS * GDN_DK
GDN_VW = GDN_HEADS * GDN_DV
GDN_CHUNK = 64
QKV_COLS = 2 * GDN_QK + GDN_VW
N_GATES = 4 * GDN_HEADS
OFF_XA = 0
OFF_GA = OFF_XA + LRU_WIDTH
OFF_GDN = OFF_GA + LRU_WIDTH
OFF_Z = OFF_GDN + QKV_COLS + N_GATES
OFF_MG = OFF_Z + GDN_VW
N_GROUPS = 4
EXP_PER_GROUP = 8
N_EXPERTS = N_GROUPS * EXP_PER_GROUP
TOP_K = 2
LN_EPS = 1e-6
L2_EPS = 1e-6
DEPTH = 1
DN_ALPHA = (2.0 * DEPTH) ** 0.25

LANES = 128
VMEM_LIMIT = 56 * 1024 * 1024

HI = lax.Precision.HIGHEST


def _cparams(*sem):
    return pltpu.CompilerParams(dimension_semantics=sem, vmem_limit_bytes=VMEM_LIMIT)


def _bdot(a, b):
    return jnp.dot(a.astype(BF16), b.astype(BF16), preferred_element_type=F32)


def _bdot_nt(a, b):
    return lax.dot_general(a.astype(BF16), b.astype(BF16), (((1,), (1,)), ((), ())),
                           preferred_element_type=F32)


def _ln(x):
    mu = jnp.mean(x, axis=-1, keepdims=True)
    xc = x - mu
    var = jnp.mean(xc * xc, axis=-1, keepdims=True)
    return xc * lax.rsqrt(var + LN_EPS)


def _sigmoid(x):
    return 0.5 * jnp.tanh(0.5 * x) + 0.5


def _silu(x):
    return x * _sigmoid(x)


def _softplus(x):
    return jnp.maximum(x, 0.0) + jnp.log(1.0 + jnp.exp(-jnp.abs(x)))


def _gelu_tanh(x):
    return 0.5 * x * (1.0 + jnp.tanh(math.sqrt(2.0 / math.pi) * (x + 0.044715 * (x * x * x))))


def _mod_body(c_ref, w_ref, b_ref, o_ref):
    o_ref[...] = jnp.dot(_silu(c_ref[...]), w_ref[...], preferred_element_type=F32,
                         precision=HI) + b_ref[...]


def _mod_vectors(cc, w_mod, b_mod):
    rows, d = cc.shape
    n = w_mod.shape[1]
    tn = 1536
    return pl.pallas_call(
        _mod_body,
        grid=(n // tn,),
        in_specs=[pl.BlockSpec((rows, d), lambda j: (0, 0)),
                  pl.BlockSpec((d, tn), lambda j: (0, j)),
                  pl.BlockSpec((1, tn), lambda j: (0, j))],
        out_specs=pl.BlockSpec((rows, tn), lambda j: (0, j)),
        out_shape=jax.ShapeDtypeStruct((rows, n), F32),
        compiler_params=_cparams("arbitrary"),
        name="mod_vectors",
    )(cc, w_mod, b_mod.reshape(1, n))


def _inproj_r_body(x_ref, sc_ref, sh_ref, w_ref, b_ref, *o_refs, splits):
    nb, tt, d = x_ref.shape
    tm = nb * tt
    xn = _ln(jnp.swapaxes(x_ref[...], 0, 1).reshape(tm, d)).reshape(tt, nb, d)
    xm = (xn * (1.0 + sc_ref[...])[None] + sh_ref[...][None]).reshape(tm, d).astype(BF16)
    for o_ref, (lo, hi) in zip(o_refs, splits):
        for n0 in range(lo, hi, 512):
            o_ref[:, n0 - lo:n0 - lo + 512] = (
                jnp.dot(xm, w_ref[:, n0:n0 + 512], preferred_element_type=F32) + b_ref[:, n0:n0 + 512])


def _inproj_raster(x, sc, sh, w, b, splits, tt):
    nb, length, d = x.shape
    n = w.shape[1]
    tm = tt * nb
    body = functools.partial(_inproj_r_body, splits=splits)
    return pl.pallas_call(
        body,
        grid=(length // tt,),
        in_specs=[pl.BlockSpec((nb, tt, d), lambda i: (0, i, 0)),
                  pl.BlockSpec((nb, d), lambda i: (0, 0)),
                  pl.BlockSpec((nb, d), lambda i: (0, 0)),
                  pl.BlockSpec((d, n), lambda i: (0, 0)),
                  pl.BlockSpec((1, n), lambda i: (0, 0))],
        out_specs=[pl.BlockSpec((tm, hi - lo), lambda i: (i, 0)) for lo, hi in splits],
        out_shape=[jax.ShapeDtypeStruct((length * nb, hi - lo), F32) for lo, hi in splits],
        compiler_params=_cparams("parallel"),
        name="inproj_raster",
    )(x, sc, sh, w, b)


def _inproj_g_body(x_ref, sc_ref, sh_ref, w_ref, b_ref, *o_refs, splits):
    gb, ch, d = x_ref.shape
    parts = []
    for i in range(gb):
        xn = _ln(x_ref[i])
        parts.append((xn * (1.0 + sc_ref[i:i + 1, :]) + sh_ref[i:i + 1, :]).astype(BF16))
    xm = jnp.concatenate(parts, axis=0)
    for o_ref, (lo, hi) in zip(o_refs, splits):
        step = 512 if (hi - lo) % 512 == 0 else hi - lo
        for n0 in range(lo, hi, step):
            res = jnp.dot(xm, w_ref[:, n0:n0 + step], preferred_element_type=F32) + b_ref[:, n0:n0 + step]
            for i in range(gb):
                o_ref[0, i, :, n0 - lo:n0 - lo + step] = res[i * ch:(i + 1) * ch]


def _inproj_column(xv, x_index_map, n_chunks, sc, sh, w, b, splits):
    nb, d = sc.shape
    gb = 8
    n = w.shape[1]
    body = functools.partial(_inproj_g_body, splits=splits)
    return pl.pallas_call(
        body,
        grid=(n_chunks, nb // gb),
        in_specs=[pl.BlockSpec((gb, GDN_CHUNK, d), x_index_map),
                  pl.BlockSpec((gb, d), lambda c, g: (g, 0)),
                  pl.BlockSpec((gb, d), lambda c, g: (g, 0)),
                  pl.BlockSpec((d, n), lambda c, g: (0, 0)),
                  pl.BlockSpec((1, n), lambda c, g: (0, 0))],
        out_specs=[pl.BlockSpec((1, gb, GDN_CHUNK, hi - lo), lambda c, g: (c, g, 0, 0)) for lo, hi in splits],
        out_shape=[jax.ShapeDtypeStruct((n_chunks, nb, GDN_CHUNK, hi - lo), F32) for lo, hi in splits],
        compiler_params=_cparams("parallel", "parallel"),
        name="inproj_column",
    )(xv, sc, sh, w, b)


def _inproj_gl_body(x_ref, sc_ref, sh_ref, w_ref, b_ref, qkv_ref, lg_ref, z_ref):
    _, tm, d = x_ref.shape
    rr = tm // GRID_W
    xm = _ln(x_ref[0]) * (1.0 + sc_ref[0]) + sh_ref[0]
    xb = xm.astype(BF16)
    for n0 in range(0, QKV_COLS, 512):
        qkv_ref[0, :, n0:n0 + 512] = (jnp.dot(xb, w_ref[:, n0:n0 + 512], preferred_element_type=F32)
                                      + b_ref[:, n0:n0 + 512])
    xs = jnp.swapaxes(xm.reshape(rr, GRID_W, d), 0, 1).reshape(tm, d).astype(BF16)
    n_qg = QKV_COLS + LANES
    lg = jnp.dot(xs, w_ref[:, QKV_COLS:n_qg], preferred_element_type=F32) + b_ref[:, QKV_COLS:n_qg]
    lg_ref[:, 0] = lg.reshape(GRID_W, rr, LANES)
    for n0 in range(0, GDN_VW, 512):
        res = (jnp.dot(xs, w_ref[:, n_qg + n0:n_qg + n0 + 512], preferred_element_type=F32)
               + b_ref[:, n_qg + n0:n_qg + n0 + 512])
        z_ref[:, 0, :, n0:n0 + 512] = res.reshape(GRID_W, rr, 512)


def _inproj_column_lat(x, sc, sh, w, b):
    nb, length, d = x.shape
    n = w.shape[1]
    rows = length // GRID_W
    rr = 8
    tm = rr * GRID_W
    vec = pl.BlockSpec((1, 1, d), lambda bi, ri: (bi, 0, 0))
    chunked = lambda m: pl.BlockSpec((GRID_W, 1, rr, m), lambda bi, ri: (0, bi, ri, 0))
    return pl.pallas_call(
        _inproj_gl_body,
        grid=(nb, rows // rr),
        in_specs=[pl.BlockSpec((1, tm, d), lambda bi, ri: (bi, ri, 0)),
                  vec, vec,
                  pl.BlockSpec((d, n), lambda bi, ri: (0, 0)),
                  pl.BlockSpec((1, n), lambda bi, ri: (0, 0))],
        out_specs=[pl.BlockSpec((1, tm, QKV_COLS), lambda bi, ri: (bi, ri, 0)), chunked(LANES), chunked(GDN_VW)],
        out_shape=[jax.ShapeDtypeStruct((nb, length, QKV_COLS), F32),
                   jax.ShapeDtypeStruct((GRID_W, nb, rows, LANES), F32),
                   jax.ShapeDtypeStruct((GRID_W, nb, rows, GDN_VW), F32)],
        compiler_params=_cparams("parallel", "parallel"),
        name="inproj_column_lat",
    )(x, sc.reshape(nb, 1, d), sh.reshape(nb, 1, d), w, b)


def _lru_body(*refs, tt, nb, reverse, emit):
    if emit == "ya":
        (xa_ref, prev_ref, next_ref, cw_ref, cb_ref, wa_ref, ba_ref, wx_ref, bx_ref, lam_ref, h0_ref,
         hf_ref, ga_ref, out_ref, hl_ref, a_s, b_s, h_s) = refs
    elif emit == "h":
        (xa_ref, prev_ref, next_ref, cw_ref, cb_ref, wa_ref, ba_ref, wx_ref, bx_ref, lam_ref, h0_ref,
         out_ref, hl_ref, a_s, b_s, h_s) = refs
    else:
        (xa_ref, prev_ref, next_ref, cw_ref, cb_ref, wa_ref, ba_ref, wx_ref, bx_ref, lam_ref, h0_ref,
         hl_ref, a_s, b_s, h_s) = refs
    step = pl.program_id(0)
    nsteps = pl.num_programs(0)
    blk = (nsteps - 1 - step) if reverse else step
    rows = tt * nb

    @pl.when(step == 0)
    def _():
        h_s[...] = h0_ref[...]

    prev = jnp.where(blk == 0, 0.0, prev_ref[...])
    nxt = jnp.where(blk == nsteps - 1, 0.0, next_ref[...])
    ext = jnp.concatenate([prev, xa_ref[...], nxt], axis=0)
    u = (cw_ref[0:1, :] * ext[0:rows] + cw_ref[1:2, :] * ext[nb:rows + nb]
         + cw_ref[2:3, :] * ext[2 * nb:rows + 2 * nb] + cw_ref[3:4, :] * ext[3 * nb:rows + 3 * nb]
         + cb_ref[...])
    c_all = (-0.5 * LRU_C) * _softplus(-lam_ref[...])
    for n in range(LRU_BLOCKS):
        sl = slice(n * LRU_BLOCK, (n + 1) * LRU_BLOCK)
        un = u[:, sl]
        ub = un.astype(BF16)
        t_r = jnp.tanh(jnp.dot(ub, wa_ref[n], preferred_element_type=F32) + ba_ref[:, sl])
        t_i = jnp.tanh(jnp.dot(ub, wx_ref[n], preferred_element_type=F32) + bx_ref[:, sl])
        c = c_all[:, sl]
        a = jnp.exp(c * t_r + c)
        hun = 0.5 * un
        a_s[:, sl] = a
        b_s[:, sl] = jnp.sqrt(1.0 - a * a) * (t_i * hun + hun)

    def scan_step(j, h):
        t = (tt - 1 - j) if reverse else j
        r0 = pl.multiple_of(t * nb, nb)
        h = a_s[pl.ds(r0, nb), :] * h + b_s[pl.ds(r0, nb), :]
        if emit == "h":
            out_ref[pl.ds(r0, nb), :] = h
        elif emit == "ya":
            b_s[pl.ds(r0, nb), :] = h
        return h

    h = lax.fori_loop(0, tt, scan_step, h_s[...], unroll=4)
    h_s[...] = h
    hl_ref[...] = h
    if emit == "ya":
        out_ref[...] = (_gelu_tanh(ga_ref[...]) * (hf_ref[...] + b_s[...])).astype(BF16)


def _lru_direction(xa, lp, di, h0, reverse, emit, tt, hf=None, ga=None):
    t, w = xa.shape
    nb = h0.shape[0]
    rows = tt * nb
    nblk = t // rows
    assert rows % (2 * nb) == 0
    prev_per = rows // nb
    next_per = rows // (2 * nb)
    n_prev = t // nb
    n_next = t // (2 * nb)

    def bi(i):
        return (nblk - 1 - i) if reverse else i

    const2 = lambda i: (0, 0)
    in_specs = [
        pl.BlockSpec((rows, w), lambda i: (bi(i), 0)),
        pl.BlockSpec((nb, w), lambda i: (jnp.maximum(bi(i) * prev_per - 1, 0), 0)),
        pl.BlockSpec((2 * nb, w), lambda i: (jnp.minimum((bi(i) + 1) * next_per, n_next - 1), 0)),
        pl.BlockSpec((CONV_W, w), const2),
        pl.BlockSpec((1, w), const2),
        pl.BlockSpec((LRU_BLOCKS, LRU_BLOCK, LRU_BLOCK), lambda i: (0, 0, 0)),
        pl.BlockSpec((1, w), const2),
        pl.BlockSpec((LRU_BLOCKS, LRU_BLOCK, LRU_BLOCK), lambda i: (0, 0, 0)),
        pl.BlockSpec((1, w), const2),
        pl.BlockSpec((1, w), const2),
        pl.BlockSpec((nb, w), const2),
    ]
    args = [xa, xa, xa, lp["conv_a_w"], lp["conv_a_b"].reshape(1, w),
            (0.5 * lp["lru_wa"][di]).astype(BF16), (0.5 * lp["lru_ba"][di]).reshape(1, w),
            (0.5 * lp["lru_wx"][di]).astype(BF16), (0.5 * lp["lru_bx"][di]).reshape(1, w),
            lp["lru_lambda"][di].reshape(1, w), h0]
    out_specs = []
    out_shape = []
    if emit == "ya":
        in_specs += [pl.BlockSpec((rows, w), lambda i: (bi(i), 0)),
                     pl.BlockSpec((rows, w), lambda i: (bi(i), 0))]
        args += [hf, ga]
        out_specs.append(pl.BlockSpec((rows, w), lambda i: (bi(i), 0)))
        out_shape.append(jax.ShapeDtypeStruct((t, w), BF16))
    elif emit == "h":
        out_specs.append(pl.BlockSpec((rows, w), lambda i: (bi(i), 0)))
        out_shape.append(jax.ShapeDtypeStruct((t, w), F32))
    out_specs.append(pl.BlockSpec((nb, w), const2))
    out_shape.append(jax.ShapeDtypeStruct((nb, w), F32))
    body = functools.partial(_lru_body, tt=tt, nb=nb, reverse=reverse, emit=emit)
    return pl.pallas_call(
        body,
        grid=(nblk,),
        in_specs=in_specs,
        out_specs=out_specs,
        out_shape=out_shape,
        scratch_shapes=[pltpu.VMEM((rows, w), F32), pltpu.VMEM((rows, w), F32), pltpu.VMEM((nb, w), F32)],
        compiler_params=_cparams("arbitrary"),
        name="lru_" + ("bwd" if reverse else "fwd") + "_" + emit,
    )(*args)


def _gdn_prep_body(qkv_ref, prev_ref, next_ref, lg_ref, cw_ref, ga_ref, gd_ref, qkv_o, g_o):
    c = pl.program_id(0)
    nc = pl.num_programs(0)
    ch = GDN_CHUNK
    gb = qkv_ref.shape[1]
    for i in range(gb):
        prev = jnp.where(c == 0, 0.0, prev_ref[0, i])
        nxt = jnp.where(c == nc - 1, 0.0, next_ref[0, i])
        ext = jnp.concatenate([prev, qkv_ref[0, i], nxt], axis=0)
        u = (cw_ref[0:1, :] * ext[7:7 + ch] + cw_ref[1:2, :] * ext[8:8 + ch]
             + cw_ref[2:3, :] * ext[9:9 + ch] + cw_ref[3:4, :] * ext[10:10 + ch])
        act = _silu(u)
        for h in range(GDN_HEADS):
            sq = slice(h * GDN_DK, (h + 1) * GDN_DK)
            qh = act[:, sq]
            qn = qh * lax.rsqrt(jnp.sum(qh * qh, axis=-1, keepdims=True) + L2_EPS) * (GDN_DK ** -0.5)
            qkv_o[0, i, :, sq] = qn.astype(qkv_o.dtype)
            sk = slice(GDN_QK + h * GDN_DK, GDN_QK + (h + 1) * GDN_DK)
            kh = act[:, sk]
            kn = kh * lax.rsqrt(jnp.sum(kh * kh, axis=-1, keepdims=True) + L2_EPS)
            qkv_o[0, i, :, sk] = kn.astype(qkv_o.dtype)
        qkv_o[0, i, :, 2 * GDN_QK:] = act[:, 2 * GDN_QK:].astype(qkv_o.dtype)
        lg = lg_ref[0, i]
        lane = lax.broadcasted_iota(jnp.int32, lg.shape, 1)
        decay = ga_ref[...] * _softplus(lg + gd_ref[...])
        g_o[0, i] = jnp.where(lane < 2 * GDN_HEADS, decay, jnp.where(lane < N_GATES, _sigmoid(lg), 0.0))


def _gdn_prep(qkv, lg, conv_w, a_row, dt_row, gb):
    nc, nb, ch, n = qkv.shape
    hb = ch // 8
    return pl.pallas_call(
        _gdn_prep_body,
        grid=(nc, nb // gb),
        in_specs=[pl.BlockSpec((1, gb, ch, n), lambda c, b: (c, b, 0, 0)),
                  pl.BlockSpec((1, gb, 8, n), lambda c, b: (jnp.maximum(c - 1, 0), b, hb - 1, 0)),
                  pl.BlockSpec((1, gb, 8, n), lambda c, b: (jnp.minimum(c + 1, nc - 1), b, 0, 0)),
                  pl.BlockSpec((1, gb, ch, LANES), lambda c, b: (c, b, 0, 0)),
                  pl.BlockSpec((CONV_W, n), lambda c, b: (0, 0)),
                  pl.BlockSpec((1, LANES), lambda c, b: (0, 0)),
                  pl.BlockSpec((1, LANES), lambda c, b: (0, 0))],
        out_specs=[pl.BlockSpec((1, gb, ch, n), lambda c, b: (c, b, 0, 0)),
                   pl.BlockSpec((1, gb, ch, LANES), lambda c, b: (c, b, 0, 0))],
        out_shape=[jax.ShapeDtypeStruct(qkv.shape, BF16), jax.ShapeDtypeStruct(lg.shape, F32)],
        compiler_params=_cparams("parallel", "parallel"),
        name="gdn_prep",
    )(qkv, qkv, qkv, lg, conv_w, a_row, dt_row)


def _gdn_prep_lat_body(x_ref, prev_ref, next_ref, cw_ref, o_ref):
    ri = pl.program_id(1)
    nr = pl.num_programs(1)
    part = pl.program_id(2)
    gw = GRID_W
    main = x_ref[0]
    rows, n = main.shape
    prev = prev_ref[0]
    nxt = next_ref[0]
    zrow = jnp.zeros((1, n), F32)
    prev_wrap = jnp.concatenate([zrow, prev[:gw - 1]], axis=0)
    nxt_wrap = jnp.concatenate([nxt[1:gw], zrow, nxt[gw + 1:], zrow], axis=0)
    prev = jnp.where(ri == 0, prev_wrap, prev)
    nxt = jnp.where(ri == nr - 1, nxt_wrap, nxt)
    ext = jnp.concatenate([prev, main, nxt], axis=0)
    cwh = 0.5 * cw_ref[...]
    hu = (cwh[0:1, :] * ext[0:rows] + cwh[1:2, :] * ext[gw:gw + rows]
          + cwh[2:3, :] * ext[2 * gw:2 * gw + rows] + cwh[3:4, :] * ext[3 * gw:3 * gw + rows])
    act = hu * jnp.tanh(hu) + hu
    scale = jnp.where(part == 0, GDN_DK ** -0.5, 1.0)
    parts = []
    for h in range(n // GDN_DK):
        ah = act[:, h * GDN_DK:(h + 1) * GDN_DK]
        nh = ah * (lax.rsqrt(jnp.sum(ah * ah, axis=-1, keepdims=True) + L2_EPS) * scale)
        parts.append(jnp.where(part < 2, nh, ah))
    y = jnp.concatenate(parts, axis=1)
    o_ref[:, 0] = jnp.swapaxes(y.reshape(rows // gw, gw, n), 0, 1).astype(BF16)


def _gdn_prep_lat(qkv_raw, conv_w):
    nb, length, n3 = qkv_raw.shape
    gw = GRID_W
    rows = length // gw
    rr = 16
    n = GDN_QK
    assert n3 == 3 * n and rows % rr == 0
    return pl.pallas_call(
        _gdn_prep_lat_body,
        grid=(nb, rows // rr, 3),
        in_specs=[pl.BlockSpec((1, rr * gw, n), lambda b, ri, j: (b, ri, j)),
                  pl.BlockSpec((1, gw, n), lambda b, ri, j: (b, (ri * rr + rows - 1) % rows, j)),
                  pl.BlockSpec((1, 2 * gw, n), lambda b, ri, j: (b, (((ri + 1) * rr) % rows) // 2, j)),
                  pl.BlockSpec((CONV_W, n), lambda b, ri, j: (0, j))],
        out_specs=pl.BlockSpec((gw, 1, rr, n), lambda b, ri, j: (0, b, ri, j)),
        out_shape=jax.ShapeDtypeStruct((gw, nb, rows, n3), BF16),
        compiler_params=_cparams("parallel", "parallel", "parallel"),
        name="gdn_prep_lat",
    )(qkv_raw, qkv_raw, qkv_raw, conv_w)


def _gdn_gates_body(lg_ref, ga_ref, gd_ref, g_o):
    lg = lg_ref[...]
    lane = lax.broadcasted_iota(jnp.int32, lg.shape, lg.ndim - 1)
    decay = ga_ref[...] * _softplus(lg + gd_ref[...])
    g_o[...] = jnp.where(lane < 2 * GDN_HEADS, decay, jnp.where(lane < N_GATES, _sigmoid(lg), 0.0))


def _gdn_gates(lg, a_row, dt_row):
    nc, nb, ch, n = lg.shape
    cb = 8
    spec = pl.BlockSpec((cb, nb, ch, n), lambda c: (c, 0, 0, 0))
    vec = pl.BlockSpec((1, n), lambda c: (0, 0))
    return pl.pallas_call(
        _gdn_gates_body,
        grid=(nc // cb,),
        in_specs=[spec, vec, vec],
        out_specs=spec,
        out_shape=jax.ShapeDtypeStruct(lg.shape, F32),
        compiler_params=_cparams("parallel"),
        name="gdn_gates",
    )(lg, a_row, dt_row)


def _gdn_scan_body(qkv_ref, g_ref, s0_ref, *rest, reverse, emit_o):
    if emit_o:
        o_ref, sl_ref, s_s = rest
    else:
        sl_ref, s_s = rest
    step = pl.program_id(1)
    ch = GDN_CHUNK
    gb = qkv_ref.shape[1]
    n_sq = int(math.log2(ch)) - 1

    @pl.when(step == 0)
    def _():
        s_s[...] = s0_ref[...]

    row = lax.broadcasted_iota(jnp.int32, (ch, ch), 0)
    col = lax.broadcasted_iota(jnp.int32, (ch, ch), 1)
    incl = (row <= col) if reverse else (row >= col)
    strict = (row < col) if reverse else (row > col)
    eye = (row == col).astype(F32)
    tri = incl.astype(F32)
    goff = GDN_HEADS if reverse else 0
    boff = 2 * GDN_HEADS + goff

    units = [(i, h) for i in range(gb) for h in range(GDN_HEADS)]
    nu = len(units)
    gcum, gcum_t, gtot, gates = [], [], [], []
    for i in range(gb):
        g = g_ref[0, i]
        gates.append(g)
        cum = jnp.dot(tri, g, preferred_element_type=F32, precision=HI)
        gcum.append(cum)
        gcum_t.append(cum.T)
        gtot.append(jnp.sum(g, axis=0, keepdims=True))

    q, k, kb, xin, decay, eg, egl, gl = [], [], [], [], [], [], [], []
    for i, h in units:
        qh = qkv_ref[0, i, :, h * GDN_DK:(h + 1) * GDN_DK].astype(F32)
        kh = qkv_ref[0, i, :, GDN_QK + h * GDN_DK:GDN_QK + (h + 1) * GDN_DK].astype(F32)
        vh = qkv_ref[0, i, :, 2 * GDN_QK + h * GDN_DV:2 * GDN_QK + (h + 1) * GDN_DV].astype(F32)
        gc = jnp.broadcast_to(gcum[i][:, goff + h:goff + h + 1], (ch, GDN_DK))
        beta = jnp.broadcast_to(gates[i][:, boff + h:boff + h + 1], (ch, GDN_DK))
        gr = gcum_t[i][goff + h:goff + h + 1, :]
        gt = gtot[i][:, goff + h:goff + h + 1]
        decay.append(jnp.where(incl, jnp.exp(jnp.where(incl, gc[:, :ch] - gr, 0.0)), 0.0))
        e = jnp.exp(gc)
        kbh = kh * beta
        q.append(qh)
        k.append(kh)
        kb.append(kbh)
        eg.append(e)
        egl.append(jnp.exp(gt - gc))
        gl.append(jnp.exp(gt))
        xin.append(jnp.concatenate([kbh * e, vh * beta], axis=1).astype(BF16))

    gram = [_bdot_nt(jnp.concatenate([kb[j], q[j]], axis=0), k[j]) for j in range(nu)]
    nil = [-jnp.where(strict, gram[j][:ch] * decay[j], 0.0) for j in range(nu)]
    amat = [gram[j][ch:] * decay[j] for j in range(nu)]
    inv = [eye + m for m in nil]
    power = nil
    for _ in range(n_sq):
        pb = [p.astype(BF16) for p in power]
        power = [jnp.dot(p, p, preferred_element_type=F32) for p in pb]
        inv = [iv + _bdot(iv, p) for iv, p in zip(inv, power)]
    wu = [_bdot(iv, x) for iv, x in zip(inv, xin)]
    s_old = [s_s[i, h] for i, h in units]
    s_bf = [s.astype(BF16) for s in s_old]
    ws = [_bdot(jnp.concatenate([wu[j][:, :GDN_DK], q[j] * eg[j]], axis=0), s_bf[j]) for j in range(nu)]
    v_new = [(wu[j][:, GDN_DK:] - ws[j][:ch]).astype(BF16) for j in range(nu)]
    upd = [_bdot(jnp.concatenate([amat[j], (k[j] * egl[j]).T], axis=0), v_new[j]) for j in range(nu)]
    for j, (i, h) in enumerate(units):
        if emit_o:
            o_ref[0, i, :, h * GDN_DV:(h + 1) * GDN_DV] = (ws[j][ch:] + upd[j][:ch]).astype(o_ref.dtype)
        s_new = s_old[j] * gl[j] + upd[j][ch:]
        s_s[i, h] = s_new
        sl_ref[i, h] = s_new


def _gdn_direction(qkv, gates, s0, reverse, emit_o, gb=1):
    nc, nb, ch, n = qkv.shape

    def ci(c):
        return (nc - 1 - c) if reverse else c

    out_specs = []
    out_shape = []
    if emit_o:
        out_specs.append(pl.BlockSpec((1, gb, ch, GDN_VW), lambda b, c: (ci(c), b, 0, 0)))
        out_shape.append(jax.ShapeDtypeStruct((nc, nb, ch, GDN_VW), BF16))
    out_specs.append(pl.BlockSpec((gb, GDN_HEADS, GDN_DK, GDN_DV), lambda b, c: (b, 0, 0, 0)))
    out_shape.append(jax.ShapeDtypeStruct((nb, GDN_HEADS, GDN_DK, GDN_DV), F32))
    body = functools.partial(_gdn_scan_body, reverse=reverse, emit_o=emit_o)
    return pl.pallas_call(
        body,
        grid=(nb // gb, nc),
        in_specs=[pl.BlockSpec((1, gb, ch, n), lambda b, c: (ci(c), b, 0, 0)),
                  pl.BlockSpec((1, gb, ch, LANES), lambda b, c: (ci(c), b, 0, 0)),
                  pl.BlockSpec((gb, GDN_HEADS, GDN_DK, GDN_DV), lambda b, c: (b, 0, 0, 0))],
        out_specs=out_specs,
        out_shape=out_shape,
        scratch_shapes=[pltpu.VMEM((gb, GDN_HEADS, GDN_DK, GDN_DV), F32)],
        compiler_params=_cparams("parallel", "arbitrary"),
        name="gdn_scan_" + ("bwd" if reverse else "fwd"),
    )(qkv, gates, s0)


def _gdn_out_body(of_ref, ob_ref, z_ref, nw_ref, y_ref, y_s):
    nb = of_ref.shape[1]
    for i in range(nb):
        o = of_ref[0, i].astype(F32) + ob_ref[0, i].astype(F32)
        zg = _silu(z_ref[0, i])
        for h in range(GDN_HEADS):
            sl = slice(h * GDN_DV, (h + 1) * GDN_DV)
            oh = o[:, sl]
            nh = oh * lax.rsqrt(jnp.mean(oh * oh, axis=-1, keepdims=True) + LN_EPS) * nw_ref[...]
            y_s[i, :, sl] = nh * zg[:, sl]
    y_ref[:, 0] = jnp.swapaxes(y_s[...], 0, 1).astype(BF16)


def _gdn_out(o_f, o_b, z, norm_w):
    nc, nb, ch, vw = o_f.shape
    spec = pl.BlockSpec((1, nb, ch, vw), lambda c: (c, 0, 0, 0))
    y = pl.pallas_call(
        _gdn_out_body,
        grid=(nc,),
        in_specs=[spec, spec, spec, pl.BlockSpec((1, GDN_DV), lambda c: (0, 0))],
        out_specs=pl.BlockSpec((ch, 1, nb, vw), lambda c: (0, c, 0, 0)),
        out_shape=jax.ShapeDtypeStruct((ch, nc, nb, vw), BF16),
        scratch_shapes=[pltpu.VMEM((nb, ch, vw), F32)],
        compiler_params=_cparams("parallel"),
        name="gdn_out",
    )(o_f, o_b, z, norm_w.reshape(1, GDN_DV))
    return y.reshape(ch * nc * nb, vw)


def _mixer_body(ya_ref, yb_ref, mg_ref, x_ref, g1_ref, sh2_ref, sc2_ref, wpa_ref, wpb_ref, wo_ref,
                lg_ref, lb_ref, wrh_ref, wrl_ref, br_ref, x1_ref, h2_ref, rt_ref, *, n_sub):
    nb, tt, d = x_ref.shape
    ts = tt // n_sub
    rs = ts * nb
    subs = [slice(k * rs, (k + 1) * rs) for k in range(n_sub)]
    xs = [jnp.swapaxes(x_ref[:, k * ts:(k + 1) * ts, :], 0, 1).reshape(rs, d) for k in range(n_sub)]
    pa = [jnp.dot(ya_ref[s, :], wpa_ref[...], preferred_element_type=F32) for s in subs]
    pb = [jnp.dot(yb_ref[s, :], wpb_ref[...], preferred_element_type=F32) for s in subs]
    merged = [(_sigmoid(mg_ref[s, :d]) * a + _sigmoid(mg_ref[s, d:]) * b).astype(BF16)
              for s, a, b in zip(subs, pa, pb)]
    mix = [jnp.dot(m, wo_ref[...], preferred_element_type=F32) for m in merged]
    x1 = []
    for s, xk, mk in zip(subs, xs, mix):
        gm = (mk.reshape(ts, nb, d) * g1_ref[...][None]).reshape(rs, d)
        x1k = _ln(DN_ALPHA * xk + gm) * lg_ref[...] + lb_ref[...]
        x1_ref[s, :] = x1k
        x1.append(x1k)
    h2 = [(_ln(v).reshape(ts, nb, d) * (1.0 + sc2_ref[...])[None] + sh2_ref[...][None]).reshape(rs, d) for v in x1]
    h2_hi = [v.astype(BF16) for v in h2]
    for s, v in zip(subs, h2_hi):
        h2_ref[s, :] = v.astype(h2_ref.dtype)
    h2_lo = [(v - hi.astype(F32)).astype(BF16) for v, hi in zip(h2, h2_hi)]
    all_logits = [(jnp.dot(hi, wrh_ref[...], preferred_element_type=F32)
                   + jnp.dot(lo, wrh_ref[...], preferred_element_type=F32)
                   + jnp.dot(hi, wrl_ref[...], preferred_element_type=F32) + br_ref[...])
                  for hi, lo in zip(h2_hi, h2_lo)]
    for s, logits in zip(subs, all_logits):
        rt_ref[s, :] = _route(logits)


def _route(logits):
    lane = lax.broadcasted_iota(jnp.int32, logits.shape, 1).astype(F32)
    neg = jnp.float32(-jnp.inf)
    big = jnp.float32(1 << 20)
    is_g = lane < N_GROUPS
    gl = jnp.where(is_g, logits, neg)
    gmax = jnp.max(gl, axis=-1, keepdims=True)
    gsum = jnp.sum(jnp.where(is_g, jnp.exp(gl - gmax), 0.0), axis=-1, keepdims=True)
    p_group = 1.0 / gsum
    g_idx = jnp.min(jnp.where(gl == gmax, lane, big), axis=-1, keepdims=True)
    lo = N_GROUPS + g_idx * EXP_PER_GROUP
    in_g = (lane >= lo) & (lane < lo + EXP_PER_GROUP)
    el = jnp.where(in_g, logits, neg)
    m1 = jnp.max(el, axis=-1, keepdims=True)
    i1 = jnp.min(jnp.where(el == m1, lane, big), axis=-1, keepdims=True)
    el2 = jnp.where(lane == i1, neg, el)
    m2 = jnp.max(el2, axis=-1, keepdims=True)
    i2 = jnp.min(jnp.where(el2 == m2, lane, big), axis=-1, keepdims=True)
    esum = jnp.sum(jnp.where(in_g, jnp.exp(el - m1), 0.0), axis=-1, keepdims=True)
    p1 = 1.0 / esum
    p2 = jnp.exp(m2 - m1) / esum
    w1 = p_group * p1 / (p1 + p2)
    w2 = p_group * p2 / (p1 + p2)
    e1 = i1 - N_GROUPS
    e2 = i2 - N_GROUPS
    return jnp.where(lane == 0, e1, jnp.where(lane == 1, e2, jnp.where(lane == 2, w1,
                     jnp.where(lane == 3, w2, 0.0))))


def _mixer(ya, yb, mg, x, g1, sh2, sc2, w_pa, w_pb, w_out, ln_g, ln_b, w_r_hi, w_r_lo, b_r, tt):
    nb, length, d = x.shape
    t = nb * length
    tm = tt * nb
    row = lambda n: pl.BlockSpec((tm, n), lambda i: (i, 0))
    full = lambda a: pl.BlockSpec(a.shape, lambda i: (0,) * a.ndim)
    args = [ya, yb, mg, x, g1, sh2, sc2, w_pa, w_pb, w_out, ln_g, ln_b, w_r_hi, w_r_lo, b_r]
    in_specs = ([row(d), row(d), row(2 * d), pl.BlockSpec((nb, tt, d), lambda i: (0, i, 0))]
                + [full(a) for a in args[4:]])
    body = functools.partial(_mixer_body, n_sub=MIXER_SUB)
    return pl.pallas_call(
        body,
        grid=(t // tm,),
        in_specs=in_specs,
        out_specs=[row(d), row(d), row(LANES)],
        out_shape=[jax.ShapeDtypeStruct((t, d), F32), jax.ShapeDtypeStruct((t, d), F32),
                   jax.ShapeDtypeStruct((t, LANES), F32)],
        compiler_params=_cparams("parallel"),
        name="mixer_router",
    )(*args)


def _experts_body(be_ref, nu_ref, x_ref, wg_ref, wu_ref, wd_ref, y_ref, wg_s, wu_s, wd_s):
    i = pl.program_id(0)
    used = i < nu_ref[0]
    new_expert = jnp.logical_or(i == 0, be_ref[i] != be_ref[jnp.maximum(i - 1, 0)])

    @pl.when(jnp.logical_and(used, new_expert))
    def _():
        wg_s[...] = wg_ref[0].astype(BF16)
        wu_s[...] = wu_ref[0].astype(BF16)
        wd_s[...] = wd_ref[0].astype(BF16)

    @pl.when(used)
    def _():
        x = x_ref[...].astype(BF16)
        hg = jnp.dot(x, wg_s[...], preferred_element_type=F32)
        hu = jnp.dot(x, wu_s[...], preferred_element_type=F32)
        y_ref[...] = _bdot(_silu(hg) * hu, wd_s[...]).astype(y_ref.dtype)

    @pl.when(jnp.logical_not(used))
    def _():
        y_ref[...] = jnp.zeros_like(y_ref)


def _experts(x_pad, block_expert, n_used, w_gate, w_up, w_down, tm):
    rows, d = x_pad.shape
    de = w_gate.shape[2]
    nblk = rows // tm
    wspec = lambda k, n: pl.BlockSpec((1, k, n), lambda i, be, nu: (be[i], 0, 0))
    return pl.pallas_call(
        _experts_body,
        grid_spec=pltpu.PrefetchScalarGridSpec(
            num_scalar_prefetch=2,
            grid=(nblk,),
            in_specs=[pl.BlockSpec((tm, d), lambda i, be, nu: (i, 0)),
                      wspec(d, de), wspec(d, de), wspec(de, d)],
            out_specs=pl.BlockSpec((tm, d), lambda i, be, nu: (i, 0)),
            scratch_shapes=[pltpu.VMEM((d, de), BF16), pltpu.VMEM((d, de), BF16), pltpu.VMEM((de, d), BF16)],
        ),
        out_shape=jax.ShapeDtypeStruct((rows, d), F32),
        compiler_params=_cparams("arbitrary"),
        name="experts",
    )(block_expert, n_used, x_pad, w_gate, w_up, w_down)


SC_CORES = 2
SC_SUBCORES = 16
SC_CHUNK = 64


def _sc_gather_rows(table, idx):
    n_rows, d = table.shape
    n_out = idx.shape[0]
    workers = SC_CORES * SC_SUBCORES
    per_w = n_out // workers
    assert n_out % (8 * workers) == 0 and per_w % SC_CHUNK == 0 and table.dtype == F32
    n_chunks = per_w // SC_CHUNK
    mesh = plsc.VectorSubcoreMesh(core_axis_name="c", subcore_axis_name="s")

    @functools.partial(
        pl.kernel, mesh=mesh,
        out_type=jax.ShapeDtypeStruct((n_out, d), F32),
        scratch_types=[pltpu.VMEM((SC_CHUNK,), jnp.int32), pltpu.VMEM((SC_CHUNK, d), F32),
                       pltpu.SemaphoreType.DMA])
    def gather(table_hbm, idx_hbm, out_hbm, idx_v, rows_v, sem):
        wid = lax.axis_index("s") * SC_CORES + lax.axis_index("c")
        base = wid * per_w

        @pl.loop(0, n_chunks)
        def _(j):
            off = pl.multiple_of(base + j * SC_CHUNK, 8)
            pltpu.sync_copy(idx_hbm.at[pl.ds(off, SC_CHUNK)], idx_v)
            pltpu.async_copy(table_hbm.at[idx_v], rows_v, sem).wait()
            pltpu.sync_copy(rows_v, out_hbm.at[pl.ds(off, SC_CHUNK)])

    return gather(table, idx)


def _final_body(x1_ref, y1_ref, y2_ref, rt_ref, g2_ref, lg_ref, lb_ref, o_ref):
    tm, d = x1_ref.shape
    nb = g2_ref.shape[0]
    y = rt_ref[:, 2:3] * y1_ref[...].astype(F32) + rt_ref[:, 3:4] * y2_ref[...].astype(F32)
    gy = (y.reshape(tm // nb, nb, d) * g2_ref[...][None]).reshape(tm, d)
    out = _ln(DN_ALPHA * x1_ref[...] + gy) * lg_ref[...] + lb_ref[...]
    o_ref[...] = jnp.swapaxes(out.reshape(tm // nb, nb, d), 0, 1)


def _final(x1_tb, y1, y2, route, g2, ln_g, ln_b, tq):
    t, d = x1_tb.shape
    nb = g2.shape[0]
    length = t // nb
    tm = tq * nb
    row = lambda n: pl.BlockSpec((tm, n), lambda i: (i, 0))
    return pl.pallas_call(
        _final_body,
        grid=(length // tq,),
        in_specs=[row(d), row(d), row(d), row(LANES),
                  pl.BlockSpec((nb, d), lambda i: (0, 0)),
                  pl.BlockSpec((1, d), lambda i: (0, 0)),
                  pl.BlockSpec((1, d), lambda i: (0, 0))],
        out_specs=pl.BlockSpec((nb, tq, d), lambda i: (0, i, 0)),
        out_shape=jax.ShapeDtypeStruct((nb, length, d), F32),
        compiler_params=_cparams("parallel"),
        name="moe_combine_postnorm",
    )(x1_tb, y1, y2, route, g2, ln_g, ln_b)


MOE_TM = 512
GDN_GB = 4
MIXER_SUB = 2


def _route_plan(e1, e2, tm):
    t = e1.shape[0]
    e_flat = jnp.concatenate([e1, e2])
    n_assign = e_flat.shape[0]
    experts = jnp.arange(N_EXPERTS, dtype=jnp.int32)
    onehot = (e_flat[:, None] == experts[None, :]).astype(jnp.int32)
    csum = jnp.cumsum(onehot, axis=0)
    rank = jnp.sum(jnp.where(onehot > 0, csum - 1, 0), axis=1)
    counts = csum[-1]
    starts = jnp.cumsum(counts) - counts
    padded = (counts + tm - 1) // tm * tm
    pends = jnp.cumsum(padded)
    pstarts = pends - padded
    dest = pstarts[e_flat] + rank
    n_blocks = (n_assign + N_EXPERTS * (tm - 1) + tm - 1) // tm
    n_used = (pends[-1] // tm).astype(jnp.int32).reshape(1)
    blk0 = jnp.arange(n_blocks, dtype=jnp.int32) * tm
    block_expert = jnp.minimum(jnp.sum((pends[None, :] <= blk0[:, None]).astype(jnp.int32), axis=1),
                               N_EXPERTS - 1).astype(jnp.int32)
    order = jnp.argsort(e_flat, stable=True).astype(jnp.int32)
    p = jnp.arange(n_blocks * tm, dtype=jnp.int32)
    pe = jnp.repeat(block_expert, tm)
    r = p - pstarts[pe]
    valid = r < counts[pe]
    a_idx = jnp.clip(starts[pe] + r, 0, n_assign - 1)
    src = jnp.where(valid, order.at[a_idx].get(mode="promise_in_bounds") % t, p % t)
    return src, dest[:t], dest[t:], block_expert, n_used


def kernel(x, c, ctx, c_ctx, w_mod, b_mod, w_in, b_in, conv_a_w, conv_a_b, lru_wa, lru_ba, lru_wx, lru_bx,
           lru_lambda, conv_qkv_w, gdn_a_log, gdn_dt_bias, gdn_norm_w, w_pa, w_pb, w_out, ln1_g, ln1_b,
           w_router_g, b_router_g, w_router_e, b_router_e, w_e_gate, w_e_up, w_e_down, ln2_g, ln2_b):
    nb, n_lat, d = x.shape
    n_ctx = ctx.shape[1]
    rows = n_lat // GRID_W
    assert rows == GDN_CHUNK and n_ctx % GDN_CHUNK == 0 and w_mod.shape[0] == 1
    layer = 0
    lp = {"conv_a_w": conv_a_w[layer], "conv_a_b": conv_a_b[layer], "lru_wa": lru_wa[layer],
          "lru_ba": lru_ba[layer], "lru_wx": lru_wx[layer], "lru_bx": lru_bx[layer],
          "lru_lambda": lru_lambda[layer]}

    pad_rows = (-(nb + 1)) % 8
    cc = jnp.concatenate([c, c_ctx[None, :], jnp.zeros((pad_rows, d), F32)], axis=0)
    mod = _mod_vectors(cc, w_mod[layer], b_mod[layer])
    sh1, sc1, g1, sh2, sc2, g2 = [mod[:nb, j * d:(j + 1) * d] for j in range(6)]
    csh1 = jnp.broadcast_to(mod[nb:nb + 1, 0:d], (nb, d))
    csc1 = jnp.broadcast_to(mod[nb:nb + 1, d:2 * d], (nb, d))

    w_l = w_in[layer]
    b_l = b_in[layer]
    w_r = jnp.concatenate([w_l[:, OFF_XA:OFF_GDN], w_l[:, OFF_MG:]], axis=1).astype(BF16)
    b_r = jnp.concatenate([b_l[OFF_XA:OFF_GDN], b_l[OFF_MG:]])[None, :]
    gate_pad = LANES - N_GATES
    w_g = jnp.concatenate([w_l[:, OFF_GDN:OFF_GDN + QKV_COLS],
                           jnp.pad(w_l[:, OFF_GDN + QKV_COLS:OFF_Z], ((0, 0), (0, gate_pad))),
                           w_l[:, OFF_Z:OFF_MG]], axis=1).astype(BF16)
    b_g = jnp.concatenate([b_l[OFF_GDN:OFF_GDN + QKV_COLS],
                           jnp.pad(b_l[OFF_GDN + QKV_COLS:OFF_Z], (0, gate_pad)),
                           b_l[OFF_Z:OFF_MG]])[None, :]
    n_qg = QKV_COLS + LANES

    (xa_ctx,) = _inproj_raster(ctx, csc1, csh1, w_r[:, :LRU_WIDTH], b_r[:, :LRU_WIDTH],
                               [(0, LRU_WIDTH)], tt=32)
    zero_lru = jnp.zeros((nb, LRU_WIDTH), F32)
    (sa_f,) = _lru_direction(xa_ctx, lp, 0, zero_lru, False, "none", tt=32)
    (sa_b,) = _lru_direction(xa_ctx, lp, 1, zero_lru, True, "none", tt=32)

    nc_ctx = n_ctx // GDN_CHUNK
    qkv_c, lg_c = _inproj_column(ctx, lambda ci, g: (g, ci, 0), nc_ctx, csc1, csh1,
                                 w_g[:, :n_qg], b_g[:, :n_qg], [(0, QKV_COLS), (QKV_COLS, n_qg)])
    a_row = jnp.pad(-jnp.exp(gdn_a_log[layer].reshape(-1)), (0, LANES - 2 * GDN_HEADS))[None, :]
    dt_row = jnp.pad(gdn_dt_bias[layer].reshape(-1), (0, LANES - 2 * GDN_HEADS))[None, :]
    cw_qkv = conv_qkv_w[layer]
    qkv_c, gates_c = _gdn_prep(qkv_c, lg_c, cw_qkv, a_row, dt_row, GDN_GB)
    zero_gdn = jnp.zeros((nb, GDN_HEADS, GDN_DK, GDN_DV), F32)
    (sb_f,) = _gdn_direction(qkv_c, gates_c, zero_gdn, False, False, GDN_GB)
    (sb_b,) = _gdn_direction(qkv_c, gates_c, zero_gdn, True, False, GDN_GB)

    xa, ga, mg = _inproj_raster(x, sc1, sh1, w_r, b_r,
                                [(0, LRU_WIDTH), (LRU_WIDTH, 2 * LRU_WIDTH), (2 * LRU_WIDTH, 2 * LRU_WIDTH + 2 * d)],
                                tt=32)
    h_f, _ = _lru_direction(xa, lp, 0, sa_f, False, "h", tt=32)
    ya, _ = _lru_direction(xa, lp, 1, sa_b, True, "ya", tt=32, hf=h_f, ga=ga)

    qkv_raw, lg_l, z_l = _inproj_column_lat(x, sc1, sh1, w_g, b_g)
    qkv_l = _gdn_prep_lat(qkv_raw, cw_qkv)
    gates_l = _gdn_gates(lg_l, a_row, dt_row)
    o_f, _ = _gdn_direction(qkv_l, gates_l, sb_f, False, True, GDN_GB)
    o_b, _ = _gdn_direction(qkv_l, gates_l, sb_b, True, True, GDN_GB)
    yb = _gdn_out(o_f, o_b, z_l, gdn_norm_w[layer])

    w_rt = jnp.pad(jnp.concatenate([w_router_g[layer], w_router_e[layer]], axis=1),
                   ((0, 0), (0, LANES - N_GROUPS - N_EXPERTS)))
    b_rt = jnp.pad(jnp.concatenate([b_router_g[layer], b_router_e[layer]]),
                   (0, LANES - N_GROUPS - N_EXPERTS))[None, :]
    w_rt_hi = w_rt.astype(BF16)
    w_rt_lo = (w_rt - w_rt_hi.astype(F32)).astype(BF16)
    x1, h2, route = _mixer(ya, yb, mg, x, g1, sh2, sc2, w_pa[layer].astype(BF16), w_pb[layer].astype(BF16),
                           w_out[layer].astype(BF16), ln1_g[layer][None, :], ln1_b[layer][None, :],
                           w_rt_hi, w_rt_lo, b_rt, tt=32)

    e1 = route[:, 0].astype(jnp.int32)
    e2 = route[:, 1].astype(jnp.int32)
    src, dest1, dest2, block_expert, n_used = _route_plan(e1, e2, MOE_TM)
    x_pad = _sc_gather_rows(h2, src)
    y_pad = _experts(x_pad, block_expert, n_used, w_e_gate[layer], w_e_up[layer], w_e_down[layer], MOE_TM)
    y1 = _sc_gather_rows(y_pad, dest1)
    y2 = _sc_gather_rows(y_pad, dest2)
    return _final(x1, y1, y2, route, g2, ln2_g[layer][None, :], ln2_b[layer][None, :], tq=32)
```

```python
import functools
import math

import jax
import jax.numpy as jnp
from jax import lax
from jax.experimental import pallas as pl
from jax.experimental.pallas import tpu as pltpu
from jax.experimental.pallas import tpu_sc as plsc

F32 = jnp.float32
BF16 = jnp.bfloat16

D_MODEL = 1024
GRID_W = 64
LRU_WIDTH = 1024
LRU_BLOCKS = 8
LRU_BLOCK = LRU_WIDTH // LRU_BLOCKS
LRU_C = 8.0
CONV_W = 4
GDN_HEADS = 8
GDN_DK = 128
GDN_DV = 128
GDN_QK = GDN_HEADS * GDN_DK
GDN_VW = GDN_HEADS * GDN_DV
GDN_CHUNK = 64
QKV_COLS = 2 * GDN_QK + GDN_VW
N_GATES = 4 * GDN_HEADS
OFF_XA = 0
OFF_GA = OFF_XA + LRU_WIDTH
OFF_GDN = OFF_GA + LRU_WIDTH
OFF_Z = OFF_GDN + QKV_COLS + N_GATES
OFF_MG = OFF_Z + GDN_VW
N_GROUPS = 4
EXP_PER_GROUP = 8
N_EXPERTS = N_GROUPS * EXP_PER_GROUP
TOP_K = 2
LN_EPS = 1e-6
L2_EPS = 1e-6
DEPTH = 1
DN_ALPHA = (2.0 * DEPTH) ** 0.25

LANES = 128
VMEM_LIMIT = 56 * 1024 * 1024

HI = lax.Precision.HIGHEST


def _cparams(*sem):
    return pltpu.CompilerParams(dimension_semantics=sem, vmem_limit_bytes=VMEM_LIMIT)


def _bdot(a, b):
    return jnp.dot(a.astype(BF16), b.astype(BF16), preferred_element_type=F32)


def _bdot_nt(a, b):
    return lax.dot_general(a.astype(BF16), b.astype(BF16), (((1,), (1,)), ((), ())),
                           preferred_element_type=F32)


def _ln(x):
    mu = jnp.mean(x, axis=-1, keepdims=True)
    xc = x - mu
    var = jnp.mean(xc * xc, axis=-1, keepdims=True)
    return xc * lax.rsqrt(var + LN_EPS)


def _sigmoid(x):
    return 0.5 * jnp.tanh(0.5 * x) + 0.5


def _silu(x):
    return x * _sigmoid(x)


def _softplus(x):
    return jnp.maximum(x, 0.0) + jnp.log(1.0 + jnp.exp(-jnp.abs(x)))


def _gelu_tanh(x):
    return 0.5 * x * (1.0 + jnp.tanh(math.sqrt(2.0 / math.pi) * (x + 0.044715 * (x * x * x))))


def _mod_body(c_ref, w_ref, b_ref, o_ref):
    o_ref[...] = jnp.dot(_silu(c_ref[...]), w_ref[...], preferred_element_type=F32,
                         precision=HI) + b_ref[...]


def _mod_vectors(cc, w_mod, b_mod):
    rows, d = cc.shape
    n = w_mod.shape[1]
    tn = 1536
    return pl.pallas_call(
        _mod_body,
        grid=(n // tn,),
        in_specs=[pl.BlockSpec((rows, d), lambda j: (0, 0)),
                  pl.BlockSpec((d, tn), lambda j: (0, j)),
                  pl.BlockSpec((1, tn), lambda j: (0, j))],
        out_specs=pl.BlockSpec((rows, tn), lambda j: (0, j)),
        out_shape=jax.ShapeDtypeStruct((rows, n), F32),
        compiler_params=_cparams("arbitrary"),
        name="mod_vectors",
    )(cc, w_mod, b_mod.reshape(1, n))


def _inproj_r_body(x_ref, sc_ref, sh_ref, w_ref, b_ref, *o_refs, splits):
    nb, tt, d = x_ref.shape
    tm = nb * tt
    xn = _ln(jnp.swapaxes(x_ref[...], 0, 1).reshape(tm, d)).reshape(tt, nb, d)
    xm = (xn * (1.0 + sc_ref[...])[None] + sh_ref[...][None]).reshape(tm, d).astype(BF16)
    for o_ref, (lo, hi) in zip(o_refs, splits):
        for n0 in range(lo, hi, 512):
            o_ref[:, n0 - lo:n0 - lo + 512] = (
                jnp.dot(xm, w_ref[:, n0:n0 + 512], preferred_element_type=F32) + b_ref[:, n0:n0 + 512])


def _inproj_raster(x, sc, sh, w, b, splits, tt):
    nb, length, d = x.shape
    n = w.shape[1]
    tm = tt * nb
    body = functools.partial(_inproj_r_body, splits=splits)
    return pl.pallas_call(
        body,
        grid=(length // tt,),
        in_specs=[pl.BlockSpec((nb, tt, d), lambda i: (0, i, 0)),
                  pl.BlockSpec((nb, d), lambda i: (0, 0)),
                  pl.BlockSpec((nb, d), lambda i: (0, 0)),
                  pl.BlockSpec((d, n), lambda i: (0, 0)),
                  pl.BlockSpec((1, n), lambda i: (0, 0))],
        out_specs=[pl.BlockSpec((tm, hi - lo), lambda i: (i, 0)) for lo, hi in splits],
        out_shape=[jax.ShapeDtypeStruct((length * nb, hi - lo), F32) for lo, hi in splits],
        compiler_params=_cparams("parallel"),
        name="inproj_raster",
    )(x, sc, sh, w, b)


def _inproj_g_body(x_ref, sc_ref, sh_ref, w_ref, b_ref, *o_refs, splits):
    gb, ch, d = x_ref.shape
    parts = []
    for i in range(gb):
        xn = _ln(x_ref[i])
        parts.append((xn * (1.0 + sc_ref[i:i + 1, :]) + sh_ref[i:i + 1, :]).astype(BF16))
    xm = jnp.concatenate(parts, axis=0)
    for o_ref, (lo, hi) in zip(o_refs, splits):
        step = 512 if (hi - lo) % 512 == 0 else hi - lo
        for n0 in range(lo, hi, step):
            res = jnp.dot(xm, w_ref[:, n0:n0 + step], preferred_element_type=F32) + b_ref[:, n0:n0 + step]
            for i in range(gb):
                o_ref[0, i, :, n0 - lo:n0 - lo + step] = res[i * ch:(i + 1) * ch]


def _inproj_column(xv, x_index_map, n_chunks, sc, sh, w, b, splits):
    nb, d = sc.shape
    gb = 8
    n = w.shape[1]
    body = functools.partial(_inproj_g_body, splits=splits)
    return pl.pallas_call(
        body,
        grid=(n_chunks, nb // gb),
        in_specs=[pl.BlockSpec((gb, GDN_CHUNK, d), x_index_map),
                  pl.BlockSpec((gb, d), lambda c, g: (g, 0)),
                  pl.BlockSpec((gb, d), lambda c, g: (g, 0)),
                  pl.BlockSpec((d, n), lambda c, g: (0, 0)),
                  pl.BlockSpec((1, n), lambda c, g: (0, 0))],
        out_specs=[pl.BlockSpec((1, gb, GDN_CHUNK, hi - lo), lambda c, g: (c, g, 0, 0)) for lo, hi in splits],
        out_shape=[jax.ShapeDtypeStruct((n_chunks, nb, GDN_CHUNK, hi - lo), F32) for lo, hi in splits],
        compiler_params=_cparams("parallel", "parallel"),
        name="inproj_column",
    )(xv, sc, sh, w, b)


def _inproj_gl_body(x_ref, sc_ref, sh_ref, w_ref, b_ref, qkv_ref, lg_ref, z_ref):
    _, tm, d = x_ref.shape
    rr = tm // GRID_W
    xm = _ln(x_ref[0]) * (1.0 + sc_ref[0]) + sh_ref[0]
    xb = xm.astype(BF16)
    for n0 in range(0, QKV_COLS, 512):
        qkv_ref[0, :, n0:n0 + 512] = (jnp.dot(xb, w_ref[:, n0:n0 + 512], preferred_element_type=F32)
                                      + b_ref[:, n0:n0 + 512])
    xs = jnp.swapaxes(xm.reshape(rr, GRID_W, d), 0, 1).reshape(tm, d).astype(BF16)
    n_qg = QKV_COLS + LANES
    lg = jnp.dot(xs, w_ref[:, QKV_COLS:n_qg], preferred_element_type=F32) + b_ref[:, QKV_COLS:n_qg]
    lg_ref[:, 0] = lg.reshape(GRID_W, rr, LANES)
    for n0 in range(0, GDN_VW, 512):
        res = (jnp.dot(xs, w_ref[:, n_qg + n0:n_qg + n0 + 512], preferred_element_type=F32)
               + b_ref[:, n_qg + n0:n_qg + n0 + 512])
        z_ref[:, 0, :, n0:n0 + 512] = res.reshape(GRID_W, rr, 512)


def _inproj_column_lat(x, sc, sh, w, b):
    nb, length, d = x.shape
    n = w.shape[1]
    rows = length // GRID_W
    rr = 8
    tm = rr * GRID_W
    vec = pl.BlockSpec((1, 1, d), lambda bi, ri: (bi, 0, 0))
    chunked = lambda m: pl.BlockSpec((GRID_W, 1, rr, m), lambda bi, ri: (0, bi, ri, 0))
    return pl.pallas_call(
        _inproj_gl_body,
        grid=(nb, rows // rr),
        in_specs=[pl.BlockSpec((1, tm, d), lambda bi, ri: (bi, ri, 0)),
                  vec, vec,
                  pl.BlockSpec((d, n), lambda bi, ri: (0, 0)),
                  pl.BlockSpec((1, n), lambda bi, ri: (0, 0))],
        out_specs=[pl.BlockSpec((1, tm, QKV_COLS), lambda bi, ri: (bi, ri, 0)), chunked(LANES), chunked(GDN_VW)],
        out_shape=[jax.ShapeDtypeStruct((nb, length, QKV_COLS), F32),
                   jax.ShapeDtypeStruct((GRID_W, nb, rows, LANES), F32),
                   jax.ShapeDtypeStruct((GRID_W, nb, rows, GDN_VW), F32)],
        compiler_params=_cparams("parallel", "parallel"),
        name="inproj_column_lat",
    )(x, sc.reshape(nb, 1, d), sh.reshape(nb, 1, d), w, b)


def _lru_body(*refs, tt, nb, reverse, emit):
    if emit == "ya":
        (xa_ref, prev_ref, next_ref, cw_ref, cb_ref, wa_ref, ba_ref, wx_ref, bx_ref, lam_ref, h0_ref,
         hf_ref, ga_ref, out_ref, hl_ref, a_s, b_s, h_s) = refs
    elif emit == "h":
        (xa_ref, prev_ref, next_ref, cw_ref, cb_ref, wa_ref, ba_ref, wx_ref, bx_ref, lam_ref, h0_ref,
         out_ref, hl_ref, a_s, b_s, h_s) = refs
    else:
        (xa_ref, prev_ref, next_ref, cw_ref, cb_ref, wa_ref, ba_ref, wx_ref, bx_ref, lam_ref, h0_ref,
         hl_ref, a_s, b_s, h_s) = refs
    step = pl.program_id(0)
    nsteps = pl.num_programs(0)
    blk = (nsteps - 1 - step) if reverse else step
    rows = tt * nb

    @pl.when(step == 0)
    def _():
        h_s[...] = h0_ref[...]

    prev = jnp.where(blk == 0, 0.0, prev_ref[...])
    nxt = jnp.where(blk == nsteps - 1, 0.0, next_ref[...])
    ext = jnp.concatenate([prev, xa_ref[...], nxt], axis=0)
    u = (cw_ref[0:1, :] * ext[0:rows] + cw_ref[1:2, :] * ext[nb:rows + nb]
         + cw_ref[2:3, :] * ext[2 * nb:rows + 2 * nb] + cw_ref[3:4, :] * ext[3 * nb:rows + 3 * nb]
         + cb_ref[...])
    c_all = (-0.5 * LRU_C) * _softplus(-lam_ref[...])
    for n in range(LRU_BLOCKS):
        sl = slice(n * LRU_BLOCK, (n + 1) * LRU_BLOCK)
        un = u[:, sl]
        ub = un.astype(BF16)
        t_r = jnp.tanh(jnp.dot(ub, wa_ref[n], preferred_element_type=F32) + ba_ref[:, sl])
        t_i = jnp.tanh(jnp.dot(ub, wx_ref[n], preferred_element_type=F32) + bx_ref[:, sl])
        c = c_all[:, sl]
        a = jnp.exp(c * t_r + c)
        hun = 0.5 * un
        a_s[:, sl] = a
        b_s[:, sl] = jnp.sqrt(1.0 - a * a) * (t_i * hun + hun)

    def scan_step(j, h):
        t = (tt - 1 - j) if reverse else j
        r0 = pl.multiple_of(t * nb, nb)
        h = a_s[pl.ds(r0, nb), :] * h + b_s[pl.ds(r0, nb), :]
        if emit == "h":
            out_ref[pl.ds(r0, nb), :] = h
        elif emit == "ya":
            b_s[pl.ds(r0, nb), :] = h
        return h

    h = lax.fori_loop(0, tt, scan_step, h_s[...], unroll=4)
    h_s[...] = h
    hl_ref[...] = h
    if emit == "ya":
        out_ref[...] = (_gelu_tanh(ga_ref[...]) * (hf_ref[...] + b_s[...])).astype(BF16)


def _lru_direction(xa, lp, di, h0, reverse, emit, tt, hf=None, ga=None):
    t, w = xa.shape
    nb = h0.shape[0]
    rows = tt * nb
    nblk = t // rows
    assert rows % (2 * nb) == 0
    prev_per = rows // nb
    next_per = rows // (2 * nb)
    n_prev = t // nb
    n_next = t // (2 * nb)

    def bi(i):
        return (nblk - 1 - i) if reverse else i

    const2 = lambda i: (0, 0)
    in_specs = [
        pl.BlockSpec((rows, w), lambda i: (bi(i), 0)),
        pl.BlockSpec((nb, w), lambda i: (jnp.maximum(bi(i) * prev_per - 1, 0), 0)),
        pl.BlockSpec((2 * nb, w), lambda i: (jnp.minimum((bi(i) + 1) * next_per, n_next - 1), 0)),
        pl.BlockSpec((CONV_W, w), const2),
        pl.BlockSpec((1, w), const2),
        pl.BlockSpec((LRU_BLOCKS, LRU_BLOCK, LRU_BLOCK), lambda i: (0, 0, 0)),
        pl.BlockSpec((1, w), const2),
        pl.BlockSpec((LRU_BLOCKS, LRU_BLOCK, LRU_BLOCK), lambda i: (0, 0, 0)),
        pl.BlockSpec((1, w), const2),
        pl.BlockSpec((1, w), const2),
        pl.BlockSpec((nb, w), const2),
    ]
    args = [xa, xa, xa, lp["conv_a_w"], lp["conv_a_b"].reshape(1, w),
            (0.5 * lp["lru_wa"][di]).astype(BF16), (0.5 * lp["lru_ba"][di]).reshape(1, w),
            (0.5 * lp["lru_wx"][di]).astype(BF16), (0.5 * lp["lru_bx"][di]).reshape(1, w),
            lp["lru_lambda"][di].reshape(1, w), h0]
    out_specs = []
    out_shape = []
    if emit == "ya":
        in_specs += [pl.BlockSpec((rows, w), lambda i: (bi(i), 0)),
                     pl.BlockSpec((rows, w), lambda i: (bi(i), 0))]
        args += [hf, ga]
        out_specs.append(pl.BlockSpec((rows, w), lambda i: (bi(i), 0)))
        out_shape.append(jax.ShapeDtypeStruct((t, w), BF16))
    elif emit == "h":
        out_specs.append(pl.BlockSpec((rows, w), lambda i: (bi(i), 0)))
        out_shape.append(jax.ShapeDtypeStruct((t, w), F32))
    out_specs.append(pl.BlockSpec((nb, w), const2))
    out_shape.append(jax.ShapeDtypeStruct((nb, w), F32))
    body = functools.partial(_lru_body, tt=tt, nb=nb, reverse=reverse, emit=emit)
    return pl.pallas_call(
        body,
        grid=(nblk,),
        in_specs=in_specs,
        out_specs=out_specs,
        out_shape=out_shape,
        scratch_shapes=[pltpu.VMEM((rows, w), F32), pltpu.VMEM((rows, w), F32), pltpu.VMEM((nb, w), F32)],
        compiler_params=_cparams("arbitrary"),
        name="lru_" + ("bwd" if reverse else "fwd") + "_" + emit,
    )(*args)


def _gdn_prep_body(qkv_ref, prev_ref, next_ref, lg_ref, cw_ref, ga_ref, gd_ref, qkv_o, g_o):
    c = pl.program_id(0)
    nc = pl.num_programs(0)
    ch = GDN_CHUNK
    gb = qkv_ref.shape[1]
    for i in range(gb):
        prev = jnp.where(c == 0, 0.0, prev_ref[0, i])
        nxt = jnp.where(c == nc - 1, 0.0, next_ref[0, i])
        ext = jnp.concatenate([prev, qkv_ref[0, i], nxt], axis=0)
        u = (cw_ref[0:1, :] * ext[7:7 + ch] + cw_ref[1:2, :] * ext[8:8 + ch]
             + cw_ref[2:3, :] * ext[9:9 + ch] + cw_ref[3:4, :] * ext[10:10 + ch])
        act = _silu(u)
        for h in range(GDN_HEADS):
            sq = slice(h * GDN_DK, (h + 1) * GDN_DK)
            qh = act[:, sq]
            qn = qh * lax.rsqrt(jnp.sum(qh * qh, axis=-1, keepdims=True) + L2_EPS) * (GDN_DK ** -0.5)
            qkv_o[0, i, :, sq] = qn.astype(qkv_o.dtype)
            sk = slice(GDN_QK + h * GDN_DK, GDN_QK + (h + 1) * GDN_DK)
            kh = act[:, sk]
            kn = kh * lax.rsqrt(jnp.sum(kh * kh, axis=-1, keepdims=True) + L2_EPS)
            qkv_o[0, i, :, sk] = kn.astype(qkv_o.dtype)
        qkv_o[0, i, :, 2 * GDN_QK:] = act[:, 2 * GDN_QK:].astype(qkv_o.dtype)
        lg = lg_ref[0, i]
        lane = lax.broadcasted_iota(jnp.int32, lg.shape, 1)
        decay = ga_ref[...] * _softplus(lg + gd_ref[...])
        g_o[0, i] = jnp.where(lane < 2 * GDN_HEADS, decay, jnp.where(lane < N_GATES, _sigmoid(lg), 0.0))


def _gdn_prep(qkv, lg, conv_w, a_row, dt_row, gb):
    nc, nb, ch, n = qkv.shape
    hb = ch // 8
    return pl.pallas_call(
        _gdn_prep_body,
        grid=(nc, nb // gb),
        in_specs=[pl.BlockSpec((1, gb, ch, n), lambda c, b: (c, b, 0, 0)),
                  pl.BlockSpec((1, gb, 8, n), lambda c, b: (jnp.maximum(c - 1, 0), b, hb - 1, 0)),
                  pl.BlockSpec((1, gb, 8, n), lambda c, b: (jnp.minimum(c + 1, nc - 1), b, 0, 0)),
                  pl.BlockSpec((1, gb, ch, LANES), lambda c, b: (c, b, 0, 0)),
                  pl.BlockSpec((CONV_W, n), lambda c, b: (0, 0)),
                  pl.BlockSpec((1, LANES), lambda c, b: (0, 0)),
                  pl.BlockSpec((1, LANES), lambda c, b: (0, 0))],
        out_specs=[pl.BlockSpec((1, gb, ch, n), lambda c, b: (c, b, 0, 0)),
                   pl.BlockSpec((1, gb, ch, LANES), lambda c, b: (c, b, 0, 0))],
        out_shape=[jax.ShapeDtypeStruct(qkv.shape, BF16), jax.ShapeDtypeStruct(lg.shape, F32)],
        compiler_params=_cparams("parallel", "parallel"),
        name="gdn_prep",
    )(qkv, qkv, qkv, lg, conv_w, a_row, dt_row)


def _gdn_prep_lat_body(x_ref, prev_ref, next_ref, cw_ref, o_ref):
    ri = pl.program_id(1)
    nr = pl.num_programs(1)
    part = pl.program_id(2)
    gw = GRID_W
    main = x_ref[0]
    rows, n = main.shape
    prev = prev_ref[0]
    nxt = next_ref[0]
    zrow = jnp.zeros((1, n), F32)
    prev_wrap = jnp.concatenate([zrow, prev[:gw - 1]], axis=0)
    nxt_wrap = jnp.concatenate([nxt[1:gw], zrow, nxt[gw + 1:], zrow], axis=0)
    prev = jnp.where(ri == 0, prev_wrap, prev)
    nxt = jnp.where(ri == nr - 1, nxt_wrap, nxt)
    ext = jnp.concatenate([prev, main, nxt], axis=0)
    cwh = 0.5 * cw_ref[...]
    hu = (cwh[0:1, :] * ext[0:rows] + cwh[1:2, :] * ext[gw:gw + rows]
          + cwh[2:3, :] * ext[2 * gw:2 * gw + rows] + cwh[3:4, :] * ext[3 * gw:3 * gw + rows])
    act = hu * jnp.tanh(hu) + hu
    scale = jnp.where(part == 0, GDN_DK ** -0.5, 1.0)
    parts = []
    for h in range(n // GDN_DK):
        ah = act[:, h * GDN_DK:(h + 1) * GDN_DK]
        nh = ah * (lax.rsqrt(jnp.sum(ah * ah, axis=-1, keepdims=True) + L2_EPS) * scale)
        parts.append(jnp.where(part < 2, nh, ah))
    y = jnp.concatenate(parts, axis=1)
    o_ref[:, 0] = jnp.swapaxes(y.reshape(rows // gw, gw, n), 0, 1).astype(BF16)


def _gdn_prep_lat(qkv_raw, conv_w):
    nb, length, n3 = qkv_raw.shape
    gw = GRID_W
    rows = length // gw
    rr = 16
    n = GDN_QK
    assert n3 == 3 * n and rows % rr == 0
    return pl.pallas_call(
        _gdn_prep_lat_body,
        grid=(nb, rows // rr, 3),
        in_specs=[pl.BlockSpec((1, rr * gw, n), lambda b, ri, j: (b, ri, j)),
                  pl.BlockSpec((1, gw, n), lambda b, ri, j: (b, (ri * rr + rows - 1) % rows, j)),
                  pl.BlockSpec((1, 2 * gw, n), lambda b, ri, j: (b, (((ri + 1) * rr) % rows) // 2, j)),
                  pl.BlockSpec((CONV_W, n), lambda b, ri, j: (0, j))],
        out_specs=pl.BlockSpec((gw, 1, rr, n), lambda b, ri, j: (0, b, ri, j)),
        out_shape=jax.ShapeDtypeStruct((gw, nb, rows, n3), BF16),
        compiler_params=_cparams("parallel", "parallel", "parallel"),
        name="gdn_prep_lat",
    )(qkv_raw, qkv_raw, qkv_raw, conv_w)


def _gdn_gates_body(lg_ref, ga_ref, gd_ref, g_o):
    lg = lg_ref[...]
    lane = lax.broadcasted_iota(jnp.int32, lg.shape, lg.ndim - 1)
    decay = ga_ref[...] * _softplus(lg + gd_ref[...])
    g_o[...] = jnp.where(lane < 2 * GDN_HEADS, decay, jnp.where(lane < N_GATES, _sigmoid(lg), 0.0))


def _gdn_gates(lg, a_row, dt_row):
    nc, nb, ch, n = lg.shape
    cb = 8
    spec = pl.BlockSpec((cb, nb, ch, n), lambda c: (c, 0, 0, 0))
    vec = pl.BlockSpec((1, n), lambda c: (0, 0))
    return pl.pallas_call(
        _gdn_gates_body,
        grid=(nc // cb,),
        in_specs=[spec, vec, vec],
        out_specs=spec,
        out_shape=jax.ShapeDtypeStruct(lg.shape, F32),
        compiler_params=_cparams("parallel"),
        name="gdn_gates",
    )(lg, a_row, dt_row)


def _gdn_scan_body(qkv_ref, g_ref, s0_ref, *rest, reverse, emit_o):
    if emit_o:
        o_ref, sl_ref, s_s = rest
    else:
        sl_ref, s_s = rest
    step = pl.program_id(1)
    ch = GDN_CHUNK
    gb = qkv_ref.shape[1]
    n_sq = int(math.log2(ch)) - 1

    @pl.when(step == 0)
    def _():
        s_s[...] = s0_ref[...]

    row = lax.broadcasted_iota(jnp.int32, (ch, ch), 0)
    col = lax.broadcasted_iota(jnp.int32, (ch, ch), 1)
    incl = (row <= col) if reverse else (row >= col)
    strict = (row < col) if reverse else (row > col)
    eye = (row == col).astype(F32)
    tri = incl.astype(F32)
    goff = GDN_HEADS if reverse else 0
    boff = 2 * GDN_HEADS + goff

    units = [(i, h) for i in range(gb) for h in range(GDN_HEADS)]
    nu = len(units)
    gcum, gcum_t, gtot, gates = [], [], [], []
    for i in range(gb):
        g = g_ref[0, i]
        gates.append(g)
        cum = jnp.dot(tri, g, preferred_element_type=F32, precision=HI)
        gcum.append(cum)
        gcum_t.append(cum.T)
        gtot.append(jnp.sum(g, axis=0, keepdims=True))

    q, k, kb, xin, decay, eg, egl, gl = [], [], [], [], [], [], [], []
    for i, h in units:
        qh = qkv_ref[0, i, :, h * GDN_DK:(h + 1) * GDN_DK].astype(F32)
        kh = qkv_ref[0, i, :, GDN_QK + h * GDN_DK:GDN_QK + (h + 1) * GDN_DK].astype(F32)
        vh = qkv_ref[0, i, :, 2 * GDN_QK + h * GDN_DV:2 * GDN_QK + (h + 1) * GDN_DV].astype(F32)
        gc = jnp.broadcast_to(gcum[i][:, goff + h:goff + h + 1], (ch, GDN_DK))
        beta = jnp.broadcast_to(gates[i][:, boff + h:boff + h + 1], (ch, GDN_DK))
        gr = gcum_t[i][goff + h:goff + h + 1, :]
        gt = gtot[i][:, goff + h:goff + h + 1]
        decay.append(jnp.where(incl, jnp.exp(jnp.where(incl, gc[:, :ch] - gr, 0.0)), 0.0))
        e = jnp.exp(gc)
        kbh = kh * beta
        q.append(qh)
        k.append(kh)
        kb.append(kbh)
        eg.append(e)
        egl.append(jnp.exp(gt - gc))
        gl.append(jnp.exp(gt))
        xin.append(jnp.concatenate([kbh * e, vh * beta], axis=1).astype(BF16))

    gram = [_bdot_nt(jnp.concatenate([kb[j], q[j]], axis=0), k[j]) for j in range(nu)]
    nil = [-jnp.where(strict, gram[j][:ch] * decay[j], 0.0) for j in range(nu)]
    amat = [gram[j][ch:] * decay[j] for j in range(nu)]
    inv = [eye + m for m in nil]
    power = nil
    for _ in range(n_sq):
        pb = [p.astype(BF16) for p in power]
        power = [jnp.dot(p, p, preferred_element_type=F32) for p in pb]
        inv = [iv + _bdot(iv, p) for iv, p in zip(inv, power)]
    wu = [_bdot(iv, x) for iv, x in zip(inv, xin)]
    s_old = [s_s[i, h] for i, h in units]
    s_bf = [s.astype(BF16) for s in s_old]
    ws = [_bdot(jnp.concatenate([wu[j][:, :GDN_DK], q[j] * eg[j]], axis=0), s_bf[j]) for j in range(nu)]
    v_new = [(wu[j][:, GDN_DK:] - ws[j][:ch]).astype(BF16) for j in range(nu)]
    upd = [_bdot(jnp.concatenate([amat[j], (k[j] * egl[j]).T], axis=0), v_new[j]) for j in range(nu)]
    for j, (i, h) in enumerate(units):
        if emit_o:
            o_ref[0, i, :, h * GDN_DV:(h + 1) * GDN_DV] = (ws[j][ch:] + upd[j][:ch]).astype(o_ref.dtype)
        s_new = s_old[j] * gl[j] + upd[j][ch:]
        s_s[i, h] = s_new
        sl_ref[i, h] = s_new


def _gdn_direction(qkv, gates, s0, reverse, emit_o, gb=1):
    nc, nb, ch, n = qkv.shape

    def ci(c):
        return (nc - 1 - c) if reverse else c

    out_specs = []
    out_shape = []
    if emit_o:
        out_specs.append(pl.BlockSpec((1, gb, ch, GDN_VW), lambda b, c: (ci(c), b, 0, 0)))
        out_shape.append(jax.ShapeDtypeStruct((nc, nb, ch, GDN_VW), BF16))
    out_specs.append(pl.BlockSpec((gb, GDN_HEADS, GDN_DK, GDN_DV), lambda b, c: (b, 0, 0, 0)))
    out_shape.append(jax.ShapeDtypeStruct((nb, GDN_HEADS, GDN_DK, GDN_DV), F32))
    body = functools.partial(_gdn_scan_body, reverse=reverse, emit_o=emit_o)
    return pl.pallas_call(
        body,
        grid=(nb // gb, nc),
        in_specs=[pl.BlockSpec((1, gb, ch, n), lambda b, c: (ci(c), b, 0, 0)),
                  pl.BlockSpec((1, gb, ch, LANES), lambda b, c: (ci(c), b, 0, 0)),
                  pl.BlockSpec((gb, GDN_HEADS, GDN_DK, GDN_DV), lambda b, c: (b, 0, 0, 0))],
        out_specs=out_specs,
        out_shape=out_shape,
        scratch_shapes=[pltpu.VMEM((gb, GDN_HEADS, GDN_DK, GDN_DV), F32)],
        compiler_params=_cparams("parallel", "arbitrary"),
        name="gdn_scan_" + ("bwd" if reverse else "fwd"),
    )(qkv, gates, s0)


def _gdn_out_body(of_ref, ob_ref, z_ref, nw_ref, y_ref, y_s):
    nb = of_ref.shape[1]
    for i in range(nb):
        o = of_ref[0, i].astype(F32) + ob_ref[0, i].astype(F32)
        zg = _silu(z_ref[0, i])
        for h in range(GDN_HEADS):
            sl = slice(h * GDN_DV, (h + 1) * GDN_DV)
            oh = o[:, sl]
            nh = oh * lax.rsqrt(jnp.mean(oh * oh, axis=-1, keepdims=True) + LN_EPS) * nw_ref[...]
            y_s[i, :, sl] = nh * zg[:, sl]
    y_ref[:, 0] = jnp.swapaxes(y_s[...], 0, 1).astype(BF16)


def _gdn_out(o_f, o_b, z, norm_w):
    nc, nb, ch, vw = o_f.shape
    spec = pl.BlockSpec((1, nb, ch, vw), lambda c: (c, 0, 0, 0))
    y = pl.pallas_call(
        _gdn_out_body,
        grid=(nc,),
        in_specs=[spec, spec, spec, pl.BlockSpec((1, GDN_DV), lambda c: (0, 0))],
        out_specs=pl.BlockSpec((ch, 1, nb, vw), lambda c: (0, c, 0, 0)),
        out_shape=jax.ShapeDtypeStruct((ch, nc, nb, vw), BF16),
        scratch_shapes=[pltpu.VMEM((nb, ch, vw), F32)],
        compiler_params=_cparams("parallel"),
        name="gdn_out",
    )(o_f, o_b, z, norm_w.reshape(1, GDN_DV))
    return y.reshape(ch * nc * nb, vw)


def _mixer_body(ya_ref, yb_ref, mg_ref, x_ref, g1_ref, sh2_ref, sc2_ref, wpa_ref, wpb_ref, wo_ref,
                lg_ref, lb_ref, wrh_ref, wrl_ref, br_ref, x1_ref, h2_ref, rt_ref, *, n_sub):
    nb, tt, d = x_ref.shape
    ts = tt // n_sub
    rs = ts * nb
    subs = [slice(k * rs, (k + 1) * rs) for k in range(n_sub)]
    xs = [jnp.swapaxes(x_ref[:, k * ts:(k + 1) * ts, :], 0, 1).reshape(rs, d) for k in range(n_sub)]
    pa = [jnp.dot(ya_ref[s, :], wpa_ref[...], preferred_element_type=F32) for s in subs]
    pb = [jnp.dot(yb_ref[s, :], wpb_ref[...], preferred_element_type=F32) for s in subs]
    merged = [(_sigmoid(mg_ref[s, :d]) * a + _sigmoid(mg_ref[s, d:]) * b).astype(BF16)
              for s, a, b in zip(subs, pa, pb)]
    mix = [jnp.dot(m, wo_ref[...], preferred_element_type=F32) for m in merged]
    x1 = []
    for s, xk, mk in zip(subs, xs, mix):
        gm = (mk.reshape(ts, nb, d) * g1_ref[...][None]).reshape(rs, d)
        x1k = _ln(DN_ALPHA * xk + gm) * lg_ref[...] + lb_ref[...]
        x1_ref[s, :] = x1k
        x1.append(x1k)
    h2 = [(_ln(v).reshape(ts, nb, d) * (1.0 + sc2_ref[...])[None] + sh2_ref[...][None]).reshape(rs, d) for v in x1]
    h2_hi = [v.astype(BF16) for v in h2]
    for s, v in zip(subs, h2_hi):
        h2_ref[s, :] = v.astype(h2_ref.dtype)
    h2_lo = [(v - hi.astype(F32)).astype(BF16) for v, hi in zip(h2, h2_hi)]
    all_logits = [(jnp.dot(hi, wrh_ref[...], preferred_element_type=F32)
                   + jnp.dot(lo, wrh_ref[...], preferred_element_type=F32)
                   + jnp.dot(hi, wrl_ref[...], preferred_element_type=F32) + br_ref[...])
                  for hi, lo in zip(h2_hi, h2_lo)]
    for s, logits in zip(subs, all_logits):
        rt_ref[s, :] = _route(logits)


def _route(logits):
    lane = lax.broadcasted_iota(jnp.int32, logits.shape, 1).astype(F32)
    neg = jnp.float32(-jnp.inf)
    big = jnp.float32(1 << 20)
    is_g = lane < N_GROUPS
    gl = jnp.where(is_g, logits, neg)
    gmax = jnp.max(gl, axis=-1, keepdims=True)
    gsum = jnp.sum(jnp.where(is_g, jnp.exp(gl - gmax), 0.0), axis=-1, keepdims=True)
    p_group = 1.0 / gsum
    g_idx = jnp.min(jnp.where(gl == gmax, lane, big), axis=-1, keepdims=True)
    lo = N_GROUPS + g_idx * EXP_PER_GROUP
    in_g = (lane >= lo) & (lane < lo + EXP_PER_GROUP)
    el = jnp.where(in_g, logits, neg)
    m1 = jnp.max(el, axis=-1, keepdims=True)
    i1 = jnp.min(jnp.where(el == m1, lane, big), axis=-1, keepdims=True)
    el2 = jnp.where(lane == i1, neg, el)
    m2 = jnp.max(el2, axis=-1, keepdims=True)
    i2 = jnp.min(jnp.where(el2 == m2, lane, big), axis=-1, keepdims=True)
    esum = jnp.sum(jnp.where(in_g, jnp.exp(el - m1), 0.0), axis=-1, keepdims=True)
    p1 = 1.0 / esum
    p2 = jnp.exp(m2 - m1) / esum
    w1 = p_group * p1 / (p1 + p2)
    w2 = p_group * p2 / (p1 + p2)
    e1 = i1 - N_GROUPS
    e2 = i2 - N_GROUPS
    return jnp.where(lane == 0, e1, jnp.where(lane == 1, e2, jnp.where(lane == 2, w1,
                     jnp.where(lane == 3, w2, 0.0))))


def _mixer(ya, yb, mg, x, g1, sh2, sc2, w_pa, w_pb, w_out, ln_g, ln_b, w_r_hi, w_r_lo, b_r, tt):
    nb, length, d = x.shape
    t = nb * length
    tm = tt * nb
    row = lambda n: pl.BlockSpec((tm, n), lambda i: (i, 0))
    full = lambda a: pl.BlockSpec(a.shape, lambda i: (0,) * a.ndim)
    args = [ya, yb, mg, x, g1, sh2, sc2, w_pa, w_pb, w_out, ln_g, ln_b, w_r_hi, w_r_lo, b_r]
    in_specs = ([row(d), row(d), row(2 * d), pl.BlockSpec((nb, tt, d), lambda i: (0, i, 0))]
                + [full(a) for a in args[4:]])
    body = functools.partial(_mixer_body, n_sub=MIXER_SUB)
    return pl.pallas_call(
        body,
        grid=(t // tm,),
        in_specs=in_specs,
        out_specs=[row(d), row(d), row(LANES)],
        out_shape=[jax.ShapeDtypeStruct((t, d), F32), jax.ShapeDtypeStruct((t, d), F32),
                   jax.ShapeDtypeStruct((t, LANES), F32)],
        compiler_params=_cparams("parallel"),
        name="mixer_router",
    )(*args)


def _experts_body(be_ref, nu_ref, x_ref, wg_ref, wu_ref, wd_ref, y_ref, wg_s, wu_s, wd_s):
    i = pl.program_id(0)
    used = i < nu_ref[0]
    new_expert = jnp.logical_or(i == 0, be_ref[i] != be_ref[jnp.maximum(i - 1, 0)])

    @pl.when(jnp.logical_and(used, new_expert))
    def _():
        wg_s[...] = wg_ref[0].astype(BF16)
        wu_s[...] = wu_ref[0].astype(BF16)
        wd_s[...] = wd_ref[0].astype(BF16)

    @pl.when(used)
    def _():
        x = x_ref[...].astype(BF16)
        hg = jnp.dot(x, wg_s[...], preferred_element_type=F32)
        hu = jnp.dot(x, wu_s[...], preferred_element_type=F32)
        y_ref[...] = _bdot(_silu(hg) * hu, wd_s[...]).astype(y_ref.dtype)

    @pl.when(jnp.logical_not(used))
    def _():
        y_ref[...] = jnp.zeros_like(y_ref)


def _experts(x_pad, block_expert, n_used, w_gate, w_up, w_down, tm):
    rows, d = x_pad.shape
    de = w_gate.shape[2]
    nblk = rows // tm
    wspec = lambda k, n: pl.BlockSpec((1, k, n), lambda i, be, nu: (be[i], 0, 0))
    return pl.pallas_call(
        _experts_body,
        grid_spec=pltpu.PrefetchScalarGridSpec(
            num_scalar_prefetch=2,
            grid=(nblk,),
            in_specs=[pl.BlockSpec((tm, d), lambda i, be, nu: (i, 0)),
                      wspec(d, de), wspec(d, de), wspec(de, d)],
            out_specs=pl.BlockSpec((tm, d), lambda i, be, nu: (i, 0)),
            scratch_shapes=[pltpu.VMEM((d, de), BF16), pltpu.VMEM((d, de), BF16), pltpu.VMEM((de, d), BF16)],
        ),
        out_shape=jax.ShapeDtypeStruct((rows, d), F32),
        compiler_params=_cparams("arbitrary"),
        name="experts",
    )(block_expert, n_used, x_pad, w_gate, w_up, w_down)


SC_CORES = 2
SC_SUBCORES = 16
SC_CHUNK = 32


def _sc_gather_rows(table, idx):
    n_rows, d = table.shape
    n_out = idx.shape[0]
    workers = SC_CORES * SC_SUBCORES
    per_w = n_out // workers
    assert n_out % (8 * workers) == 0 and per_w % (2 * SC_CHUNK) == 0 and table.dtype == F32
    n_chunks = per_w // SC_CHUNK
    mesh = plsc.VectorSubcoreMesh(core_axis_name="c", subcore_axis_name="s")
    rows_buf = pltpu.VMEM((SC_CHUNK, d), F32)
    dma_sem = pltpu.SemaphoreType.DMA

    @functools.partial(
        pl.kernel, mesh=mesh,
        out_type=jax.ShapeDtypeStruct((n_out, d), F32),
        scratch_types=[pltpu.VMEM((per_w,), jnp.int32), rows_buf, rows_buf, dma_sem, dma_sem, dma_sem, dma_sem])
    def gather(table_hbm, idx_hbm, out_hbm, idx_v, rows_a, rows_b, gsem_a, gsem_b, wsem_a, wsem_b):
        wid = lax.axis_index("s") * SC_CORES + lax.axis_index("c")
        base = pl.multiple_of(wid * per_w, 8)
        pltpu.sync_copy(idx_hbm.at[pl.ds(base, per_w)], idx_v)

        @pl.loop(0, n_chunks, step=2)
        def _(j):
            off_a = pl.multiple_of(j * SC_CHUNK, 8)
            off_b = pl.multiple_of(off_a + SC_CHUNK, 8)
            g_a = pltpu.async_copy(table_hbm.at[idx_v.at[pl.ds(off_a, SC_CHUNK)]], rows_a, gsem_a)
            g_b = pltpu.async_copy(table_hbm.at[idx_v.at[pl.ds(off_b, SC_CHUNK)]], rows_b, gsem_b)
            g_a.wait()
            w_a = pltpu.async_copy(rows_a, out_hbm.at[pl.ds(base + off_a, SC_CHUNK)], wsem_a)
            g_b.wait()
            w_b = pltpu.async_copy(rows_b, out_hbm.at[pl.ds(base + off_b, SC_CHUNK)], wsem_b)
            w_a.wait()
            w_b.wait()

    return gather(table, idx)


def _final_body(x1_ref, y1_ref, y2_ref, rt_ref, g2_ref, lg_ref, lb_ref, o_ref):
    tm, d = x1_ref.shape
    nb = g2_ref.shape[0]
    y = rt_ref[:, 2:3] * y1_ref[...].astype(F32) + rt_ref[:, 3:4] * y2_ref[...].astype(F32)
    gy = (y.reshape(tm // nb, nb, d) * g2_ref[...][None]).reshape(tm, d)
    out = _ln(DN_ALPHA * x1_ref[...] + gy) * lg_ref[...] + lb_ref[...]
    o_ref[...] = jnp.swapaxes(out.reshape(tm // nb, nb, d), 0, 1)


def _final(x1_tb, y1, y2, route, g2, ln_g, ln_b, tq):
    t, d = x1_tb.shape
    nb = g2.shape[0]
    length = t // nb
    tm = tq * nb
    row = lambda n: pl.BlockSpec((tm, n), lambda i: (i, 0))
    return pl.pallas_call(
        _final_body,
        grid=(length // tq,),
        in_specs=[row(d), row(d), row(d), row(LANES),
                  pl.BlockSpec((nb, d), lambda i: (0, 0)),
                  pl.BlockSpec((1, d), lambda i: (0, 0)),
                  pl.BlockSpec((1, d), lambda i: (0, 0))],
        out_specs=pl.BlockSpec((nb, tq, d), lambda i: (0, i, 0)),
        out_shape=jax.ShapeDtypeStruct((nb, length, d), F32),
        compiler_params=_cparams("parallel"),
        name="moe_combine_postnorm",
    )(x1_tb, y1, y2, route, g2, ln_g, ln_b)


MOE_TM = 512
GDN_GB = 4
MIXER_SUB = 2


def _route_plan(e1, e2, tm):
    t = e1.shape[0]
    e_flat = jnp.concatenate([e1, e2])
    n_assign = e_flat.shape[0]
    experts = jnp.arange(N_EXPERTS, dtype=jnp.int32)
    onehot = (e_flat[:, None] == experts[None, :]).astype(jnp.int32)
    csum = jnp.cumsum(onehot, axis=0)
    rank = jnp.sum(jnp.where(onehot > 0, csum - 1, 0), axis=1)
    counts = csum[-1]
    starts = jnp.cumsum(counts) - counts
    padded = (counts + tm - 1) // tm * tm
    pends = jnp.cumsum(padded)
    pstarts = pends - padded
    dest = pstarts[e_flat] + rank
    n_blocks = (n_assign + N_EXPERTS * (tm - 1) + tm - 1) // tm
    n_used = (pends[-1] // tm).astype(jnp.int32).reshape(1)
    blk0 = jnp.arange(n_blocks, dtype=jnp.int32) * tm
    block_expert = jnp.minimum(jnp.sum((pends[None, :] <= blk0[:, None]).astype(jnp.int32), axis=1),
                               N_EXPERTS - 1).astype(jnp.int32)
    order = jnp.argsort(e_flat, stable=True).astype(jnp.int32)
    p = jnp.arange(n_blocks * tm, dtype=jnp.int32)
    pe = jnp.repeat(block_expert, tm)
    r = p - pstarts[pe]
    valid = r < counts[pe]
    a_idx = jnp.clip(starts[pe] + r, 0, n_assign - 1)
    src = jnp.where(valid, order.at[a_idx].get(mode="promise_in_bounds") % t, p % t)
    return src, dest[:t], dest[t:], block_expert, n_used


def kernel(x, c, ctx, c_ctx, w_mod, b_mod, w_in, b_in, conv_a_w, conv_a_b, lru_wa, lru_ba, lru_wx, lru_bx,
           lru_lambda, conv_qkv_w, gdn_a_log, gdn_dt_bias, gdn_norm_w, w_pa, w_pb, w_out, ln1_g, ln1_b,
           w_router_g, b_router_g, w_router_e, b_router_e, w_e_gate, w_e_up, w_e_down, ln2_g, ln2_b):
    nb, n_lat, d = x.shape
    n_ctx = ctx.shape[1]
    rows = n_lat // GRID_W
    assert rows == GDN_CHUNK and n_ctx % GDN_CHUNK == 0 and w_mod.shape[0] == 1
    layer = 0
    lp = {"conv_a_w": conv_a_w[layer], "conv_a_b": conv_a_b[layer], "lru_wa": lru_wa[layer],
          "lru_ba": lru_ba[layer], "lru_wx": lru_wx[layer], "lru_bx": lru_bx[layer],
          "lru_lambda": lru_lambda[layer]}

    pad_rows = (-(nb + 1)) % 8
    cc = jnp.concatenate([c, c_ctx[None, :], jnp.zeros((pad_rows, d), F32)], axis=0)
    mod = _mod_vectors(cc, w_mod[layer], b_mod[layer])
    sh1, sc1, g1, sh2, sc2, g2 = [mod[:nb, j * d:(j + 1) * d] for j in range(6)]
    csh1 = jnp.broadcast_to(mod[nb:nb + 1, 0:d], (nb, d))
    csc1 = jnp.broadcast_to(mod[nb:nb + 1, d:2 * d], (nb, d))

    w_l = w_in[layer]
    b_l = b_in[layer]
    w_r = jnp.concatenate([w_l[:, OFF_XA:OFF_GDN], w_l[:, OFF_MG:]], axis=1).astype(BF16)
    b_r = jnp.concatenate([b_l[OFF_XA:OFF_GDN], b_l[OFF_MG:]])[None, :]
    gate_pad = LANES - N_GATES
    w_g = jnp.concatenate([w_l[:, OFF_GDN:OFF_GDN + QKV_COLS],
                           jnp.pad(w_l[:, OFF_GDN + QKV_COLS:OFF_Z], ((0, 0), (0, gate_pad))),
                           w_l[:, OFF_Z:OFF_MG]], axis=1).astype(BF16)
    b_g = jnp.concatenate([b_l[OFF_GDN:OFF_GDN + QKV_COLS],
                           jnp.pad(b_l[OFF_GDN + QKV_COLS:OFF_Z], (0, gate_pad)),
                           b_l[OFF_Z:OFF_MG]])[None, :]
    n_qg = QKV_COLS + LANES

    (xa_ctx,) = _inproj_raster(ctx, csc1, csh1, w_r[:, :LRU_WIDTH], b_r[:, :LRU_WIDTH],
                               [(0, LRU_WIDTH)], tt=32)
    zero_lru = jnp.zeros((nb, LRU_WIDTH), F32)
    (sa_f,) = _lru_direction(xa_ctx, lp, 0, zero_lru, False, "none", tt=32)
    (sa_b,) = _lru_direction(xa_ctx, lp, 1, zero_lru, True, "none", tt=32)

    nc_ctx = n_ctx // GDN_CHUNK
    qkv_c, lg_c = _inproj_column(ctx, lambda ci, g: (g, ci, 0), nc_ctx, csc1, csh1,
                                 w_g[:, :n_qg], b_g[:, :n_qg], [(0, QKV_COLS), (QKV_COLS, n_qg)])
    a_row = jnp.pad(-jnp.exp(gdn_a_log[layer].reshape(-1)), (0, LANES - 2 * GDN_HEADS))[None, :]
    dt_row = jnp.pad(gdn_dt_bias[layer].reshape(-1), (0, LANES - 2 * GDN_HEADS))[None, :]
    cw_qkv = conv_qkv_w[layer]
    qkv_c, gates_c = _gdn_prep(qkv_c, lg_c, cw_qkv, a_row, dt_row, GDN_GB)
    zero_gdn = jnp.zeros((nb, GDN_HEADS, GDN_DK, GDN_DV), F32)
    (sb_f,) = _gdn_direction(qkv_c, gates_c, zero_gdn, False, False, GDN_GB)
    (sb_b,) = _gdn_direction(qkv_c, gates_c, zero_gdn, True, False, GDN_GB)

    xa, ga, mg = _inproj_raster(x, sc1, sh1, w_r, b_r,
                                [(0, LRU_WIDTH), (LRU_WIDTH, 2 * LRU_WIDTH), (2 * LRU_WIDTH, 2 * LRU_WIDTH + 2 * d)],
                                tt=32)
    h_f, _ = _lru_direction(xa, lp, 0, sa_f, False, "h", tt=32)
    ya, _ = _lru_direction(xa, lp, 1, sa_b, True, "ya", tt=32, hf=h_f, ga=ga)

    qkv_raw, lg_l, z_l = _inproj_column_lat(x, sc1, sh1, w_g, b_g)
    qkv_l = _gdn_prep_lat(qkv_raw, cw_qkv)
    gates_l = _gdn_gates(lg_l, a_row, dt_row)
    o_f, _ = _gdn_direction(qkv_l, gates_l, sb_f, False, True, GDN_GB)
    o_b, _ = _gdn_direction(qkv_l, gates_l, sb_b, True, True, GDN_GB)
    yb = _gdn_out(o_f, o_b, z_l, gdn_norm_w[layer])

    w_rt = jnp.pad(jnp.concatenate([w_router_g[layer], w_router_e[layer]], axis=1),
                   ((0, 0), (0, LANES - N_GROUPS - N_EXPERTS)))
    b_rt = jnp.pad(jnp.concatenate([b_router_g[layer], b_router_e[layer]]),
                   (0, LANES - N_GROUPS - N_EXPERTS))[None, :]
    w_rt_hi = w_rt.astype(BF16)
    w_rt_lo = (w_rt - w_rt_hi.astype(F32)).astype(BF16)
    x1, h2, route = _mixer(ya, yb, mg, x, g1, sh2, sc2, w_pa[layer].astype(BF16), w_pb[layer].astype(BF16),
                           w_out[layer].astype(BF16), ln1_g[layer][None, :], ln1_b[layer][None, :],
                           w_rt_hi, w_rt_lo, b_rt, tt=32)

    e1 = route[:, 0].astype(jnp.int32)
    e2 = route[:, 1].astype(jnp.int32)
    src, dest1, dest2, block_expert, n_used = _route_plan(e1, e2, MOE_TM)
    x_pad = _sc_gather_rows(h2, src)
    y_pad = _experts(x_pad, block_expert, n_used, w_e_gate[layer], w_e_up[layer], w_e_down[layer], MOE_TM)
    y1 = _sc_gather_rows(y_pad, dest1)
    y2 = _sc_gather_rows(y_pad, dest2)
    return _final(x1, y1, y2, route, g2, ln2_g[layer][None, :], ln2_b[layer][None, :], tq=32)
```

```python
import functools
import math

import jax
import jax.numpy as jnp
from jax import lax
from jax.experimental import pallas as pl
from jax.experimental.pallas import tpu as pltpu
from jax.experimental.pallas import tpu_sc as plsc

F32 = jnp.float32
BF16 = jnp.bfloat16

D_MODEL = 1024
GRID_W = 64
LRU_WIDTH = 1024
LRU_BLOCKS = 8
LRU_BLOCK = LRU_WIDTH // LRU_BLOCKS
LRU_C = 8.0
CONV_W = 4
GDN_HEADS = 8
GDN_DK = 128
GDN_DV = 128
GDN_QK = GDN_HEADS * GDN_DK
GDN_VW = GDN_HEADS * GDN_DV
GDN_CHUNK = 64
QKV_COLS = 2 * GDN_QK + GDN_VW
N_GATES = 4 * GDN_HEADS
OFF_XA = 0
OFF_GA = OFF_XA + LRU_WIDTH
OFF_GDN = OFF_GA + LRU_WIDTH
OFF_Z = OFF_GDN + QKV_COLS + N_GATES
OFF_MG = OFF_Z + GDN_VW
N_GROUPS = 4
EXP_PER_GROUP = 8
N_EXPERTS = N_GROUPS * EXP_PER_GROUP
TOP_K = 2
LN_EPS = 1e-6
L2_EPS = 1e-6
DEPTH = 1
DN_ALPHA = (2.0 * DEPTH) ** 0.25

LANES = 128
VMEM_LIMIT = 56 * 1024 * 1024

HI = lax.Precision.HIGHEST


def _cparams(*sem):
    return pltpu.CompilerParams(dimension_semantics=sem, vmem_limit_bytes=VMEM_LIMIT)


def _bdot(a, b):
    return jnp.dot(a.astype(BF16), b.astype(BF16), preferred_element_type=F32)


def _bdot_nt(a, b):
    return lax.dot_general(a.astype(BF16), b.astype(BF16), (((1,), (1,)), ((), ())),
                           preferred_element_type=F32)


def _ln(x):
    mu = jnp.mean(x, axis=-1, keepdims=True)
    xc = x - mu
    var = jnp.mean(xc * xc, axis=-1, keepdims=True)
    return xc * lax.rsqrt(var + LN_EPS)


def _sigmoid(x):
    return 0.5 * jnp.tanh(0.5 * x) + 0.5


def _silu(x):
    return x * _sigmoid(x)


def _softplus(x):
    return jnp.maximum(x, 0.0) + jnp.log(1.0 + jnp.exp(-jnp.abs(x)))


def _pack_halves(x):
    n = x.shape[1] // 2
    hi = pltpu.bitcast(x[:, :n], jnp.uint32)
    lo = pltpu.bitcast(x[:, n:], jnp.uint32)
    return pltpu.bitcast(hi | (lo >> 16), F32)


def _unpack_halves(w):
    u = pltpu.bitcast(w, jnp.uint32)
    hi = pltpu.bitcast(u & jnp.uint32(0xFFFF0000), F32)
    lo = pltpu.bitcast(u << 16, F32)
    return jnp.concatenate([hi, lo], axis=1)


def _gelu_tanh(x):
    return 0.5 * x * (1.0 + jnp.tanh(math.sqrt(2.0 / math.pi) * (x + 0.044715 * (x * x * x))))


def _mod_body(c_ref, w_ref, b_ref, o_ref):
    o_ref[...] = jnp.dot(_silu(c_ref[...]), w_ref[...], preferred_element_type=F32,
                         precision=HI) + b_ref[...]


def _mod_vectors(cc, w_mod, b_mod):
    rows, d = cc.shape
    n = w_mod.shape[1]
    tn = 1536
    return pl.pallas_call(
        _mod_body,
        grid=(n // tn,),
        in_specs=[pl.BlockSpec((rows, d), lambda j: (0, 0)),
                  pl.BlockSpec((d, tn), lambda j: (0, j)),
                  pl.BlockSpec((1, tn), lambda j: (0, j))],
        out_specs=pl.BlockSpec((rows, tn), lambda j: (0, j)),
        out_shape=jax.ShapeDtypeStruct((rows, n), F32),
        compiler_params=_cparams("arbitrary"),
        name="mod_vectors",
    )(cc, w_mod, b_mod.reshape(1, n))


def _inproj_r_body(x_ref, sc_ref, sh_ref, w_ref, b_ref, *o_refs, splits):
    nb, tt, d = x_ref.shape
    tm = nb * tt
    xn = _ln(jnp.swapaxes(x_ref[...], 0, 1).reshape(tm, d)).reshape(tt, nb, d)
    xm = (xn * (1.0 + sc_ref[...])[None] + sh_ref[...][None]).reshape(tm, d).astype(BF16)
    for o_ref, (lo, hi) in zip(o_refs, splits):
        for n0 in range(lo, hi, 512):
            o_ref[:, n0 - lo:n0 - lo + 512] = (
                jnp.dot(xm, w_ref[:, n0:n0 + 512], preferred_element_type=F32) + b_ref[:, n0:n0 + 512])


def _inproj_raster(x, sc, sh, w, b, splits, tt):
    nb, length, d = x.shape
    n = w.shape[1]
    tm = tt * nb
    body = functools.partial(_inproj_r_body, splits=splits)
    return pl.pallas_call(
        body,
        grid=(length // tt,),
        in_specs=[pl.BlockSpec((nb, tt, d), lambda i: (0, i, 0)),
                  pl.BlockSpec((nb, d), lambda i: (0, 0)),
                  pl.BlockSpec((nb, d), lambda i: (0, 0)),
                  pl.BlockSpec((d, n), lambda i: (0, 0)),
                  pl.BlockSpec((1, n), lambda i: (0, 0))],
        out_specs=[pl.BlockSpec((tm, hi - lo), lambda i: (i, 0)) for lo, hi in splits],
        out_shape=[jax.ShapeDtypeStruct((length * nb, hi - lo), F32) for lo, hi in splits],
        compiler_params=_cparams("parallel"),
        name="inproj_raster",
    )(x, sc, sh, w, b)


def _inproj_g_body(x_ref, sc_ref, sh_ref, w_ref, b_ref, *o_refs, splits):
    gb, ch, d = x_ref.shape
    parts = []
    for i in range(gb):
        xn = _ln(x_ref[i])
        parts.append((xn * (1.0 + sc_ref[i:i + 1, :]) + sh_ref[i:i + 1, :]).astype(BF16))
    xm = jnp.concatenate(parts, axis=0)
    for o_ref, (lo, hi) in zip(o_refs, splits):
        step = 512 if (hi - lo) % 512 == 0 else hi - lo
        for n0 in range(lo, hi, step):
            res = jnp.dot(xm, w_ref[:, n0:n0 + step], preferred_element_type=F32) + b_ref[:, n0:n0 + step]
            for i in range(gb):
                o_ref[0, i, :, n0 - lo:n0 - lo + step] = res[i * ch:(i + 1) * ch]


def _inproj_column(xv, x_index_map, n_chunks, sc, sh, w, b, splits):
    nb, d = sc.shape
    gb = 8
    n = w.shape[1]
    body = functools.partial(_inproj_g_body, splits=splits)
    return pl.pallas_call(
        body,
        grid=(n_chunks, nb // gb),
        in_specs=[pl.BlockSpec((gb, GDN_CHUNK, d), x_index_map),
                  pl.BlockSpec((gb, d), lambda c, g: (g, 0)),
                  pl.BlockSpec((gb, d), lambda c, g: (g, 0)),
                  pl.BlockSpec((d, n), lambda c, g: (0, 0)),
                  pl.BlockSpec((1, n), lambda c, g: (0, 0))],
        out_specs=[pl.BlockSpec((1, gb, GDN_CHUNK, hi - lo), lambda c, g: (c, g, 0, 0)) for lo, hi in splits],
        out_shape=[jax.ShapeDtypeStruct((n_chunks, nb, GDN_CHUNK, hi - lo), F32) for lo, hi in splits],
        compiler_params=_cparams("parallel", "parallel"),
        name="inproj_column",
    )(xv, sc, sh, w, b)


def _inproj_gl_body(x_ref, sc_ref, sh_ref, w_ref, b_ref, qkv_ref, lg_ref, z_ref):
    _, tm, d = x_ref.shape
    rr = tm // GRID_W
    xm = _ln(x_ref[0]) * (1.0 + sc_ref[0]) + sh_ref[0]
    xb = xm.astype(BF16)
    for n0 in range(0, QKV_COLS, 512):
        qkv_ref[0, :, n0:n0 + 512] = (jnp.dot(xb, w_ref[:, n0:n0 + 512], preferred_element_type=F32)
                                      + b_ref[:, n0:n0 + 512])
    xs = jnp.swapaxes(xm.reshape(rr, GRID_W, d), 0, 1).reshape(tm, d).astype(BF16)
    n_qg = QKV_COLS + LANES
    lg = jnp.dot(xs, w_ref[:, QKV_COLS:n_qg], preferred_element_type=F32) + b_ref[:, QKV_COLS:n_qg]
    lg_ref[:, 0] = lg.reshape(GRID_W, rr, LANES)
    for n0 in range(0, GDN_VW, 512):
        res = (jnp.dot(xs, w_ref[:, n_qg + n0:n_qg + n0 + 512], preferred_element_type=F32)
               + b_ref[:, n_qg + n0:n_qg + n0 + 512])
        z_ref[:, 0, :, n0:n0 + 512] = res.reshape(GRID_W, rr, 512)


def _inproj_column_lat(x, sc, sh, w, b):
    nb, length, d = x.shape
    n = w.shape[1]
    rows = length // GRID_W
    rr = 8
    tm = rr * GRID_W
    vec = pl.BlockSpec((1, 1, d), lambda bi, ri: (bi, 0, 0))
    chunked = lambda m: pl.BlockSpec((GRID_W, 1, rr, m), lambda bi, ri: (0, bi, ri, 0))
    return pl.pallas_call(
        _inproj_gl_body,
        grid=(nb, rows // rr),
        in_specs=[pl.BlockSpec((1, tm, d), lambda bi, ri: (bi, ri, 0)),
                  vec, vec,
                  pl.BlockSpec((d, n), lambda bi, ri: (0, 0)),
                  pl.BlockSpec((1, n), lambda bi, ri: (0, 0))],
        out_specs=[pl.BlockSpec((1, tm, QKV_COLS), lambda bi, ri: (bi, ri, 0)), chunked(LANES), chunked(GDN_VW)],
        out_shape=[jax.ShapeDtypeStruct((nb, length, QKV_COLS), F32),
                   jax.ShapeDtypeStruct((GRID_W, nb, rows, LANES), F32),
                   jax.ShapeDtypeStruct((GRID_W, nb, rows, GDN_VW), F32)],
        compiler_params=_cparams("parallel", "parallel"),
        name="inproj_column_lat",
    )(x, sc.reshape(nb, 1, d), sh.reshape(nb, 1, d), w, b)


def _lru_body(*refs, tt, nb, reverse, emit):
    if emit == "ya":
        (xa_ref, prev_ref, next_ref, cw_ref, cb_ref, wa_ref, ba_ref, wx_ref, bx_ref, lam_ref, h0_ref,
         hf_ref, ga_ref, out_ref, hl_ref, a_s, b_s, h_s) = refs
    elif emit == "h":
        (xa_ref, prev_ref, next_ref, cw_ref, cb_ref, wa_ref, ba_ref, wx_ref, bx_ref, lam_ref, h0_ref,
         out_ref, hl_ref, a_s, b_s, h_s) = refs
    else:
        (xa_ref, prev_ref, next_ref, cw_ref, cb_ref, wa_ref, ba_ref, wx_ref, bx_ref, lam_ref, h0_ref,
         hl_ref, a_s, b_s, h_s) = refs
    step = pl.program_id(0)
    nsteps = pl.num_programs(0)
    blk = (nsteps - 1 - step) if reverse else step
    rows = tt * nb

    @pl.when(step == 0)
    def _():
        h_s[...] = h0_ref[...]

    prev = jnp.where(blk == 0, 0.0, prev_ref[...])
    nxt = jnp.where(blk == nsteps - 1, 0.0, next_ref[...])
    ext = jnp.concatenate([prev, xa_ref[...], nxt], axis=0)
    u = (cw_ref[0:1, :] * ext[0:rows] + cw_ref[1:2, :] * ext[nb:rows + nb]
         + cw_ref[2:3, :] * ext[2 * nb:rows + 2 * nb] + cw_ref[3:4, :] * ext[3 * nb:rows + 3 * nb]
         + cb_ref[...])
    c_all = (-0.5 * LRU_C) * _softplus(-lam_ref[...])
    for n in range(LRU_BLOCKS):
        sl = slice(n * LRU_BLOCK, (n + 1) * LRU_BLOCK)
        un = u[:, sl]
        ub = un.astype(BF16)
        t_r = jnp.tanh(jnp.dot(ub, wa_ref[n], preferred_element_type=F32) + ba_ref[:, sl])
        t_i = jnp.tanh(jnp.dot(ub, wx_ref[n], preferred_element_type=F32) + bx_ref[:, sl])
        c = c_all[:, sl]
        a = jnp.exp(c * t_r + c)
        hun = 0.5 * un
        a_s[:, sl] = a
        b_s[:, sl] = jnp.sqrt(1.0 - a * a) * (t_i * hun + hun)

    def scan_step(j, h):
        t = (tt - 1 - j) if reverse else j
        r0 = pl.multiple_of(t * nb, nb)
        h = a_s[pl.ds(r0, nb), :] * h + b_s[pl.ds(r0, nb), :]
        if emit == "h":
            out_ref[pl.ds(r0, nb), :] = h
        elif emit == "ya":
            b_s[pl.ds(r0, nb), :] = h
        return h

    h = lax.fori_loop(0, tt, scan_step, h_s[...], unroll=4)
    h_s[...] = h
    hl_ref[...] = h
    if emit == "ya":
        out_ref[...] = (_gelu_tanh(ga_ref[...]) * (hf_ref[...] + b_s[...])).astype(BF16)


def _lru_direction(xa, lp, di, h0, reverse, emit, tt, hf=None, ga=None):
    t, w = xa.shape
    nb = h0.shape[0]
    rows = tt * nb
    nblk = t // rows
    assert rows % (2 * nb) == 0
    prev_per = rows // nb
    next_per = rows // (2 * nb)
    n_prev = t // nb
    n_next = t // (2 * nb)

    def bi(i):
        return (nblk - 1 - i) if reverse else i

    const2 = lambda i: (0, 0)
    in_specs = [
        pl.BlockSpec((rows, w), lambda i: (bi(i), 0)),
        pl.BlockSpec((nb, w), lambda i: (jnp.maximum(bi(i) * prev_per - 1, 0), 0)),
        pl.BlockSpec((2 * nb, w), lambda i: (jnp.minimum((bi(i) + 1) * next_per, n_next - 1), 0)),
        pl.BlockSpec((CONV_W, w), const2),
        pl.BlockSpec((1, w), const2),
        pl.BlockSpec((LRU_BLOCKS, LRU_BLOCK, LRU_BLOCK), lambda i: (0, 0, 0)),
        pl.BlockSpec((1, w), const2),
        pl.BlockSpec((LRU_BLOCKS, LRU_BLOCK, LRU_BLOCK), lambda i: (0, 0, 0)),
        pl.BlockSpec((1, w), const2),
        pl.BlockSpec((1, w), const2),
        pl.BlockSpec((nb, w), const2),
    ]
    args = [xa, xa, xa, lp["conv_a_w"], lp["conv_a_b"].reshape(1, w),
            (0.5 * lp["lru_wa"][di]).astype(BF16), (0.5 * lp["lru_ba"][di]).reshape(1, w),
            (0.5 * lp["lru_wx"][di]).astype(BF16), (0.5 * lp["lru_bx"][di]).reshape(1, w),
            lp["lru_lambda"][di].reshape(1, w), h0]
    out_specs = []
    out_shape = []
    if emit == "ya":
        in_specs += [pl.BlockSpec((rows, w), lambda i: (bi(i), 0)),
                     pl.BlockSpec((rows, w), lambda i: (bi(i), 0))]
        args += [hf, ga]
        out_specs.append(pl.BlockSpec((rows, w), lambda i: (bi(i), 0)))
        out_shape.append(jax.ShapeDtypeStruct((t, w), BF16))
    elif emit == "h":
        out_specs.append(pl.BlockSpec((rows, w), lambda i: (bi(i), 0)))
        out_shape.append(jax.ShapeDtypeStruct((t, w), F32))
    out_specs.append(pl.BlockSpec((nb, w), const2))
    out_shape.append(jax.ShapeDtypeStruct((nb, w), F32))
    body = functools.partial(_lru_body, tt=tt, nb=nb, reverse=reverse, emit=emit)
    return pl.pallas_call(
        body,
        grid=(nblk,),
        in_specs=in_specs,
        out_specs=out_specs,
        out_shape=out_shape,
        scratch_shapes=[pltpu.VMEM((rows, w), F32), pltpu.VMEM((rows, w), F32), pltpu.VMEM((nb, w), F32)],
        compiler_params=_cparams("arbitrary"),
        name="lru_" + ("bwd" if reverse else "fwd") + "_" + emit,
    )(*args)


def _gdn_prep_body(qkv_ref, prev_ref, next_ref, lg_ref, cw_ref, ga_ref, gd_ref, qkv_o, g_o):
    c = pl.program_id(0)
    nc = pl.num_programs(0)
    ch = GDN_CHUNK
    gb = qkv_ref.shape[1]
    for i in range(gb):
        prev = jnp.where(c == 0, 0.0, prev_ref[0, i])
        nxt = jnp.where(c == nc - 1, 0.0, next_ref[0, i])
        ext = jnp.concatenate([prev, qkv_ref[0, i], nxt], axis=0)
        u = (cw_ref[0:1, :] * ext[7:7 + ch] + cw_ref[1:2, :] * ext[8:8 + ch]
             + cw_ref[2:3, :] * ext[9:9 + ch] + cw_ref[3:4, :] * ext[10:10 + ch])
        act = _silu(u)
        for h in range(GDN_HEADS):
            sq = slice(h * GDN_DK, (h + 1) * GDN_DK)
            qh = act[:, sq]
            qn = qh * lax.rsqrt(jnp.sum(qh * qh, axis=-1, keepdims=True) + L2_EPS) * (GDN_DK ** -0.5)
            qkv_o[0, i, :, sq] = qn.astype(qkv_o.dtype)
            sk = slice(GDN_QK + h * GDN_DK, GDN_QK + (h + 1) * GDN_DK)
            kh = act[:, sk]
            kn = kh * lax.rsqrt(jnp.sum(kh * kh, axis=-1, keepdims=True) + L2_EPS)
            qkv_o[0, i, :, sk] = kn.astype(qkv_o.dtype)
        qkv_o[0, i, :, 2 * GDN_QK:] = act[:, 2 * GDN_QK:].astype(qkv_o.dtype)
        lg = lg_ref[0, i]
        lane = lax.broadcasted_iota(jnp.int32, lg.shape, 1)
        decay = ga_ref[...] * _softplus(lg + gd_ref[...])
        g_o[0, i] = jnp.where(lane < 2 * GDN_HEADS, decay, jnp.where(lane < N_GATES, _sigmoid(lg), 0.0))


def _gdn_prep(qkv, lg, conv_w, a_row, dt_row, gb):
    nc, nb, ch, n = qkv.shape
    hb = ch // 8
    return pl.pallas_call(
        _gdn_prep_body,
        grid=(nc, nb // gb),
        in_specs=[pl.BlockSpec((1, gb, ch, n), lambda c, b: (c, b, 0, 0)),
                  pl.BlockSpec((1, gb, 8, n), lambda c, b: (jnp.maximum(c - 1, 0), b, hb - 1, 0)),
                  pl.BlockSpec((1, gb, 8, n), lambda c, b: (jnp.minimum(c + 1, nc - 1), b, 0, 0)),
                  pl.BlockSpec((1, gb, ch, LANES), lambda c, b: (c, b, 0, 0)),
                  pl.BlockSpec((CONV_W, n), lambda c, b: (0, 0)),
                  pl.BlockSpec((1, LANES), lambda c, b: (0, 0)),
                  pl.BlockSpec((1, LANES), lambda c, b: (0, 0))],
        out_specs=[pl.BlockSpec((1, gb, ch, n), lambda c, b: (c, b, 0, 0)),
                   pl.BlockSpec((1, gb, ch, LANES), lambda c, b: (c, b, 0, 0))],
        out_shape=[jax.ShapeDtypeStruct(qkv.shape, BF16), jax.ShapeDtypeStruct(lg.shape, F32)],
        compiler_params=_cparams("parallel", "parallel"),
        name="gdn_prep",
    )(qkv, qkv, qkv, lg, conv_w, a_row, dt_row)


def _gdn_prep_lat_body(x_ref, prev_ref, next_ref, cw_ref, o_ref):
    ri = pl.program_id(1)
    nr = pl.num_programs(1)
    part = pl.program_id(2)
    gw = GRID_W
    main = x_ref[0]
    rows, n = main.shape
    prev = prev_ref[0]
    nxt = next_ref[0]
    zrow = jnp.zeros((1, n), F32)
    prev_wrap = jnp.concatenate([zrow, prev[:gw - 1]], axis=0)
    nxt_wrap = jnp.concatenate([nxt[1:gw], zrow, nxt[gw + 1:], zrow], axis=0)
    prev = jnp.where(ri == 0, prev_wrap, prev)
    nxt = jnp.where(ri == nr - 1, nxt_wrap, nxt)
    ext = jnp.concatenate([prev, main, nxt], axis=0)
    cwh = 0.5 * cw_ref[...]
    hu = (cwh[0:1, :] * ext[0:rows] + cwh[1:2, :] * ext[gw:gw + rows]
          + cwh[2:3, :] * ext[2 * gw:2 * gw + rows] + cwh[3:4, :] * ext[3 * gw:3 * gw + rows])
    act = hu * jnp.tanh(hu) + hu
    scale = jnp.where(part == 0, GDN_DK ** -0.5, 1.0)
    parts = []
    for h in range(n // GDN_DK):
        ah = act[:, h * GDN_DK:(h + 1) * GDN_DK]
        nh = ah * (lax.rsqrt(jnp.sum(ah * ah, axis=-1, keepdims=True) + L2_EPS) * scale)
        parts.append(jnp.where(part < 2, nh, ah))
    y = jnp.concatenate(parts, axis=1)
    o_ref[:, 0] = jnp.swapaxes(y.reshape(rows // gw, gw, n), 0, 1).astype(BF16)


def _gdn_prep_lat(qkv_raw, conv_w):
    nb, length, n3 = qkv_raw.shape
    gw = GRID_W
    rows = length // gw
    rr = 16
    n = GDN_QK
    assert n3 == 3 * n and rows % rr == 0
    return pl.pallas_call(
        _gdn_prep_lat_body,
        grid=(nb, rows // rr, 3),
        in_specs=[pl.BlockSpec((1, rr * gw, n), lambda b, ri, j: (b, ri, j)),
                  pl.BlockSpec((1, gw, n), lambda b, ri, j: (b, (ri * rr + rows - 1) % rows, j)),
                  pl.BlockSpec((1, 2 * gw, n), lambda b, ri, j: (b, (((ri + 1) * rr) % rows) // 2, j)),
                  pl.BlockSpec((CONV_W, n), lambda b, ri, j: (0, j))],
        out_specs=pl.BlockSpec((gw, 1, rr, n), lambda b, ri, j: (0, b, ri, j)),
        out_shape=jax.ShapeDtypeStruct((gw, nb, rows, n3), BF16),
        compiler_params=_cparams("parallel", "parallel", "parallel"),
        name="gdn_prep_lat",
    )(qkv_raw, qkv_raw, qkv_raw, conv_w)


def _gdn_gates_body(lg_ref, ga_ref, gd_ref, g_o):
    lg = lg_ref[...]
    lane = lax.broadcasted_iota(jnp.int32, lg.shape, lg.ndim - 1)
    decay = ga_ref[...] * _softplus(lg + gd_ref[...])
    g_o[...] = jnp.where(lane < 2 * GDN_HEADS, decay, jnp.where(lane < N_GATES, _sigmoid(lg), 0.0))


def _gdn_gates(lg, a_row, dt_row):
    nc, nb, ch, n = lg.shape
    cb = 8
    spec = pl.BlockSpec((cb, nb, ch, n), lambda c: (c, 0, 0, 0))
    vec = pl.BlockSpec((1, n), lambda c: (0, 0))
    return pl.pallas_call(
        _gdn_gates_body,
        grid=(nc // cb,),
        in_specs=[spec, vec, vec],
        out_specs=spec,
        out_shape=jax.ShapeDtypeStruct(lg.shape, F32),
        compiler_params=_cparams("parallel"),
        name="gdn_gates",
    )(lg, a_row, dt_row)


def _gdn_scan_body(qkv_ref, g_ref, s0_ref, *rest, reverse, emit_o):
    if emit_o:
        o_ref, sl_ref, s_s = rest
    else:
        sl_ref, s_s = rest
    step = pl.program_id(1)
    ch = GDN_CHUNK
    gb = qkv_ref.shape[1]
    n_sq = int(math.log2(ch)) - 1

    @pl.when(step == 0)
    def _():
        s_s[...] = s0_ref[...]

    row = lax.broadcasted_iota(jnp.int32, (ch, ch), 0)
    col = lax.broadcasted_iota(jnp.int32, (ch, ch), 1)
    incl = (row <= col) if reverse else (row >= col)
    strict = (row < col) if reverse else (row > col)
    eye = (row == col).astype(F32)
    tri = incl.astype(F32)
    goff = GDN_HEADS if reverse else 0
    boff = 2 * GDN_HEADS + goff

    units = [(i, h) for i in range(gb) for h in range(GDN_HEADS)]
    nu = len(units)
    gcum, gcum_t, gtot, gates = [], [], [], []
    for i in range(gb):
        g = g_ref[0, i]
        gates.append(g)
        cum = jnp.dot(tri, g, preferred_element_type=F32, precision=HI)
        gcum.append(cum)
        gcum_t.append(cum.T)
        gtot.append(jnp.sum(g, axis=0, keepdims=True))

    q, k, kb, xin, decay, eg, egl, gl = [], [], [], [], [], [], [], []
    for i, h in units:
        qh = qkv_ref[0, i, :, h * GDN_DK:(h + 1) * GDN_DK].astype(F32)
        kh = qkv_ref[0, i, :, GDN_QK + h * GDN_DK:GDN_QK + (h + 1) * GDN_DK].astype(F32)
        vh = qkv_ref[0, i, :, 2 * GDN_QK + h * GDN_DV:2 * GDN_QK + (h + 1) * GDN_DV].astype(F32)
        gc = jnp.broadcast_to(gcum[i][:, goff + h:goff + h + 1], (ch, GDN_DK))
        beta = jnp.broadcast_to(gates[i][:, boff + h:boff + h + 1], (ch, GDN_DK))
        gr = gcum_t[i][goff + h:goff + h + 1, :]
        gt = gtot[i][:, goff + h:goff + h + 1]
        decay.append(jnp.where(incl, jnp.exp(jnp.where(incl, gc[:, :ch] - gr, 0.0)), 0.0))
        e = jnp.exp(gc)
        kbh = kh * beta
        q.append(qh)
        k.append(kh)
        kb.append(kbh)
        eg.append(e)
        egl.append(jnp.exp(gt - gc))
        gl.append(jnp.exp(gt))
        xin.append(jnp.concatenate([kbh * e, vh * beta], axis=1).astype(BF16))

    gram = [_bdot_nt(jnp.concatenate([kb[j], q[j]], axis=0), k[j]) for j in range(nu)]
    nil = [-jnp.where(strict, gram[j][:ch] * decay[j], 0.0) for j in range(nu)]
    amat = [gram[j][ch:] * decay[j] for j in range(nu)]
    inv = [eye + m for m in nil]
    power = nil
    for _ in range(n_sq):
        pb = [p.astype(BF16) for p in power]
        power = [jnp.dot(p, p, preferred_element_type=F32) for p in pb]
        inv = [iv + _bdot(iv, p) for iv, p in zip(inv, power)]
    wu = [_bdot(iv, x) for iv, x in zip(inv, xin)]
    s_old = [s_s[i, h] for i, h in units]
    s_bf = [s.astype(BF16) for s in s_old]
    ws = [_bdot(jnp.concatenate([wu[j][:, :GDN_DK], q[j] * eg[j]], axis=0), s_bf[j]) for j in range(nu)]
    v_new = [(wu[j][:, GDN_DK:] - ws[j][:ch]).astype(BF16) for j in range(nu)]
    upd = [_bdot(jnp.concatenate([amat[j], (k[j] * egl[j]).T], axis=0), v_new[j]) for j in range(nu)]
    for j, (i, h) in enumerate(units):
        if emit_o:
            o_ref[0, i, :, h * GDN_DV:(h + 1) * GDN_DV] = (ws[j][ch:] + upd[j][:ch]).astype(o_ref.dtype)
        s_new = s_old[j] * gl[j] + upd[j][ch:]
        s_s[i, h] = s_new
        sl_ref[i, h] = s_new


def _gdn_direction(qkv, gates, s0, reverse, emit_o, gb=1):
    nc, nb, ch, n = qkv.shape

    def ci(c):
        return (nc - 1 - c) if reverse else c

    out_specs = []
    out_shape = []
    if emit_o:
        out_specs.append(pl.BlockSpec((1, gb, ch, GDN_VW), lambda b, c: (ci(c), b, 0, 0)))
        out_shape.append(jax.ShapeDtypeStruct((nc, nb, ch, GDN_VW), BF16))
    out_specs.append(pl.BlockSpec((gb, GDN_HEADS, GDN_DK, GDN_DV), lambda b, c: (b, 0, 0, 0)))
    out_shape.append(jax.ShapeDtypeStruct((nb, GDN_HEADS, GDN_DK, GDN_DV), F32))
    body = functools.partial(_gdn_scan_body, reverse=reverse, emit_o=emit_o)
    return pl.pallas_call(
        body,
        grid=(nb // gb, nc),
        in_specs=[pl.BlockSpec((1, gb, ch, n), lambda b, c: (ci(c), b, 0, 0)),
                  pl.BlockSpec((1, gb, ch, LANES), lambda b, c: (ci(c), b, 0, 0)),
                  pl.BlockSpec((gb, GDN_HEADS, GDN_DK, GDN_DV), lambda b, c: (b, 0, 0, 0))],
        out_specs=out_specs,
        out_shape=out_shape,
        scratch_shapes=[pltpu.VMEM((gb, GDN_HEADS, GDN_DK, GDN_DV), F32)],
        compiler_params=_cparams("parallel", "arbitrary"),
        name="gdn_scan_" + ("bwd" if reverse else "fwd"),
    )(qkv, gates, s0)


def _gdn_out_body(of_ref, ob_ref, z_ref, nw_ref, y_ref, y_s):
    nb = of_ref.shape[1]
    for i in range(nb):
        o = of_ref[0, i].astype(F32) + ob_ref[0, i].astype(F32)
        zg = _silu(z_ref[0, i])
        for h in range(GDN_HEADS):
            sl = slice(h * GDN_DV, (h + 1) * GDN_DV)
            oh = o[:, sl]
            nh = oh * lax.rsqrt(jnp.mean(oh * oh, axis=-1, keepdims=True) + LN_EPS) * nw_ref[...]
            y_s[i, :, sl] = nh * zg[:, sl]
    y_ref[:, 0] = jnp.swapaxes(y_s[...], 0, 1).astype(BF16)


def _gdn_out(o_f, o_b, z, norm_w):
    nc, nb, ch, vw = o_f.shape
    spec = pl.BlockSpec((1, nb, ch, vw), lambda c: (c, 0, 0, 0))
    y = pl.pallas_call(
        _gdn_out_body,
        grid=(nc,),
        in_specs=[spec, spec, spec, pl.BlockSpec((1, GDN_DV), lambda c: (0, 0))],
        out_specs=pl.BlockSpec((ch, 1, nb, vw), lambda c: (0, c, 0, 0)),
        out_shape=jax.ShapeDtypeStruct((ch, nc, nb, vw), BF16),
        scratch_shapes=[pltpu.VMEM((nb, ch, vw), F32)],
        compiler_params=_cparams("parallel"),
        name="gdn_out",
    )(o_f, o_b, z, norm_w.reshape(1, GDN_DV))
    return y.reshape(ch * nc * nb, vw)


def _mixer_body(ya_ref, yb_ref, mg_ref, x_ref, g1_ref, sh2_ref, sc2_ref, wpa_ref, wpb_ref, wo_ref,
                lg_ref, lb_ref, wrh_ref, wrl_ref, br_ref, x1_ref, h2_ref, rt_ref, *, n_sub):
    nb, tt, d = x_ref.shape
    ts = tt // n_sub
    rs = ts * nb
    subs = [slice(k * rs, (k + 1) * rs) for k in range(n_sub)]
    xs = [jnp.swapaxes(x_ref[:, k * ts:(k + 1) * ts, :], 0, 1).reshape(rs, d) for k in range(n_sub)]
    pa = [jnp.dot(ya_ref[s, :], wpa_ref[...], preferred_element_type=F32) for s in subs]
    pb = [jnp.dot(yb_ref[s, :], wpb_ref[...], preferred_element_type=F32) for s in subs]
    merged = [(_sigmoid(mg_ref[s, :d]) * a + _sigmoid(mg_ref[s, d:]) * b).astype(BF16)
              for s, a, b in zip(subs, pa, pb)]
    mix = [jnp.dot(m, wo_ref[...], preferred_element_type=F32) for m in merged]
    x1 = []
    for s, xk, mk in zip(subs, xs, mix):
        gm = (mk.reshape(ts, nb, d) * g1_ref[...][None]).reshape(rs, d)
        x1k = _ln(DN_ALPHA * xk + gm) * lg_ref[...] + lb_ref[...]
        x1_ref[s, :] = x1k
        x1.append(x1k)
    h2 = [(_ln(v).reshape(ts, nb, d) * (1.0 + sc2_ref[...])[None] + sh2_ref[...][None]).reshape(rs, d) for v in x1]
    h2_hi = [v.astype(BF16) for v in h2]
    for s, v in zip(subs, h2_hi):
        h2_ref[s, :] = _pack_halves(v.astype(F32))
    h2_lo = [(v - hi.astype(F32)).astype(BF16) for v, hi in zip(h2, h2_hi)]
    all_logits = [(jnp.dot(hi, wrh_ref[...], preferred_element_type=F32)
                   + jnp.dot(lo, wrh_ref[...], preferred_element_type=F32)
                   + jnp.dot(hi, wrl_ref[...], preferred_element_type=F32) + br_ref[...])
                  for hi, lo in zip(h2_hi, h2_lo)]
    for s, logits in zip(subs, all_logits):
        rt_ref[s, :] = _route(logits)


def _route(logits):
    lane = lax.broadcasted_iota(jnp.int32, logits.shape, 1).astype(F32)
    neg = jnp.float32(-jnp.inf)
    big = jnp.float32(1 << 20)
    is_g = lane < N_GROUPS
    gl = jnp.where(is_g, logits, neg)
    gmax = jnp.max(gl, axis=-1, keepdims=True)
    gsum = jnp.sum(jnp.where(is_g, jnp.exp(gl - gmax), 0.0), axis=-1, keepdims=True)
    p_group = 1.0 / gsum
    g_idx = jnp.min(jnp.where(gl == gmax, lane, big), axis=-1, keepdims=True)
    lo = N_GROUPS + g_idx * EXP_PER_GROUP
    in_g = (lane >= lo) & (lane < lo + EXP_PER_GROUP)
    el = jnp.where(in_g, logits, neg)
    m1 = jnp.max(el, axis=-1, keepdims=True)
    i1 = jnp.min(jnp.where(el == m1, lane, big), axis=-1, keepdims=True)
    el2 = jnp.where(lane == i1, neg, el)
    m2 = jnp.max(el2, axis=-1, keepdims=True)
    i2 = jnp.min(jnp.where(el2 == m2, lane, big), axis=-1, keepdims=True)
    esum = jnp.sum(jnp.where(in_g, jnp.exp(el - m1), 0.0), axis=-1, keepdims=True)
    p1 = 1.0 / esum
    p2 = jnp.exp(m2 - m1) / esum
    w1 = p_group * p1 / (p1 + p2)
    w2 = p_group * p2 / (p1 + p2)
    e1 = i1 - N_GROUPS
    e2 = i2 - N_GROUPS
    return jnp.where(lane == 0, e1, jnp.where(lane == 1, e2, jnp.where(lane == 2, w1,
                     jnp.where(lane == 3, w2, 0.0))))


def _mixer(ya, yb, mg, x, g1, sh2, sc2, w_pa, w_pb, w_out, ln_g, ln_b, w_r_hi, w_r_lo, b_r, tt):
    nb, length, d = x.shape
    t = nb * length
    tm = tt * nb
    row = lambda n: pl.BlockSpec((tm, n), lambda i: (i, 0))
    full = lambda a: pl.BlockSpec(a.shape, lambda i: (0,) * a.ndim)
    args = [ya, yb, mg, x, g1, sh2, sc2, w_pa, w_pb, w_out, ln_g, ln_b, w_r_hi, w_r_lo, b_r]
    in_specs = ([row(d), row(d), row(2 * d), pl.BlockSpec((nb, tt, d), lambda i: (0, i, 0))]
                + [full(a) for a in args[4:]])
    body = functools.partial(_mixer_body, n_sub=MIXER_SUB)
    return pl.pallas_call(
        body,
        grid=(t // tm,),
        in_specs=in_specs,
        out_specs=[row(d), row(d // 2), row(LANES)],
        out_shape=[jax.ShapeDtypeStruct((t, d), F32), jax.ShapeDtypeStruct((t, d // 2), F32),
                   jax.ShapeDtypeStruct((t, LANES), F32)],
        compiler_params=_cparams("parallel"),
        name="mixer_router",
    )(*args)


def _experts_body(be_ref, nu_ref, x_ref, wg_ref, wu_ref, wd_ref, y_ref, wg_s, wu_s, wd_s):
    i = pl.program_id(0)
    used = i < nu_ref[0]
    new_expert = jnp.logical_or(i == 0, be_ref[i] != be_ref[jnp.maximum(i - 1, 0)])

    @pl.when(jnp.logical_and(used, new_expert))
    def _():
        wg_s[...] = wg_ref[0].astype(BF16)
        wu_s[...] = wu_ref[0].astype(BF16)
        wd_s[...] = wd_ref[0].astype(BF16)

    @pl.when(used)
    def _():
        x = _unpack_halves(x_ref[...]).astype(BF16)
        hg = jnp.dot(x, wg_s[...], preferred_element_type=F32)
        hu = jnp.dot(x, wu_s[...], preferred_element_type=F32)
        y_ref[...] = _pack_halves(_bdot(_silu(hg) * hu, wd_s[...]).astype(BF16).astype(F32))

    @pl.when(jnp.logical_not(used))
    def _():
        y_ref[...] = jnp.zeros_like(y_ref)


def _experts(x_pad, block_expert, n_used, w_gate, w_up, w_down, tm):
    rows, dh = x_pad.shape
    d = 2 * dh
    de = w_gate.shape[2]
    nblk = rows // tm
    wspec = lambda k, n: pl.BlockSpec((1, k, n), lambda i, be, nu: (be[i], 0, 0))
    return pl.pallas_call(
        _experts_body,
        grid_spec=pltpu.PrefetchScalarGridSpec(
            num_scalar_prefetch=2,
            grid=(nblk,),
            in_specs=[pl.BlockSpec((tm, dh), lambda i, be, nu: (i, 0)),
                      wspec(d, de), wspec(d, de), wspec(de, d)],
            out_specs=pl.BlockSpec((tm, dh), lambda i, be, nu: (i, 0)),
            scratch_shapes=[pltpu.VMEM((d, de), BF16), pltpu.VMEM((d, de), BF16), pltpu.VMEM((de, d), BF16)],
        ),
        out_shape=jax.ShapeDtypeStruct((rows, dh), F32),
        compiler_params=_cparams("arbitrary"),
        name="experts",
    )(block_expert, n_used, x_pad, w_gate, w_up, w_down)


SC_CORES = 2
SC_SUBCORES = 16
SC_CHUNK = 32


def _sc_gather_rows(table, idx):
    n_rows, d = table.shape
    n_out = idx.shape[0]
    workers = SC_CORES * SC_SUBCORES
    per_w = n_out // workers
    assert n_out % (8 * workers) == 0 and per_w % (2 * SC_CHUNK) == 0 and table.dtype == F32
    n_chunks = per_w // SC_CHUNK
    mesh = plsc.VectorSubcoreMesh(core_axis_name="c", subcore_axis_name="s")
    rows_buf = pltpu.VMEM((SC_CHUNK, d), F32)
    dma_sem = pltpu.SemaphoreType.DMA

    @functools.partial(
        pl.kernel, mesh=mesh,
        out_type=jax.ShapeDtypeStruct((n_out, d), F32),
        scratch_types=[pltpu.VMEM((per_w,), jnp.int32), rows_buf, rows_buf, dma_sem, dma_sem, dma_sem, dma_sem])
    def gather(table_hbm, idx_hbm, out_hbm, idx_v, rows_a, rows_b, gsem_a, gsem_b, wsem_a, wsem_b):
        wid = lax.axis_index("s") * SC_CORES + lax.axis_index("c")
        base = pl.multiple_of(wid * per_w, 8)
        pltpu.sync_copy(idx_hbm.at[pl.ds(base, per_w)], idx_v)

        @pl.loop(0, n_chunks, step=2)
        def _(j):
            off_a = pl.multiple_of(j * SC_CHUNK, 8)
            off_b = pl.multiple_of(off_a + SC_CHUNK, 8)
            g_a = pltpu.async_copy(table_hbm.at[idx_v.at[pl.ds(off_a, SC_CHUNK)]], rows_a, gsem_a)
            g_b = pltpu.async_copy(table_hbm.at[idx_v.at[pl.ds(off_b, SC_CHUNK)]], rows_b, gsem_b)
            g_a.wait()
            w_a = pltpu.async_copy(rows_a, out_hbm.at[pl.ds(base + off_a, SC_CHUNK)], wsem_a)
            g_b.wait()
            w_b = pltpu.async_copy(rows_b, out_hbm.at[pl.ds(base + off_b, SC_CHUNK)], wsem_b)
            w_a.wait()
            w_b.wait()

    return gather(table, idx)


def _final_body(x1_ref, y1_ref, y2_ref, rt_ref, g2_ref, lg_ref, lb_ref, o_ref):
    tm, d = x1_ref.shape
    nb = g2_ref.shape[0]
    y = rt_ref[:, 2:3] * _unpack_halves(y1_ref[...]) + rt_ref[:, 3:4] * _unpack_halves(y2_ref[...])
    gy = (y.reshape(tm // nb, nb, d) * g2_ref[...][None]).reshape(tm, d)
    out = _ln(DN_ALPHA * x1_ref[...] + gy) * lg_ref[...] + lb_ref[...]
    o_ref[...] = jnp.swapaxes(out.reshape(tm // nb, nb, d), 0, 1)


def _final(x1_tb, y1, y2, route, g2, ln_g, ln_b, tq):
    t, d = x1_tb.shape
    nb = g2.shape[0]
    length = t // nb
    tm = tq * nb
    row = lambda n: pl.BlockSpec((tm, n), lambda i: (i, 0))
    return pl.pallas_call(
        _final_body,
        grid=(length // tq,),
        in_specs=[row(d), row(d // 2), row(d // 2), row(LANES),
                  pl.BlockSpec((nb, d), lambda i: (0, 0)),
                  pl.BlockSpec((1, d), lambda i: (0, 0)),
                  pl.BlockSpec((1, d), lambda i: (0, 0))],
        out_specs=pl.BlockSpec((nb, tq, d), lambda i: (0, i, 0)),
        out_shape=jax.ShapeDtypeStruct((nb, length, d), F32),
        compiler_params=_cparams("parallel"),
        name="moe_combine_postnorm",
    )(x1_tb, y1, y2, route, g2, ln_g, ln_b)


MOE_TM = 512
GDN_GB = 4
MIXER_SUB = 2


def _route_plan(e1, e2, tm):
    t = e1.shape[0]
    e_flat = jnp.concatenate([e1, e2])
    n_assign = e_flat.shape[0]
    experts = jnp.arange(N_EXPERTS, dtype=jnp.int32)
    onehot = (e_flat[:, None] == experts[None, :]).astype(jnp.int32)
    csum = jnp.cumsum(onehot, axis=0)
    rank = jnp.sum(jnp.where(onehot > 0, csum - 1, 0), axis=1)
    counts = csum[-1]
    starts = jnp.cumsum(counts) - counts
    padded = (counts + tm - 1) // tm * tm
    pends = jnp.cumsum(padded)
    pstarts = pends - padded
    dest = pstarts[e_flat] + rank
    n_blocks = (n_assign + N_EXPERTS * (tm - 1) + tm - 1) // tm
    n_used = (pends[-1] // tm).astype(jnp.int32).reshape(1)
    blk0 = jnp.arange(n_blocks, dtype=jnp.int32) * tm
    block_expert = jnp.minimum(jnp.sum((pends[None, :] <= blk0[:, None]).astype(jnp.int32), axis=1),
                               N_EXPERTS - 1).astype(jnp.int32)
    order = jnp.argsort(e_flat, stable=True).astype(jnp.int32)
    p = jnp.arange(n_blocks * tm, dtype=jnp.int32)
    pe = jnp.repeat(block_expert, tm)
    r = p - pstarts[pe]
    valid = r < counts[pe]
    a_idx = jnp.clip(starts[pe] + r, 0, n_assign - 1)
    src = jnp.where(valid, order.at[a_idx].get(mode="promise_in_bounds") % t, p % t)
    return src, dest[:t], dest[t:], block_expert, n_used


def kernel(x, c, ctx, c_ctx, w_mod, b_mod, w_in, b_in, conv_a_w, conv_a_b, lru_wa, lru_ba, lru_wx, lru_bx,
           lru_lambda, conv_qkv_w, gdn_a_log, gdn_dt_bias, gdn_norm_w, w_pa, w_pb, w_out, ln1_g, ln1_b,
           w_router_g, b_router_g, w_router_e, b_router_e, w_e_gate, w_e_up, w_e_down, ln2_g, ln2_b):
    nb, n_lat, d = x.shape
    n_ctx = ctx.shape[1]
    rows = n_lat // GRID_W
    assert rows == GDN_CHUNK and n_ctx % GDN_CHUNK == 0 and w_mod.shape[0] == 1
    layer = 0
    lp = {"conv_a_w": conv_a_w[layer], "conv_a_b": conv_a_b[layer], "lru_wa": lru_wa[layer],
          "lru_ba": lru_ba[layer], "lru_wx": lru_wx[layer], "lru_bx": lru_bx[layer],
          "lru_lambda": lru_lambda[layer]}

    pad_rows = (-(nb + 1)) % 8
    cc = jnp.concatenate([c, c_ctx[None, :], jnp.zeros((pad_rows, d), F32)], axis=0)
    mod = _mod_vectors(cc, w_mod[layer], b_mod[layer])
    sh1, sc1, g1, sh2, sc2, g2 = [mod[:nb, j * d:(j + 1) * d] for j in range(6)]
    csh1 = jnp.broadcast_to(mod[nb:nb + 1, 0:d], (nb, d))
    csc1 = jnp.broadcast_to(mod[nb:nb + 1, d:2 * d], (nb, d))

    w_l = w_in[layer]
    b_l = b_in[layer]
    w_r = jnp.concatenate([w_l[:, OFF_XA:OFF_GDN], w_l[:, OFF_MG:]], axis=1).astype(BF16)
    b_r = jnp.concatenate([b_l[OFF_XA:OFF_GDN], b_l[OFF_MG:]])[None, :]
    gate_pad = LANES - N_GATES
    w_g = jnp.concatenate([w_l[:, OFF_GDN:OFF_GDN + QKV_COLS],
                           jnp.pad(w_l[:, OFF_GDN + QKV_COLS:OFF_Z], ((0, 0), (0, gate_pad))),
                           w_l[:, OFF_Z:OFF_MG]], axis=1).astype(BF16)
    b_g = jnp.concatenate([b_l[OFF_GDN:OFF_GDN + QKV_COLS],
                           jnp.pad(b_l[OFF_GDN + QKV_COLS:OFF_Z], (0, gate_pad)),
                           b_l[OFF_Z:OFF_MG]])[None, :]
    n_qg = QKV_COLS + LANES

    (xa_ctx,) = _inproj_raster(ctx, csc1, csh1, w_r[:, :LRU_WIDTH], b_r[:, :LRU_WIDTH],
                               [(0, LRU_WIDTH)], tt=32)
    zero_lru = jnp.zeros((nb, LRU_WIDTH), F32)
    (sa_f,) = _lru_direction(xa_ctx, lp, 0, zero_lru, False, "none", tt=32)
    (sa_b,) = _lru_direction(xa_ctx, lp, 1, zero_lru, True, "none", tt=32)

    nc_ctx = n_ctx // GDN_CHUNK
    qkv_c, lg_c = _inproj_column(ctx, lambda ci, g: (g, ci, 0), nc_ctx, csc1, csh1,
                                 w_g[:, :n_qg], b_g[:, :n_qg], [(0, QKV_COLS), (QKV_COLS, n_qg)])
    a_row = jnp.pad(-jnp.exp(gdn_a_log[layer].reshape(-1)), (0, LANES - 2 * GDN_HEADS))[None, :]
    dt_row = jnp.pad(gdn_dt_bias[layer].reshape(-1), (0, LANES - 2 * GDN_HEADS))[None, :]
    cw_qkv = conv_qkv_w[layer]
    qkv_c, gates_c = _gdn_prep(qkv_c, lg_c, cw_qkv, a_row, dt_row, GDN_GB)
    zero_gdn = jnp.zeros((nb, GDN_HEADS, GDN_DK, GDN_DV), F32)
    (sb_f,) = _gdn_direction(qkv_c, gates_c, zero_gdn, False, False, GDN_GB)
    (sb_b,) = _gdn_direction(qkv_c, gates_c, zero_gdn, True, False, GDN_GB)

    xa, ga, mg = _inproj_raster(x, sc1, sh1, w_r, b_r,
                                [(0, LRU_WIDTH), (LRU_WIDTH, 2 * LRU_WIDTH), (2 * LRU_WIDTH, 2 * LRU_WIDTH + 2 * d)],
                                tt=32)
    h_f, _ = _lru_direction(xa, lp, 0, sa_f, False, "h", tt=32)
    ya, _ = _lru_direction(xa, lp, 1, sa_b, True, "ya", tt=32, hf=h_f, ga=ga)

    qkv_raw, lg_l, z_l = _inproj_column_lat(x, sc1, sh1, w_g, b_g)
    qkv_l = _gdn_prep_lat(qkv_raw, cw_qkv)
    gates_l = _gdn_gates(lg_l, a_row, dt_row)
    o_f, _ = _gdn_direction(qkv_l, gates_l, sb_f, False, True, GDN_GB)
    o_b, _ = _gdn_direction(qkv_l, gates_l, sb_b, True, True, GDN_GB)
    yb = _gdn_out(o_f, o_b, z_l, gdn_norm_w[layer])

    w_rt = jnp.pad(jnp.concatenate([w_router_g[layer], w_router_e[layer]], axis=1),
                   ((0, 0), (0, LANES - N_GROUPS - N_EXPERTS)))
    b_rt = jnp.pad(jnp.concatenate([b_router_g[layer], b_router_e[layer]]),
                   (0, LANES - N_GROUPS - N_EXPERTS))[None, :]
    w_rt_hi = w_rt.astype(BF16)
    w_rt_lo = (w_rt - w_rt_hi.astype(F32)).astype(BF16)
    x1, h2, route = _mixer(ya, yb, mg, x, g1, sh2, sc2, w_pa[layer].astype(BF16), w_pb[layer].astype(BF16),
                           w_out[layer].astype(BF16), ln1_g[layer][None, :], ln1_b[layer][None, :],
                           w_rt_hi, w_rt_lo, b_rt, tt=32)

    e1 = route[:, 0].astype(jnp.int32)
    e2 = route[:, 1].astype(jnp.int32)
    src, dest1, dest2, block_expert, n_used = _route_plan(e1, e2, MOE_TM)
    x_pad = _sc_gather_rows(h2, src)
    y_pad = _experts(x_pad, block_expert, n_used, w_e_gate[layer], w_e_up[layer], w_e_down[layer], MOE_TM)
    y1 = _sc_gather_rows(y_pad, dest1)
    y2 = _sc_gather_rows(y_pad, dest2)
    return _final(x1, y1, y2, route, g2, ln2_g[layer][None, :], ln2_b[layer][None, :], tq=32)
```
